```python
import jax
import jax.numpy as jnp
from jax import lax
import numpy as np

D_MODEL = 2048
BATCH = 2
SEQ = 8192
DEPTH = 1

CTX_LEN = 256
GRID_W = 64
NORM_EPS = 1e-6

MLA_HEADS = 8
QK_NOPE_DIM = 128
QK_ROPE_DIM = 64
V_HEAD_DIM = 128
Q_LORA_RANK = 512
KV_LORA_RANK = 256
MLA_WIDTH = MLA_HEADS * V_HEAD_DIM
MLA_SCALE = (QK_NOPE_DIM + QK_ROPE_DIM) ** -0.5
ROPE_THETA = 10000.0
ROPE_AXIS_DIM = QK_ROPE_DIM // 2
Q_BLOCK = 128

RWKV_HEAD_DIM = 64
RWKV_WIDTH = D_MODEL - MLA_WIDTH
RWKV_HEADS = RWKV_WIDTH // RWKV_HEAD_DIM
DECAY_LORA = 64
ICLR_LORA = 64
GATE_LORA = 160
LNX_EPS = 64e-5

N_GROUPS = 4
EXPERTS_PER_GROUP = 8
N_EXPERTS = N_GROUPS * EXPERTS_PER_GROUP
TOP_K = 2
D_EXPERT = 512
MOE_BLOCK = 128

MLA_COLS = (Q_LORA_RANK, KV_LORA_RANK, QK_ROPE_DIM)
RWKV_COLS = (RWKV_WIDTH, RWKV_WIDTH, RWKV_WIDTH, DECAY_LORA, ICLR_LORA, GATE_LORA)
MLA_IN = sum(MLA_COLS)
RWKV_IN = sum(RWKV_COLS)
IN_COLS = MLA_IN + RWKV_IN

kernel_name = 'hybrid_mla_rwkv7_hmoe_dit_block'


def rms_norm(x, g):
    xf = x.astype(jnp.float32)
    y = xf * lax.rsqrt(jnp.mean(xf * xf, axis=-1, keepdims=True) + NORM_EPS)
    return (y * g).astype(x.dtype)


def modulate(x, g, shift, scale):
    return rms_norm(x, g) * (1 + scale) + shift


def split_cols(t, sizes):
    cuts = [int(i) for i in np.cumsum(sizes)[:-1]]
    return jnp.split(t, cuts, axis=-1)


def centred_shift(p, mu):
    zero = jnp.zeros_like(p[:, :1])
    prev = jnp.concatenate([zero, p[:, :-1]], axis=1)
    nxt = jnp.concatenate([p[:, 1:], zero], axis=1)
    return p + mu[0] * (prev - p) + mu[1] * (nxt - p)


def project_in(h, w_in, mu):
    p = h @ w_in
    p_mla, p_rwkv = p[..., :MLA_IN], p[..., MLA_IN:]
    return split_cols(p_mla, MLA_COLS) + split_cols(centred_shift(p_rwkv, mu), RWKV_COLS)


def axial_rope_tables(T):
    rows = T // GRID_W
    row, col = jnp.meshgrid(jnp.arange(rows), jnp.arange(GRID_W), indexing='ij')
    inv_freq = ROPE_THETA ** (-jnp.arange(0, ROPE_AXIS_DIM, 2, dtype=jnp.float32) / ROPE_AXIS_DIM)
    ang_r = row.reshape(-1)[:, None].astype(jnp.float32) * inv_freq
    ang_c = col.reshape(-1)[:, None].astype(jnp.float32) * inv_freq
    return (jnp.cos(ang_r), jnp.sin(ang_r), jnp.cos(ang_c), jnp.sin(ang_c))


def rotate_pair(x, cos, sin):
    x1, x2 = jnp.split(x, 2, axis=-1)
    return jnp.concatenate([x1 * cos - x2 * sin, x1 * sin + x2 * cos], axis=-1)


def rope_2d(x, tables):
    cr, sr, cc, sc = tables
    xr, xc = jnp.split(x, 2, axis=-1)
    return jnp.concatenate([rotate_pair(xr, cr, sr), rotate_pair(xc, cc, sc)], axis=-1).astype(x.dtype)


def mla_queries(c_q, q_norm_g, w_uq):
    B, T = c_q.shape[:2]
    q = (rms_norm(c_q, q_norm_g) @ w_uq).reshape(B, T, MLA_HEADS, QK_NOPE_DIM + QK_ROPE_DIM)
    return q[..., :QK_NOPE_DIM], q[..., QK_NOPE_DIM:]


def mla_keys_values(c_kv, kv_norm_g, w_ukv):
    B, T = c_kv.shape[:2]
    kv = (rms_norm(c_kv, kv_norm_g) @ w_ukv).reshape(B, T, MLA_HEADS, QK_NOPE_DIM + V_HEAD_DIM)
    return kv[..., :QK_NOPE_DIM], kv[..., QK_NOPE_DIM:]


def mla_attend(q_nope, q_rope, k_nope, k_rope, v):
    s = (jnp.einsum('bqhd,bkhd->bhqk', q_nope, k_nope, preferred_element_type=jnp.float32)
         + jnp.einsum('bqhd,bkd->bhqk', q_rope, k_rope, preferred_element_type=jnp.float32))
    p = jax.nn.softmax(s * MLA_SCALE, axis=-1)
    o = jnp.einsum('bhqk,bkhd->bqhd', p.astype(v.dtype), v)
    return o.reshape(o.shape[0], o.shape[1], MLA_WIDTH)


def mla_latent_attention(q_nope, q_rope, k_nope, k_rope, v):
    B, T = q_nope.shape[:2]
    nb = T // Q_BLOCK

    def blockify(t):
        return jnp.moveaxis(t.reshape(B, nb, Q_BLOCK, *t.shape[2:]), 1, 0)

    out = lax.map(lambda qs: mla_attend(qs[0], qs[1], k_nope, k_rope, v),
                  (blockify(q_nope), blockify(q_rope)))
    return jnp.moveaxis(out, 0, 1).reshape(B, T, MLA_WIDTH)


def heads(t):
    return t.reshape(*t.shape[:2], RWKV_HEADS, RWKV_HEAD_DIM)


def rwkv_direction(k, wl, al, w0, w_up, a0, a_up, key_a):
    w_raw = (w0 + jnp.tanh(wl) @ w_up).astype(jnp.float32)
    decay = jnp.exp(-jnp.exp(-jax.nn.softplus(-w_raw) - 0.5))
    a = jax.nn.sigmoid((a0 + al @ a_up).astype(jnp.float32))
    k_mod = k.astype(jnp.float32) * (1 + (a - 1) * key_a)
    return decay, a, k_mod


def rwkv_step(S, inp):
    r, w, k, v, kk, a = inp
    sa = jnp.einsum('bhvk,bhk->bhv', S, -kk)
    S = S * w[:, :, None, :] + sa[..., None] * (kk * a)[:, :, None, :] + v[..., None] * k[:, :, None, :]
    return S, jnp.einsum('bhvk,bhk->bhv', S, r)


def rwkv_scan(S0, r, w, k, v, kk, a, reverse):
    xs = tuple(jnp.moveaxis(heads(t), 1, 0) for t in (r, w, k, v, kk, a))
    S, ys = lax.scan(rwkv_step, S0, xs, reverse=reverse)
    return S, jnp.moveaxis(ys, 0, 1)


def rwkv_bidirectional(r, k, v, wl, al, S0_f, S0_b, params):
    w0, w_up, a0, a_up, key_k, key_a = params
    r32 = r.astype(jnp.float32)
    v32 = v.astype(jnp.float32)
    kk = heads(k.astype(jnp.float32) * key_k)
    kk = (kk * lax.rsqrt(jnp.sum(kk * kk, axis=-1, keepdims=True) + 1e-12)).reshape(k.shape)
    outs = []
    for d, (S0, rev) in enumerate(((S0_f, False), (S0_b, True))):
        decay, a, k_d = rwkv_direction(k, wl, al, w0[d], w_up[d], a0[d], a_up[d], key_a)
        S, y = rwkv_scan(S0, r32, decay, k_d, v32, kk, a, rev)
        outs.append((S, y, k_d))
    (s_f, y_f, k_f), (s_b, y_b, k_b) = outs
    return y_f + y_b, k_f + k_b, s_f, s_b


def rwkv_readout(y, r, k_sum, v, gl, gate_up, r_k, lnx_g, lnx_b):
    B, T = r.shape[:2]
    mu = jnp.mean(y, axis=-1, keepdims=True)
    var = jnp.mean(jnp.square(y - mu), axis=-1, keepdims=True)
    yn = ((y - mu) * lax.rsqrt(var + LNX_EPS)).reshape(B, T, RWKV_WIDTH) * lnx_g + lnx_b
    bonus = jnp.sum(heads(r.astype(jnp.float32)) * heads(k_sum) * r_k, axis=-1, keepdims=True) * heads(v.astype(jnp.float32))
    g = jax.nn.sigmoid(gl) @ gate_up
    return ((yn + bonus.reshape(B, T, RWKV_WIDTH)) * g).astype(r.dtype)


def hierarchical_route(x, w_grp, b_grp, w_exp, b_exp):
    N = x.shape[0]
    grp_prob = jax.nn.softmax((x @ w_grp + b_grp).astype(jnp.float32), axis=-1)
    g_val, g_idx = lax.top_k(grp_prob, 1)
    exp_logits = (x @ w_exp + b_exp).astype(jnp.float32).reshape(N, N_GROUPS, EXPERTS_PER_GROUP)
    within = exp_logits[jnp.arange(N), g_idx[:, 0]]
    e_val, e_idx = lax.top_k(jax.nn.softmax(within, axis=-1), TOP_K)
    gates = g_val * e_val / jnp.sum(e_val, axis=-1, keepdims=True)
    return g_idx * EXPERTS_PER_GROUP + e_idx, gates


def moe_ffn(h, w_grp, b_grp, w_exp, b_exp, w1, w3, w2):
    B, T, D = h.shape
    x = h.reshape(B * T, D)
    N = x.shape[0]
    expert_idx, gates = hierarchical_route(x, w_grp, b_grp, w_exp, b_exp)
    A = N * TOP_K
    n_blocks = (A + N_EXPERTS * (MOE_BLOCK - 1) + MOE_BLOCK - 1) // MOE_BLOCK
    P = n_blocks * MOE_BLOCK
    flat_e = expert_idx.reshape(-1)
    flat_tok = jnp.repeat(jnp.arange(N, dtype=jnp.int32), TOP_K)
    flat_gate = gates.reshape(-1)
    order = jnp.argsort(flat_e)
    sorted_e = flat_e[order]
    counts = jnp.bincount(flat_e, length=N_EXPERTS)
    padded = (counts + MOE_BLOCK - 1) // MOE_BLOCK * MOE_BLOCK
    start = jnp.cumsum(counts) - counts
    pad_end = jnp.cumsum(padded)
    pad_start = pad_end - padded
    dest = pad_start[sorted_e] + jnp.arange(A) - start[sorted_e]
    slot_tok = jnp.full((P,), N, jnp.int32).at[dest].set(flat_tok[order])
    slot_gate = jnp.zeros((P,), jnp.float32).at[dest].set(flat_gate[order])
    block_expert = jnp.minimum(
        jnp.searchsorted(pad_end, jnp.arange(n_blocks) * MOE_BLOCK, side='right'), N_EXPERTS - 1)
    x_pad = jnp.concatenate([x, jnp.zeros((1, D), x.dtype)], axis=0)

    def expert_block(args):
        tok, gate, e = args
        xb = x_pad[tok]
        yb = (jax.nn.silu(xb @ w1[e]) * (xb @ w3[e])) @ w2[e]
        return (yb * gate[:, None]).astype(x.dtype)

    y = lax.map(expert_block, (slot_tok.reshape(n_blocks, MOE_BLOCK),
                               slot_gate.reshape(n_blocks, MOE_BLOCK), block_expert))
    out = jnp.zeros((N + 1, D), x.dtype).at[slot_tok].add(y.reshape(P, D))[:N]
    return out.reshape(B, T, D)


def setup_inputs(seed: int = 0) -> dict:
    key = jax.random.key(seed)
    ks = iter(jax.random.split(key, 40))
    L, D = DEPTH, D_MODEL

    def nrm(shape, scale):
        return scale * jax.random.normal(next(ks), shape, jnp.float32)

    def uni(shape, lo, hi):
        return jax.random.uniform(next(ks), shape, jnp.float32, lo, hi)

    return {
        'x': nrm((BATCH, SEQ, D), 1.0),
        'c': nrm((BATCH, D), 1.0),
        'ctx': nrm((BATCH, CTX_LEN, D), 1.0),
        'c_ctx': nrm((D,), 1.0),
        'w_mod': nrm((L, D, 6 * D), 0.5 * D ** -0.5),
        'b_mod': nrm((L, 6 * D), 0.02),
        'norm_attn_g': 1.0 + nrm((L, D), 0.05),
        'norm_ffn_g': 1.0 + nrm((L, D), 0.05),
        'w_in': nrm((L, D, IN_COLS), D ** -0.5),
        'shift_mu': uni((L, 2, RWKV_IN), 0.0, 0.5),
        'q_norm_g': 1.0 + nrm((L, Q_LORA_RANK), 0.05),
        'w_uq': nrm((L, Q_LORA_RANK, MLA_HEADS * (QK_NOPE_DIM + QK_ROPE_DIM)), Q_LORA_RANK ** -0.5),
        'kv_norm_g': 1.0 + nrm((L, KV_LORA_RANK), 0.05),
        'w_ukv': nrm((L, KV_LORA_RANK, MLA_HEADS * (QK_NOPE_DIM + V_HEAD_DIM)), KV_LORA_RANK ** -0.5),
        'decay_w0': nrm((L, 2, RWKV_WIDTH), 0.5),
        'decay_up': nrm((L, 2, DECAY_LORA, RWKV_WIDTH), 0.5 * DECAY_LORA ** -0.5),
        'iclr_a0': nrm((L, 2, RWKV_WIDTH), 0.5),
        'iclr_up': nrm((L, 2, ICLR_LORA, RWKV_WIDTH), 0.5 * ICLR_LORA ** -0.5),
        'gate_up': nrm((L, GATE_LORA, RWKV_WIDTH), GATE_LORA ** -0.5),
        'key_k': 0.85 + nrm((L, RWKV_WIDTH), 0.05),
        'key_a': 1.0 + nrm((L, RWKV_WIDTH), 0.05),
        'bonus_r_k': nrm((L, RWKV_HEADS, RWKV_HEAD_DIM), 0.1),
        'lnx_g': 1.0 + nrm((L, RWKV_WIDTH), 0.05),
        'lnx_b': nrm((L, RWKV_WIDTH), 0.02),
        'w_out': nrm((L, D, D), D ** -0.5),
        'w_grp': nrm((L, D, N_GROUPS), D ** -0.5),
        'b_grp': nrm((L, N_GROUPS), 0.01),
        'w_exp': nrm((L, D, N_EXPERTS), D ** -0.5),
        'b_exp': nrm((L, N_EXPERTS), 0.01),
        'w1': nrm((L, N_EXPERTS, D, D_EXPERT), D ** -0.5),
        'w3': nrm((L, N_EXPERTS, D, D_EXPERT), D ** -0.5),
        'w2': nrm((L, N_EXPERTS, D_EXPERT, D), D_EXPERT ** -0.5),
        'final_norm_g': 1.0 + nrm((D,), 0.05),
    }


def reference(x, c, ctx, c_ctx, w_mod, b_mod, norm_attn_g, norm_ffn_g, w_in, shift_mu,
              q_norm_g, w_uq, kv_norm_g, w_ukv, decay_w0, decay_up, iclr_a0, iclr_up,
              gate_up, key_k, key_a, bonus_r_k, lnx_g, lnx_b, w_out,
              w_grp, b_grp, w_exp, b_exp, w1, w3, w2, final_norm_g):
    B, T, _ = x.shape
    rope_k = axial_rope_tables(T)
    rope_q = tuple(t[:, None, :] for t in rope_k)
    for l in range(DEPTH):
        mod = (jax.nn.silu(c) @ w_mod[l] + b_mod[l])[:, None, :]
        mod_c = jax.nn.silu(c_ctx) @ w_mod[l] + b_mod[l]
        sh1, sc1, g1, sh2, sc2, g2 = jnp.split(mod, 6, axis=-1)
        csh1, csc1, cg1, csh2, csc2, cg2 = jnp.split(mod_c, 6, axis=-1)

        h = modulate(x, norm_attn_g[l], sh1, sc1)
        hc = modulate(ctx, norm_attn_g[l], csh1, csc1)
        cq, ckv, kr, r, k, v, wl, al, gl = project_in(h, w_in[l], shift_mu[l])
        ccq, cckv, ckr, cr, ck, cv, cwl, cal, cgl = project_in(hc, w_in[l], shift_mu[l])

        qn, qr = mla_queries(cq, q_norm_g[l], w_uq[l])
        kn, vh = mla_keys_values(ckv, kv_norm_g[l], w_ukv[l])
        ckn, cvh = mla_keys_values(cckv, kv_norm_g[l], w_ukv[l])
        keys_nope = jnp.concatenate([kn, ckn], axis=1)
        keys_rope = jnp.concatenate([rope_2d(kr, rope_k), ckr], axis=1)
        values = jnp.concatenate([vh, cvh], axis=1)
        attn = mla_latent_attention(qn, rope_2d(qr, rope_q), keys_nope, keys_rope, values)

        rw = (decay_w0[l], decay_up[l], iclr_a0[l], iclr_up[l], key_k[l], key_a[l])
        zero = jnp.zeros((B, RWKV_HEADS, RWKV_HEAD_DIM, RWKV_HEAD_DIM), jnp.float32)
        cy, ck_sum, s_f, s_b = rwkv_bidirectional(cr, ck, cv, cwl, cal, zero, zero, rw)
        y, k_sum, _, _ = rwkv_bidirectional(r, k, v, wl, al, s_f, s_b, rw)
        rwkv = rwkv_readout(y, r, k_sum, v, gl, gate_up[l], bonus_r_k[l], lnx_g[l], lnx_b[l])

        x = x + g1 * (jnp.concatenate([attn, rwkv], axis=-1) @ w_out[l])
        x = x + g2 * moe_ffn(modulate(x, norm_ffn_g[l], sh2, sc2),
                             w_grp[l], b_grp[l], w_exp[l], b_exp[l], w1[l], w3[l], w2[l])

        if l < DEPTH - 1:
            cqn, cqr = mla_queries(ccq, q_norm_g[l], w_uq[l])
            cattn = mla_attend(cqn, cqr, ckn, ckr, cvh)
            crwkv = rwkv_readout(cy, cr, ck_sum, cv, cgl, gate_up[l], bonus_r_k[l], lnx_g[l], lnx_b[l])
            ctx = ctx + cg1 * (jnp.concatenate([cattn, crwkv], axis=-1) @ w_out[l])
            ctx = ctx + cg2 * moe_ffn(modulate(ctx, norm_ffn_g[l], csh2, csc2),
                                      w_grp[l], b_grp[l], w_exp[l], b_exp[l], w1[l], w3[l], w2[l])
    return rms_norm(x, final_norm_g)
```

```python
import functools
import math

import jax
import jax.numpy as jnp
import numpy as np
from jax import lax
from jax.experimental import pallas as pl
from jax.experimental.pallas import tpu as pltpu

F32 = jnp.float32
BF16 = jnp.bfloat16
HIGHEST = lax.Precision.HIGHEST

D_MODEL = 2048
CTX_LEN = 256
GRID_W = 64
NORM_EPS = 1e-6

MLA_HEADS = 8
QK_NOPE_DIM = 128
QK_ROPE_DIM = 64
V_HEAD_DIM = 128
Q_LORA_RANK = 512
KV_LORA_RANK = 256
MLA_WIDTH = MLA_HEADS * V_HEAD_DIM
MLA_SCALE = (QK_NOPE_DIM + QK_ROPE_DIM) ** -0.5
ROPE_THETA = 10000.0
ROPE_AXIS_DIM = QK_ROPE_DIM // 2
QK_PAD_DIM = 256

RWKV_HEAD_DIM = 64
RWKV_WIDTH = D_MODEL - MLA_WIDTH
RWKV_HEADS = RWKV_WIDTH // RWKV_HEAD_DIM
DECAY_LORA = 64
ICLR_LORA = 64
GATE_LORA = 160
LNX_EPS = 64e-5

N_GROUPS = 4
EXPERTS_PER_GROUP = 8
N_EXPERTS = N_GROUPS * EXPERTS_PER_GROUP
TOP_K = 2
D_EXPERT = 512
MOE_BLOCK = 128

MLA_IN = Q_LORA_RANK + KV_LORA_RANK + QK_ROPE_DIM
LANES = 128
TM = 256
CHUNK = 64
PAIR = 2 * RWKV_HEAD_DIM
N_PAIRS = RWKV_WIDTH // PAIR
PAIRS_PER_STEP = 8
ATTN_TQ = 256
ATTN_TK = 768
VMEM_LIMIT = 56 * 1024 * 1024

COLS_MLA = 1024
COLS_RKV = 3 * RWKV_WIDTH
COLS_LORA = 512
COLS_IN = COLS_MLA + COLS_RKV + COLS_LORA
ROUTER_COLS = 128


def _cparams(sem):
    return pltpu.CompilerParams(dimension_semantics=sem, vmem_limit_bytes=VMEM_LIMIT)


def _resident(shape, index_map):
    return pl.BlockSpec(shape, index_map, pipeline_mode=pl.Buffered(1))


def _dot(a, b):
    return jnp.dot(a, b, preferred_element_type=F32)


def _dot_nt(a, b):
    return lax.dot_general(a, b, (((1,), (1,)), ((), ())), preferred_element_type=F32)


def _dot_tn(a, b):
    return lax.dot_general(a, b, (((0,), (0,)), ((), ())), preferred_element_type=F32)


def _split2(x):
    hi = x.astype(BF16)
    lo = (x - hi.astype(F32)).astype(BF16)
    return hi, lo


def _split3(x):
    hi = x.astype(BF16)
    r1 = x - hi.astype(F32)
    mid = r1.astype(BF16)
    lo = (r1 - mid.astype(F32)).astype(BF16)
    return hi, mid, lo


def _mod_kernel(c_ref, w_ref, b_ref, o_ref):
    c = c_ref[...]
    s = c * jax.nn.sigmoid(c)
    o_ref[...] = jnp.dot(s, w_ref[...], preferred_element_type=F32, precision=HIGHEST) + b_ref[...]


def _mod_call(c_rows, w_mod, b_mod):
    n = w_mod.shape[1]
    tn = 1024
    return pl.pallas_call(
        _mod_kernel,
        grid=(n // tn,),
        in_specs=[
            pl.BlockSpec((8, D_MODEL), lambda i: (0, 0)),
            pl.BlockSpec((D_MODEL, tn), lambda i: (0, i)),
            pl.BlockSpec((1, tn), lambda i: (0, i)),
        ],
        out_specs=pl.BlockSpec((8, tn), lambda i: (0, i)),
        out_shape=jax.ShapeDtypeStruct((8, n), F32),
        compiler_params=_cparams(("arbitrary",)),
        name="mod",
    )(c_rows, w_mod, b_mod.reshape(1, n))


def _project_kernel(tpb, x_ref, ctx_ref, sh_ref, sc_ref, g_ref, w_ref, o_mla, o_rkv, o_lora):
    is_ctx = (pl.program_id(0) % tpb) == 0
    xin = jnp.where(is_ctx, ctx_ref[...], x_ref[...])
    ms = jnp.mean(xin * xin, axis=-1, keepdims=True)
    h = xin * lax.rsqrt(ms + NORM_EPS) * g_ref[...]
    hb = (h * (1.0 + sc_ref[...]) + sh_ref[...]).astype(BF16)
    o_mla[...] = _dot(hb, w_ref[:, 0:COLS_MLA])
    o_rkv[...] = _dot(hb, w_ref[:, COLS_MLA:COLS_MLA + COLS_RKV])
    o_lora[...] = _dot(hb, w_ref[:, COLS_MLA + COLS_RKV:COLS_IN])


def _mod_row(i, tpb, n_batch):
    return jnp.where(i % tpb == 0, n_batch, i // tpb)


def _project_call(x, ctx, mod_tab, norm_g, w_in_p, tpb):
    n_batch, t_len, _ = x.shape
    nt = n_batch * tpb * TM

    def mod_spec(k):
        return pl.BlockSpec((None, 1, D_MODEL), lambda i: (_mod_row(i, tpb, n_batch) * 6 + k, 0, 0))

    return pl.pallas_call(
        functools.partial(_project_kernel, tpb),
        grid=(n_batch * tpb,),
        in_specs=[
            pl.BlockSpec((None, TM, D_MODEL), lambda i: (i // tpb, jnp.maximum(i % tpb - 1, 0), 0)),
            pl.BlockSpec((None, TM, D_MODEL), lambda i: (i // tpb, 0, 0)),
            mod_spec(0),
            mod_spec(1),
            _resident((1, D_MODEL), lambda i: (0, 0)),
            _resident((D_MODEL, COLS_IN), lambda i: (0, 0)),
        ],
        out_specs=[
            pl.BlockSpec((TM, COLS_MLA), lambda i: (i, 0)),
            pl.BlockSpec((TM, COLS_RKV), lambda i: (i, 0)),
            pl.BlockSpec((TM, COLS_LORA), lambda i: (i, 0)),
        ],
        out_shape=[
            jax.ShapeDtypeStruct((nt, COLS_MLA), F32),
            jax.ShapeDtypeStruct((nt, COLS_RKV), F32),
            jax.ShapeDtypeStruct((nt, COLS_LORA), F32),
        ],
        compiler_params=_cparams(("arbitrary",)),
        name="project",
    )(x, ctx, mod_tab, mod_tab, norm_g.reshape(1, D_MODEL), w_in_p)


def _prep_kernel(tpb, p_ref, pp_ref, pn_ref, l_ref, lp_ref, ln_ref, mu_ref, mul_ref,
                 wup_ref, aup_ref, gup_ref, w0_ref, a0_ref,
                 r_o, k_o, v_o, lw_o, a_o, g_o):
    j = pl.program_id(0) % tpb
    no_prev = j <= 1
    no_next = (j == 0) | (j == tpb - 1)

    def shifted(main, prev_blk, next_blk, mu):
        rows = lax.broadcasted_iota(jnp.int32, main.shape, 0)
        prow = jnp.where(no_prev, 0.0, prev_blk[7:8, :])
        nrow = jnp.where(no_next, 0.0, next_blk[0:1, :])
        prev = jnp.where(rows == 0, prow, pltpu.roll(main, 1, 0))
        nxt = jnp.where(rows == TM - 1, nrow, pltpu.roll(main, TM - 1, 0))
        return main + mu[0:1, :] * (prev - main) + mu[1:2, :] * (nxt - main)

    for c, out in enumerate((r_o, k_o, v_o)):
        sl = slice(c * RWKV_WIDTH, (c + 1) * RWKV_WIDTH)
        out[...] = shifted(p_ref[:, sl], pp_ref[:, sl], pn_ref[:, sl], mu_ref[:, sl])

    lo = shifted(l_ref[...], lp_ref[...], ln_ref[...], mul_ref[...])
    wl = jnp.tanh(lo[:, 0:LANES]).astype(BF16)
    al = lo[:, LANES:2 * LANES].astype(BF16)
    gl = jax.nn.sigmoid(lo[:, 2 * LANES:4 * LANES]).astype(BF16)
    w_raw = w0_ref[...] + _dot(wl, wup_ref[...])
    lw_o[...] = -math.exp(-0.5) * jax.nn.sigmoid(w_raw)
    a_o[...] = jax.nn.sigmoid(a0_ref[...] + _dot(al, aup_ref[...]))
    g_o[...] = _dot(gl, gup_ref[...])


def _prep_call(p_rkv, p_lora, mu_rkv, mu_lora, wup, aup, gup, w0, a0, tpb):
    nt = p_rkv.shape[0]
    last8 = nt // 8 - 1
    sub = TM // 8

    def halo(cols):
        return [
            pl.BlockSpec((TM, cols), lambda i: (i, 0)),
            pl.BlockSpec((8, cols), lambda i: (jnp.maximum(i * sub - 1, 0), 0)),
            pl.BlockSpec((8, cols), lambda i: (jnp.minimum((i + 1) * sub, last8), 0)),
        ]

    w2 = 2 * RWKV_WIDTH
    return pl.pallas_call(
        functools.partial(_prep_kernel, tpb),
        grid=(nt // TM,),
        in_specs=halo(COLS_RKV) + halo(COLS_LORA) + [
            _resident((2, COLS_RKV), lambda i: (0, 0)),
            _resident((2, COLS_LORA), lambda i: (0, 0)),
            _resident((LANES, w2), lambda i: (0, 0)),
            _resident((LANES, w2), lambda i: (0, 0)),
            _resident((2 * LANES, RWKV_WIDTH), lambda i: (0, 0)),
            _resident((1, w2), lambda i: (0, 0)),
            _resident((1, w2), lambda i: (0, 0)),
        ],
        out_specs=[
            pl.BlockSpec((TM, RWKV_WIDTH), lambda i: (i, 0)),
            pl.BlockSpec((TM, RWKV_WIDTH), lambda i: (i, 0)),
            pl.BlockSpec((TM, RWKV_WIDTH), lambda i: (i, 0)),
            pl.BlockSpec((TM, w2), lambda i: (i, 0)),
            pl.BlockSpec((TM, w2), lambda i: (i, 0)),
            pl.BlockSpec((TM, RWKV_WIDTH), lambda i: (i, 0)),
        ],
        out_shape=[
            jax.ShapeDtypeStruct((nt, RWKV_WIDTH), F32),
            jax.ShapeDtypeStruct((nt, RWKV_WIDTH), F32),
            jax.ShapeDtypeStruct((nt, RWKV_WIDTH), F32),
            jax.ShapeDtypeStruct((nt, w2), F32),
            jax.ShapeDtypeStruct((nt, w2), F32),
            jax.ShapeDtypeStruct((nt, RWKV_WIDTH), F32),
        ],
        compiler_params=_cparams(("arbitrary",)),
        name="prep",
    )(p_rkv, p_rkv, p_rkv, p_lora, p_lora, p_lora, mu_rkv, mu_lora, wup, aup, gup, w0, a0)


def _stack_heads(x):
    lane = lax.broadcasted_iota(jnp.int32, x.shape, 1)
    zero = jnp.zeros_like(x)
    return jnp.concatenate([jnp.where(lane < RWKV_HEAD_DIM, x, zero),
                            jnp.where(lane >= RWKV_HEAD_DIM, x, zero)], axis=0)


def _unstack_heads(z):
    half = z.shape[0] // 2
    return z[:half] + z[half:]


def _scan_kernel(r_ref, k_ref, v_ref, lw_ref, a_ref, kkey_ref, akey_ref, y_ref, s_scr):
    rev = pl.program_id(0) == 1

    @pl.when(pl.program_id(3) == 0)
    def _():
        s_scr[...] = jnp.zeros_like(s_scr)

    c2 = 2 * CHUNK
    t64 = lax.broadcasted_iota(jnp.int32, (CHUNK, CHUNK), 0)
    i64 = lax.broadcasted_iota(jnp.int32, (CHUNK, CHUNK), 1)
    cum_mat = jnp.where(jnp.where(rev, t64 - i64, i64 - t64) <= 0, 1.0, 0.0).astype(BF16)

    row = lax.broadcasted_iota(jnp.int32, (c2, LANES), 0)
    col = lax.broadcasted_iota(jnp.int32, (c2, LANES), 1)
    t_idx = row % CHUNK
    i_idx = col % CHUNK
    order = jnp.where(rev, t_idx - i_idx, i_idx - t_idx)
    keep = (order < 0) | ((order == 0) & (row >= CHUNK))
    same_head = (row // RWKV_HEAD_DIM) == (col // RWKV_HEAD_DIM)
    eye = row == col
    ones_bd = jnp.where(same_head, 1.0, 0.0).astype(BF16)
    eye_f = jnp.where(eye, 1.0, 0.0)

    for p in range(PAIRS_PER_STEP):
        sl = slice(p * PAIR, (p + 1) * PAIR)
        r = r_ref[:, sl]
        k = k_ref[:, sl]
        v = v_ref[:, sl]
        lw = lw_ref[:, sl]
        lr = a_ref[:, sl]
        kraw = k * kkey_ref[:, sl]
        sq_hi, sq_lo = _split2(kraw * kraw)
        kk = kraw * lax.rsqrt(_dot(sq_hi, ones_bd) + _dot(sq_lo, ones_bd) + 1e-12)
        b = kk * lr
        kd = k * (1.0 + (lr - 1.0) * akey_ref[:, sl])

        w_hi, w_mid, w_lo = _split3(lw)
        lp = _dot(cum_mat, w_hi) + _dot(cum_mat, w_mid) + _dot(cum_mat, w_lo)
        ltot = jnp.where(rev, lp[0:1, :], lp[CHUNK - 1:CHUNK, :])
        e_inc = jnp.exp(lp)
        e_neg = jnp.exp(-lp)
        e_rest = jnp.exp(ltot - lp)
        at = -kk * jnp.exp(lp - lw)
        rt = r * e_inc
        bt = b * e_neg
        kt = kd * e_neg
        bh = (b * e_rest).astype(BF16)
        kh = (kd * e_rest).astype(BF16)
        vb = v.astype(BF16)
        sv = _stack_heads(vb)

        ar = jnp.concatenate([at, rt], axis=0).astype(BF16)
        zero = jnp.zeros((c2, LANES), F32)
        ab = jnp.where(keep, _dot_nt(ar, _stack_heads(bt.astype(BF16))), zero)
        ak = jnp.where(keep, _dot_nt(ar, _stack_heads(kt.astype(BF16))), zero)
        a_ab, a_rb = ab[:CHUNK], ab[CHUNK:].astype(BF16)
        a_ak, a_rk = ak[:CHUNK].astype(BF16), ak[CHUNK:].astype(BF16)

        pw = _stack_heads(a_ab)
        tm = eye_f + pw
        for _ in range(int(math.log2(CHUNK)) - 1):
            pwb = pw.astype(BF16)
            pw = _dot(pwb, pwb)
            tm = tm + _dot(tm.astype(BF16), pw.astype(BF16))
        t_p = _unstack_heads(tm).astype(BF16)

        x1 = _dot(a_ak, sv)
        wg = _dot(t_p, jnp.concatenate([_stack_heads(x1.astype(BF16)), _stack_heads(at.astype(BF16))], axis=1))
        w_b = wg[:, :LANES].astype(BF16)
        g_b = wg[:, LANES:].astype(BF16)
        qz = _dot(a_rb, jnp.concatenate([_stack_heads(g_b), _stack_heads(w_b)], axis=1))
        q = rt + qz[:, :LANES]
        z = qz[:, LANES:] + _dot(a_rk, sv)

        m_bd = jnp.where(same_head, _dot_tn(g_b, bh), zero)
        n_st = jnp.where(same_head,
                         _dot_tn(jnp.concatenate([w_b, vb], axis=0), jnp.concatenate([bh, kh], axis=0)), zero)

        s_old = s_scr[p]
        sb = s_old.astype(BF16)
        y_ref[:, sl] = _dot_nt(q.astype(BF16), _stack_heads(sb)) + z
        s_scr[p] = s_old * jnp.exp(ltot) + _dot(sb, m_bd.astype(BF16)) + _unstack_heads(n_st)


def _scan_call(r, k, v, lw, lr, key_k, key_a, n_batch, cpb, ctx_chunks):
    nt = r.shape[0]
    groups = N_PAIRS // PAIRS_PER_STEP
    gw = PAIRS_PER_STEP * PAIR

    def chunk_row(d, b, j):
        back = jnp.where(j < ctx_chunks, ctx_chunks - 1 - j, cpb + ctx_chunks - 1 - j)
        return b * cpb + jnp.where(d == 0, j, back)

    shared = pl.BlockSpec((CHUNK, gw), lambda d, b, g, j: (chunk_row(d, b, j), g))
    per_dir = pl.BlockSpec((CHUNK, gw), lambda d, b, g, j: (chunk_row(d, b, j), d * groups + g))
    keys = pl.BlockSpec((1, gw), lambda d, b, g, j: (0, g))
    return pl.pallas_call(
        _scan_kernel,
        grid=(2, n_batch, groups, cpb),
        in_specs=[shared, shared, shared, per_dir, per_dir, keys, keys],
        out_specs=pl.BlockSpec((None, CHUNK, gw), lambda d, b, g, j: (d, chunk_row(d, b, j), g)),
        out_shape=jax.ShapeDtypeStruct((2, nt, RWKV_WIDTH), F32),
        scratch_shapes=[pltpu.VMEM((PAIRS_PER_STEP, RWKV_HEAD_DIM, PAIR), F32)],
        compiler_params=_cparams(("arbitrary", "arbitrary", "arbitrary", "arbitrary")),
        name="scan",
    )(r, k, v, lw, lr, key_k, key_a)


def _mla_prep_kernel(p_ref, ck_ref, sk_ref, cq_ref, sq_ref, qg_ref, kvg_ref, wa_ref, wb_ref, wkv_ref,
                     q_o, k_o, v_o):
    cq = p_ref[:, 0:Q_LORA_RANK]
    cqn = (cq * lax.rsqrt(jnp.mean(cq * cq, axis=-1, keepdims=True) + NORM_EPS) * qg_ref[...]).astype(BF16)
    ckv = p_ref[:, Q_LORA_RANK:Q_LORA_RANK + KV_LORA_RANK]
    ckvn = (ckv * lax.rsqrt(jnp.mean(ckv * ckv, axis=-1, keepdims=True) + NORM_EPS) * kvg_ref[...]).astype(BF16)
    kr_a = p_ref[:, 768:896]
    kr_b = p_ref[:, 896:1024]
    k_rot = (kr_a * ck_ref[...] + kr_b * sk_ref[...]).astype(BF16)
    cos_q = cq_ref[...]
    sin_q = sq_ref[...]
    for h in range(MLA_HEADS):
        hs = slice(h * QK_PAD_DIM, (h + 1) * QK_PAD_DIM)
        q_o[:, hs] = (_dot(cqn, wa_ref[:, hs]) * cos_q + _dot(cqn, wb_ref[:, hs]) * sin_q).astype(BF16)
        k_o[:, h * QK_PAD_DIM:h * QK_PAD_DIM + QK_NOPE_DIM] = _dot(
            ckvn, wkv_ref[:, h * QK_NOPE_DIM:(h + 1) * QK_NOPE_DIM]).astype(BF16)
        k_o[:, h * QK_PAD_DIM + QK_NOPE_DIM:(h + 1) * QK_PAD_DIM] = k_rot
    v_o[...] = _dot(ckvn, wkv_ref[:, MLA_WIDTH:2 * MLA_WIDTH]).astype(BF16)


def _mla_prep_call(p_mla, tabs, q_norm_g, kv_norm_g, wa, wb, wkv, tpb):
    nt = p_mla.shape[0]
    ck, sk, cq, sq = tabs
    qw = MLA_HEADS * QK_PAD_DIM
    return pl.pallas_call(
        _mla_prep_kernel,
        grid=(nt // TM,),
        in_specs=[
            pl.BlockSpec((TM, COLS_MLA), lambda i: (i, 0)),
            pl.BlockSpec((TM, LANES), lambda i: (i % tpb, 0)),
            pl.BlockSpec((TM, LANES), lambda i: (i % tpb, 0)),
            pl.BlockSpec((TM, QK_PAD_DIM), lambda i: (i % tpb, 0)),
            pl.BlockSpec((TM, QK_PAD_DIM), lambda i: (i % tpb, 0)),
            _resident((1, Q_LORA_RANK), lambda i: (0, 0)),
            _resident((1, KV_LORA_RANK), lambda i: (0, 0)),
            _resident((Q_LORA_RANK, qw), lambda i: (0, 0)),
            _resident((Q_LORA_RANK, qw), lambda i: (0, 0)),
            _resident((KV_LORA_RANK, 2 * MLA_WIDTH), lambda i: (0, 0)),
        ],
        out_specs=[
            pl.BlockSpec((TM, qw), lambda i: (i, 0)),
            pl.BlockSpec((TM, qw), lambda i: (i, 0)),
            pl.BlockSpec((TM, MLA_WIDTH), lambda i: (i, 0)),
        ],
        out_shape=[
            jax.ShapeDtypeStruct((nt, qw), BF16),
            jax.ShapeDtypeStruct((nt, qw), BF16),
            jax.ShapeDtypeStruct((nt, MLA_WIDTH), BF16),
        ],
        compiler_params=_cparams(("arbitrary",)),
        name="mla_prep",
    )(p_mla, ck, sk, cq, sq, q_norm_g.reshape(1, -1), kv_norm_g.reshape(1, -1), wa, wb, wkv)


def _attn_kernel(n_kv, q_ref, k_ref, v_ref, o_ref):
    q = q_ref[...]

    def body(j, carry):
        m, l, acc = carry
        start = pl.multiple_of(j * ATTN_TK, ATTN_TK)
        kj = k_ref[pl.ds(start, ATTN_TK), :]
        vj = v_ref[pl.ds(start, ATTN_TK), :]
        s = _dot_nt(q, kj)
        m_new = jnp.maximum(m, jnp.max(s, axis=-1, keepdims=True))
        alpha = jnp.exp(m - m_new)
        p = jnp.exp(s - m_new)
        l = alpha * l + jnp.sum(p, axis=-1, keepdims=True)
        acc = alpha * acc + _dot(p.astype(BF16), vj)
        return m_new, l, acc

    init = (jnp.full((ATTN_TQ, 1), -jnp.inf, F32), jnp.zeros((ATTN_TQ, 1), F32),
            jnp.zeros((ATTN_TQ, V_HEAD_DIM), F32))
    _, l, acc = lax.fori_loop(0, n_kv, body, init)
    o_ref[...] = (acc / l).astype(o_ref.dtype)


def _attn_call(q, k, v, n_batch, t_len, tpb):
    rows_b = tpb * TM
    assert rows_b % ATTN_TK == 0 and t_len % ATTN_TQ == 0 and CTX_LEN % ATTN_TQ == 0
    n_q = t_len // ATTN_TQ
    q_off = CTX_LEN // ATTN_TQ
    k3 = k.reshape(n_batch, rows_b, MLA_HEADS * QK_PAD_DIM)
    v3 = v.reshape(n_batch, rows_b, MLA_WIDTH)
    return pl.pallas_call(
        functools.partial(_attn_kernel, rows_b // ATTN_TK),
        grid=(n_batch, MLA_HEADS, n_q),
        in_specs=[
            pl.BlockSpec((ATTN_TQ, QK_PAD_DIM), lambda b, h, i: (b * (rows_b // ATTN_TQ) + q_off + i, h)),
            pl.BlockSpec((None, rows_b, QK_PAD_DIM), lambda b, h, i: (b, 0, h)),
            pl.BlockSpec((None, rows_b, V_HEAD_DIM), lambda b, h, i: (b, 0, h)),
        ],
        out_specs=pl.BlockSpec((ATTN_TQ, V_HEAD_DIM), lambda b, h, i: (b * n_q + i, h)),
        out_shape=jax.ShapeDtypeStruct((n_batch * t_len, MLA_WIDTH), BF16),
        compiler_params=_cparams(("arbitrary", "arbitrary", "arbitrary")),
        name="attention",
    )(q, k3, v3)


def _head_sum(x, ones_bd):
    hi, lo = _split2(x)
    return _dot(hi, ones_bd) + _dot(lo, ones_bd)


def _route(logits):
    lane = lax.broadcasted_iota(jnp.int32, logits.shape, 1)
    neg = jnp.full_like(logits, -jnp.inf)
    big = jnp.full_like(lane, 2 ** 30)
    is_grp = lane < N_GROUPS
    gl = jnp.where(is_grp, logits, neg)
    ge = jnp.exp(gl - jnp.max(gl, axis=-1, keepdims=True))
    gp = ge / jnp.sum(ge, axis=-1, keepdims=True)
    g_val = jnp.max(gp, axis=-1, keepdims=True)
    g_idx = jnp.min(jnp.where(is_grp & (gp == g_val), lane, big), axis=-1, keepdims=True)
    e_lane = lane - N_GROUPS
    in_grp = (e_lane >= g_idx * EXPERTS_PER_GROUP) & (e_lane < (g_idx + 1) * EXPERTS_PER_GROUP)
    el = jnp.where(in_grp, logits, neg)
    ee = jnp.exp(el - jnp.max(el, axis=-1, keepdims=True))
    ep = ee / jnp.sum(ee, axis=-1, keepdims=True)
    v1 = jnp.max(ep, axis=-1, keepdims=True)
    i1 = jnp.min(jnp.where(in_grp & (ep == v1), lane, big), axis=-1, keepdims=True)
    rest = in_grp & (lane != i1)
    v2 = jnp.max(jnp.where(rest, ep, neg), axis=-1, keepdims=True)
    i2 = jnp.min(jnp.where(rest & (ep == v2), lane, big), axis=-1, keepdims=True)
    denom = v1 + v2
    idx = jnp.where(lane == 0, i1 - N_GROUPS, jnp.where(lane == 1, i2 - N_GROUPS, 0))
    gate = jnp.where(lane == 0, g_val * v1 / denom, jnp.where(lane == 1, g_val * v2 / denom, 0.0))
    return idx, gate


def _mix_kernel(x_ref, attn_ref, yf_ref, yb_ref, r_ref, k_ref, v_ref, af_ref, ab_ref, g_ref,
                g1_ref, sh2_ref, sc2_ref, akey_ref, rk_ref, lng_ref, lnb_ref, ng_ref,
                ones_ref, wo_ref, wr_ref, br_ref,
                x1_o, h2_o, idx_o, gate_o):
    ones_bd = ones_ref[...]
    inv = 1.0 / RWKV_HEAD_DIM
    y = yf_ref[...] + yb_ref[...]
    mu = _head_sum(y, ones_bd) * inv
    dy = y - mu
    var = _head_sum(dy * dy, ones_bd) * inv
    yn = dy * lax.rsqrt(var + LNX_EPS) * lng_ref[...] + lnb_ref[...]
    k_sum = k_ref[...] * (2.0 + (af_ref[...] + ab_ref[...] - 2.0) * akey_ref[...])
    bonus = _head_sum(r_ref[...] * k_sum * rk_ref[...], ones_bd) * v_ref[...]
    rw = ((yn + bonus) * g_ref[...]).astype(BF16)
    o = _dot(attn_ref[...], wo_ref[0:MLA_WIDTH, :]) + _dot(rw, wo_ref[MLA_WIDTH:D_MODEL, :])
    x1 = x_ref[...] + g1_ref[...] * o
    x1_o[...] = x1
    h = x1 * lax.rsqrt(jnp.mean(x1 * x1, axis=-1, keepdims=True) + NORM_EPS) * ng_ref[...]
    h2 = h * (1.0 + sc2_ref[...]) + sh2_ref[...]
    h2_o[...] = h2
    logits = jnp.dot(h2, wr_ref[...], preferred_element_type=F32, precision=HIGHEST) + br_ref[...]
    idx, gate = _route(logits)
    idx_o[...] = idx
    gate_o[...] = gate


def _mix_call(x, attn, yscan, r, k, v, lr, g, mod_tab, key_a, bonus_rk, lnx_g, lnx_b, norm_g,
              ones_bd, w_out_b, w_router, b_router, tpb):
    n_batch, t_len, _ = x.shape
    tpl = t_len // TM
    n = n_batch * t_len
    groups = RWKV_WIDTH // RWKV_WIDTH

    def lat(i):
        return (i // tpl) * tpb + 1 + i % tpl

    def tok(cols, col_blk=0):
        return pl.BlockSpec((TM, cols), lambda i: (lat(i), col_blk))

    def mod_spec(kk):
        return pl.BlockSpec((None, 1, D_MODEL), lambda i: ((i // tpl) * 6 + kk, 0, 0))

    def vec(cols):
        return _resident((1, cols), lambda i: (0, 0))

    del groups
    return pl.pallas_call(
        _mix_kernel,
        grid=(n // TM,),
        in_specs=[
            pl.BlockSpec((None, TM, D_MODEL), lambda i: (i // tpl, i % tpl, 0)),
            pl.BlockSpec((TM, MLA_WIDTH), lambda i: (i, 0)),
            pl.BlockSpec((None, TM, RWKV_WIDTH), lambda i: (0, lat(i), 0)),
            pl.BlockSpec((None, TM, RWKV_WIDTH), lambda i: (1, lat(i), 0)),
            tok(RWKV_WIDTH), tok(RWKV_WIDTH), tok(RWKV_WIDTH),
            tok(RWKV_WIDTH, 0), tok(RWKV_WIDTH, 1), tok(RWKV_WIDTH),
            mod_spec(2), mod_spec(3), mod_spec(4),
            vec(RWKV_WIDTH), vec(RWKV_WIDTH), vec(RWKV_WIDTH), vec(RWKV_WIDTH), vec(D_MODEL),
            _resident((RWKV_WIDTH, RWKV_WIDTH), lambda i: (0, 0)),
            _resident((D_MODEL, D_MODEL), lambda i: (0, 0)),
            _resident((D_MODEL, ROUTER_COLS), lambda i: (0, 0)),
            vec(ROUTER_COLS),
        ],
        out_specs=[
            pl.BlockSpec((TM, D_MODEL), lambda i: (i, 0)),
            pl.BlockSpec((TM, D_MODEL), lambda i: (i, 0)),
            pl.BlockSpec((TM, ROUTER_COLS), lambda i: (i, 0)),
            pl.BlockSpec((TM, ROUTER_COLS), lambda i: (i, 0)),
        ],
        out_shape=[
            jax.ShapeDtypeStruct((n, D_MODEL), F32),
            jax.ShapeDtypeStruct((n, D_MODEL), F32),
            jax.ShapeDtypeStruct((n, ROUTER_COLS), jnp.int32),
            jax.ShapeDtypeStruct((n, ROUTER_COLS), F32),
        ],
        compiler_params=_cparams(("arbitrary",)),
        name="mix",
    )(x, attn, yscan, yscan, r, k, v, lr, lr, g, mod_tab, mod_tab, mod_tab,
      key_a, bonus_rk, lnx_g, lnx_b, norm_g.reshape(1, D_MODEL), ones_bd, w_out_b, w_router, b_router)


def _moe_kernel(be_ref, src_ref, h_hbm, w1_ref, w3_ref, w2_ref, y_hbm,
                xb, yb, w1b, w3b, w2b, sem_in, sem_out):
    i = pl.program_id(0)
    base = i * MOE_BLOCK

    def row_in(s, tok):
        return pltpu.make_async_copy(h_hbm.at[pl.ds(tok, 1)], xb.at[pl.ds(s, 1)], sem_in)

    def row_out(s, f):
        return pltpu.make_async_copy(yb.at[pl.ds(s, 1)], y_hbm.at[pl.ds(f, 1)], sem_out)

    @pl.when(src_ref[base] >= 0)
    def _():
        def start_in(s, c):
            row_in(s, jnp.maximum(src_ref[base + s], 0) // TOP_K).start()
            return c
        lax.fori_loop(0, MOE_BLOCK, start_in, 0)

        @pl.when((i == 0) | (be_ref[i] != be_ref[jnp.maximum(i - 1, 0)]))
        def _():
            w1b[...] = w1_ref[...].astype(BF16)
            w3b[...] = w3_ref[...].astype(BF16)
            w2b[...] = w2_ref[...].astype(BF16)

        def wait_in(s, c):
            row_in(s, 0).wait()
            return c
        lax.fori_loop(0, MOE_BLOCK, wait_in, 0)

        x = xb[...].astype(BF16)
        a1 = _dot(x, w1b[...])
        a3 = _dot(x, w3b[...])
        hm = (a1 * jax.nn.sigmoid(a1) * a3).astype(BF16)
        yb[...] = _dot(hm, w2b[...])

        def start_out(s, c):
            f = src_ref[base + s]

            @pl.when(f >= 0)
            def _():
                row_out(s, f).start()
            return c
        lax.fori_loop(0, MOE_BLOCK, start_out, 0)

        def wait_out(s, c):
            @pl.when(src_ref[base + s] >= 0)
            def _():
                row_out(s, 0).wait()
            return c
        lax.fori_loop(0, MOE_BLOCK, wait_out, 0)


def _moe_call(block_expert, slot_src, h2, w1, w3, w2, n_assign):
    n_blocks = block_expert.shape[0]

    def wspec(shape):
        return pl.BlockSpec((None,) + shape, lambda i, be, src: (be[i], 0, 0))

    return pl.pallas_call(
        _moe_kernel,
        grid_spec=pltpu.PrefetchScalarGridSpec(
            num_scalar_prefetch=2,
            grid=(n_blocks,),
            in_specs=[
                pl.BlockSpec(memory_space=pl.ANY),
                wspec((D_MODEL, D_EXPERT)),
                wspec((D_MODEL, D_EXPERT)),
                wspec((D_EXPERT, D_MODEL)),
            ],
            out_specs=pl.BlockSpec(memory_space=pl.ANY),
            scratch_shapes=[
                pltpu.VMEM((MOE_BLOCK, D_MODEL), F32),
                pltpu.VMEM((MOE_BLOCK, D_MODEL), F32),
                pltpu.VMEM((D_MODEL, D_EXPERT), BF16),
                pltpu.VMEM((D_MODEL, D_EXPERT), BF16),
                pltpu.VMEM((D_EXPERT, D_MODEL), BF16),
                pltpu.SemaphoreType.DMA,
                pltpu.SemaphoreType.DMA,
            ],
        ),
        out_shape=jax.ShapeDtypeStruct((n_assign, D_MODEL), F32),
        compiler_params=_cparams(("arbitrary",)),
        name="moe",
    )(block_expert, slot_src, h2, w1, w3, w2)


def _final_kernel(x1_ref, y_ref, gate_ref, g2_ref, ng_ref, o_ref):
    gate = gate_ref[...]
    y = y_ref[:, 0:D_MODEL] * gate[:, 0:1] + y_ref[:, D_MODEL:2 * D_MODEL] * gate[:, 1:2]
    x2 = x1_ref[...] + g2_ref[...] * y
    o_ref[...] = x2 * lax.rsqrt(jnp.mean(x2 * x2, axis=-1, keepdims=True) + NORM_EPS) * ng_ref[...]


def _final_call(x1, y2, gates, mod_tab, final_g, t_len):
    n = x1.shape[0]
    tpl = t_len // TM
    return pl.pallas_call(
        _final_kernel,
        grid=(n // TM,),
        in_specs=[
            pl.BlockSpec((TM, D_MODEL), lambda i: (i, 0)),
            pl.BlockSpec((TM, TOP_K * D_MODEL), lambda i: (i, 0)),
            pl.BlockSpec((TM, ROUTER_COLS), lambda i: (i, 0)),
            pl.BlockSpec((None, 1, D_MODEL), lambda i: ((i // tpl) * 6 + 5, 0, 0)),
            _resident((1, D_MODEL), lambda i: (0, 0)),
        ],
        out_specs=pl.BlockSpec((TM, D_MODEL), lambda i: (i, 0)),
        out_shape=jax.ShapeDtypeStruct((n, D_MODEL), F32),
        compiler_params=_cparams(("arbitrary",)),
        name="final",
    )(x1, y2, gates, mod_tab, final_g.reshape(1, D_MODEL))


def _pad_cols(w, width):
    return jnp.pad(w, ((0, 0), (0, width - w.shape[1])))


_ROPE_SWAP = np.concatenate([np.arange(16, 32), np.arange(0, 16), np.arange(48, 64), np.arange(32, 48)])


def _rope_tables(t_len):
    pos = jnp.arange(t_len)
    inv_freq = ROPE_THETA ** (-jnp.arange(0, ROPE_AXIS_DIM, 2, dtype=F32) / ROPE_AXIS_DIM)
    ang_r = (pos // GRID_W)[:, None].astype(F32) * inv_freq
    ang_c = (pos % GRID_W)[:, None].astype(F32) * inv_freq
    cos = jnp.concatenate([jnp.cos(ang_r)] * 2 + [jnp.cos(ang_c)] * 2, axis=1)
    sin = jnp.concatenate([-jnp.sin(ang_r), jnp.sin(ang_r), -jnp.sin(ang_c), jnp.sin(ang_c)], axis=1)
    cos = jnp.concatenate([jnp.ones((CTX_LEN, QK_ROPE_DIM), F32), cos], axis=0)
    sin = jnp.concatenate([jnp.zeros((CTX_LEN, QK_ROPE_DIM), F32), sin], axis=0)
    rows = cos.shape[0]
    z64 = jnp.zeros((rows, QK_ROPE_DIM), F32)
    ck = jnp.concatenate([cos, z64], axis=1)
    sk = jnp.concatenate([sin, z64], axis=1)
    cq = MLA_SCALE * jnp.concatenate([jnp.ones((rows, QK_NOPE_DIM), F32), cos, z64], axis=1)
    sq = MLA_SCALE * jnp.concatenate([jnp.zeros((rows, QK_NOPE_DIM), F32), sin, z64], axis=1)
    return ck, sk, cq, sq


def _dispatch(idx2, n_tokens):
    n_assign = n_tokens * TOP_K
    n_blocks = (n_assign + N_EXPERTS * (MOE_BLOCK - 1) + MOE_BLOCK - 1) // MOE_BLOCK
    flat_e = idx2.reshape(-1)
    order = jnp.argsort(flat_e).astype(jnp.int32)
    sorted_e = flat_e[order]
    counts = jnp.bincount(flat_e, length=N_EXPERTS)
    padded = (counts + MOE_BLOCK - 1) // MOE_BLOCK * MOE_BLOCK
    start = jnp.cumsum(counts) - counts
    pad_end = jnp.cumsum(padded)
    pad_start = pad_end - padded
    dest = pad_start[sorted_e] + jnp.arange(n_assign) - start[sorted_e]
    slot_src = jnp.full((n_blocks * MOE_BLOCK,), -1, jnp.int32).at[dest].set(order)
    block_expert = jnp.minimum(
        jnp.searchsorted(pad_end, jnp.arange(n_blocks) * MOE_BLOCK, side='right'), N_EXPERTS - 1)
    return block_expert.astype(jnp.int32), slot_src


def kernel(x, c, ctx, c_ctx, w_mod, b_mod, norm_attn_g, norm_ffn_g, w_in, shift_mu, q_norm_g, w_uq, kv_norm_g, w_ukv, decay_w0, decay_up, iclr_a0, iclr_up, gate_up, key_k, key_a, bonus_r_k, lnx_g, lnx_b, w_out, w_grp, b_grp, w_exp, b_exp, w1, w3, w2, final_norm_g):
    n_batch, t_len, _ = x.shape
    assert ctx.shape[1] == CTX_LEN == TM and t_len % TM == 0 and w_mod.shape[0] == 1
    tpb = (CTX_LEN + t_len) // TM
    cpb = (CTX_LEN + t_len) // CHUNK
    n = n_batch * t_len

    c_rows = jnp.zeros((8, D_MODEL), F32).at[:n_batch].set(c).at[n_batch].set(c_ctx)
    mod_tab = _mod_call(c_rows, w_mod[0], b_mod[0]).reshape(8 * 6, 1, D_MODEL)

    wi = w_in[0]
    w_kr = wi[:, 768:MLA_IN]
    o = MLA_IN
    w_in_p = jnp.concatenate([
        wi[:, 0:768], _pad_cols(w_kr, LANES), _pad_cols(w_kr[:, _ROPE_SWAP], LANES),
        wi[:, o:o + COLS_RKV],
        _pad_cols(wi[:, o + COLS_RKV:o + COLS_RKV + DECAY_LORA], LANES),
        _pad_cols(wi[:, o + COLS_RKV + DECAY_LORA:o + COLS_RKV + DECAY_LORA + ICLR_LORA], LANES),
        _pad_cols(wi[:, o + COLS_RKV + DECAY_LORA + ICLR_LORA:], 2 * LANES),
    ], axis=1).astype(BF16)
    p_mla, p_rkv, p_lora = _project_call(x, ctx, mod_tab, norm_attn_g[0], w_in_p, tpb)

    mu = shift_mu[0]
    mu_rkv = mu[:, 0:COLS_RKV]
    mu_lora = jnp.concatenate([
        _pad_cols(mu[:, COLS_RKV:COLS_RKV + DECAY_LORA], LANES),
        _pad_cols(mu[:, COLS_RKV + DECAY_LORA:COLS_RKV + DECAY_LORA + ICLR_LORA], LANES),
        _pad_cols(mu[:, COLS_RKV + DECAY_LORA + ICLR_LORA:], 2 * LANES)], axis=1)

    def lora_up(w):
        both = jnp.concatenate([w[0], w[1]], axis=1)
        return jnp.pad(both, ((0, LANES - both.shape[0]), (0, 0))).astype(BF16)

    gup = jnp.pad(gate_up[0], ((0, 2 * LANES - GATE_LORA), (0, 0))).astype(BF16)
    r, k, v, lw, lr, g = _prep_call(
        p_rkv, p_lora, mu_rkv, mu_lora, lora_up(decay_up[0]), lora_up(iclr_up[0]), gup,
        decay_w0[0].reshape(1, -1), iclr_a0[0].reshape(1, -1), tpb)
    key_k2 = key_k[0].reshape(1, -1)
    key_a2 = key_a[0].reshape(1, -1)
    yscan = _scan_call(r, k, v, lw, lr, key_k2, key_a2, n_batch, cpb, CTX_LEN // CHUNK)

    hd = QK_NOPE_DIM + QK_ROPE_DIM
    wq = w_uq[0].reshape(Q_LORA_RANK, MLA_HEADS, hd)
    zq = jnp.zeros((Q_LORA_RANK, MLA_HEADS, QK_ROPE_DIM), F32)
    wa = jnp.concatenate([wq, zq], axis=2).reshape(Q_LORA_RANK, -1).astype(BF16)
    wb = jnp.concatenate([jnp.zeros((Q_LORA_RANK, MLA_HEADS, QK_NOPE_DIM), F32),
                          wq[:, :, QK_NOPE_DIM:][:, :, _ROPE_SWAP], zq], axis=2
                         ).reshape(Q_LORA_RANK, -1).astype(BF16)
    wkv3 = w_ukv[0].reshape(KV_LORA_RANK, MLA_HEADS, QK_NOPE_DIM + V_HEAD_DIM)
    wkv = jnp.concatenate([wkv3[:, :, :QK_NOPE_DIM].reshape(KV_LORA_RANK, -1),
                           wkv3[:, :, QK_NOPE_DIM:].reshape(KV_LORA_RANK, -1)], axis=1).astype(BF16)
    q, kmat, vmat = _mla_prep_call(p_mla, _rope_tables(t_len), q_norm_g[0], kv_norm_g[0], wa, wb, wkv, tpb)
    attn = _attn_call(q, kmat, vmat, n_batch, t_len, tpb)

    lane = np.arange(RWKV_WIDTH)
    ones_bd = jnp.asarray((lane[:, None] // RWKV_HEAD_DIM) == (lane[None, :] // RWKV_HEAD_DIM), BF16)
    w_router = _pad_cols(jnp.concatenate([w_grp[0], w_exp[0]], axis=1), ROUTER_COLS)
    b_router = _pad_cols(jnp.concatenate([b_grp[0], b_exp[0]]).reshape(1, -1), ROUTER_COLS)
    x1, h2, idx, gates = _mix_call(
        x, attn, yscan, r, k, v, lr, g, mod_tab, key_a2, bonus_r_k[0].reshape(1, -1),
        lnx_g[0].reshape(1, -1), lnx_b[0].reshape(1, -1), norm_ffn_g[0], ones_bd,
        w_out[0].astype(BF16), w_router, b_router, tpb)

    block_expert, slot_src = _dispatch(idx[:, :TOP_K], n)
    y = _moe_call(block_expert, slot_src, h2, w1[0], w3[0], w2[0], n * TOP_K)
    out = _final_call(x1, y.reshape(n, TOP_K * D_MODEL), gates, mod_tab, final_norm_g, t_len)
    return out.reshape(n_batch, t_len, D_MODEL)
```

```python
import functools
import math

import jax
import jax.numpy as jnp
import numpy as np
from jax import lax
from jax.experimental import pallas as pl
from jax.experimental.pallas import tpu as pltpu

F32 = jnp.float32
BF16 = jnp.bfloat16
HIGHEST = lax.Precision.HIGHEST

D_MODEL = 2048
CTX_LEN = 256
GRID_W = 64
NORM_EPS = 1e-6

MLA_HEADS = 8
QK_NOPE_DIM = 128
QK_ROPE_DIM = 64
V_HEAD_DIM = 128
Q_LORA_RANK = 512
KV_LORA_RANK = 256
MLA_WIDTH = MLA_HEADS * V_HEAD_DIM
MLA_SCALE = (QK_NOPE_DIM + QK_ROPE_DIM) ** -0.5
ROPE_THETA = 10000.0
ROPE_AXIS_DIM = QK_ROPE_DIM // 2
QK_PAD_DIM = 256

RWKV_HEAD_DIM = 64
RWKV_WIDTH = D_MODEL - MLA_WIDTH
RWKV_HEADS = RWKV_WIDTH // RWKV_HEAD_DIM
DECAY_LORA = 64
ICLR_LORA = 64
GATE_LORA = 160
LNX_EPS = 64e-5

N_GROUPS = 4
EXPERTS_PER_GROUP = 8
N_EXPERTS = N_GROUPS * EXPERTS_PER_GROUP
TOP_K = 2
D_EXPERT = 512
MOE_BLOCK = 128

MLA_IN = Q_LORA_RANK + KV_LORA_RANK + QK_ROPE_DIM
LANES = 128
TM = 256
CHUNK = 64
PAIR = 2 * RWKV_HEAD_DIM
N_PAIRS = RWKV_WIDTH // PAIR
PAIRS_PER_STEP = 8
ATTN_TQ = 256
ATTN_TK = 768
VMEM_LIMIT = 56 * 1024 * 1024

COLS_MLA = 1024
COLS_RKV = 3 * RWKV_WIDTH
COLS_LORA = 512
COLS_IN = COLS_MLA + COLS_RKV + COLS_LORA
ROUTER_COLS = 128


def _cparams(sem):
    return pltpu.CompilerParams(dimension_semantics=sem, vmem_limit_bytes=VMEM_LIMIT)


def _resident(shape, index_map):
    return pl.BlockSpec(shape, index_map, pipeline_mode=pl.Buffered(1))


def _dot(a, b):
    return jnp.dot(a, b, preferred_element_type=F32)


def _dot_nt(a, b):
    return lax.dot_general(a, b, (((1,), (1,)), ((), ())), preferred_element_type=F32)


def _dot_tn(a, b):
    return lax.dot_general(a, b, (((0,), (0,)), ((), ())), preferred_element_type=F32)


def _split2(x):
    hi = x.astype(BF16)
    lo = (x - hi.astype(F32)).astype(BF16)
    return hi, lo


def _split3(x):
    hi = x.astype(BF16)
    r1 = x - hi.astype(F32)
    mid = r1.astype(BF16)
    lo = (r1 - mid.astype(F32)).astype(BF16)
    return hi, mid, lo


def _mod_kernel(c_ref, w_ref, b_ref, o_ref):
    c = c_ref[...]
    s = c * jax.nn.sigmoid(c)
    o_ref[...] = jnp.dot(s, w_ref[...], preferred_element_type=F32, precision=HIGHEST) + b_ref[...]


def _mod_call(c_rows, w_mod, b_mod):
    n = w_mod.shape[1]
    tn = 1024
    return pl.pallas_call(
        _mod_kernel,
        grid=(n // tn,),
        in_specs=[
            pl.BlockSpec((8, D_MODEL), lambda i: (0, 0)),
            pl.BlockSpec((D_MODEL, tn), lambda i: (0, i)),
            pl.BlockSpec((1, tn), lambda i: (0, i)),
        ],
        out_specs=pl.BlockSpec((8, tn), lambda i: (0, i)),
        out_shape=jax.ShapeDtypeStruct((8, n), F32),
        compiler_params=_cparams(("arbitrary",)),
        name="mod",
    )(c_rows, w_mod, b_mod.reshape(1, n))


def _project_kernel(tpb, x_ref, ctx_ref, sh_ref, sc_ref, g_ref, w_ref, o_mla, o_rkv, o_lora):
    is_ctx = (pl.program_id(0) % tpb) == 0
    xin = jnp.where(is_ctx, ctx_ref[...], x_ref[...])
    ms = jnp.mean(xin * xin, axis=-1, keepdims=True)
    h = xin * lax.rsqrt(ms + NORM_EPS) * g_ref[...]
    hb = (h * (1.0 + sc_ref[...]) + sh_ref[...]).astype(BF16)
    o_mla[...] = _dot(hb, w_ref[:, 0:COLS_MLA])
    o_rkv[...] = _dot(hb, w_ref[:, COLS_MLA:COLS_MLA + COLS_RKV])
    o_lora[...] = _dot(hb, w_ref[:, COLS_MLA + COLS_RKV:COLS_IN])


def _mod_row(i, tpb, n_batch):
    return jnp.where(i % tpb == 0, n_batch, i // tpb)


def _project_call(x, ctx, mod_tab, norm_g, w_in_p, tpb):
    n_batch, t_len, _ = x.shape
    nt = n_batch * tpb * TM

    def mod_spec(k):
        return pl.BlockSpec((None, 1, D_MODEL), lambda i: (_mod_row(i, tpb, n_batch) * 6 + k, 0, 0))

    return pl.pallas_call(
        functools.partial(_project_kernel, tpb),
        grid=(n_batch * tpb,),
        in_specs=[
            pl.BlockSpec((None, TM, D_MODEL), lambda i: (i // tpb, jnp.maximum(i % tpb - 1, 0), 0)),
            pl.BlockSpec((None, TM, D_MODEL), lambda i: (i // tpb, 0, 0)),
            mod_spec(0),
            mod_spec(1),
            _resident((1, D_MODEL), lambda i: (0, 0)),
            _resident((D_MODEL, COLS_IN), lambda i: (0, 0)),
        ],
        out_specs=[
            pl.BlockSpec((TM, COLS_MLA), lambda i: (i, 0)),
            pl.BlockSpec((TM, COLS_RKV), lambda i: (i, 0)),
            pl.BlockSpec((TM, COLS_LORA), lambda i: (i, 0)),
        ],
        out_shape=[
            jax.ShapeDtypeStruct((nt, COLS_MLA), F32),
            jax.ShapeDtypeStruct((nt, COLS_RKV), F32),
            jax.ShapeDtypeStruct((nt, COLS_LORA), F32),
        ],
        compiler_params=_cparams(("arbitrary",)),
        name="project",
    )(x, ctx, mod_tab, mod_tab, norm_g.reshape(1, D_MODEL), w_in_p)


def _prep_kernel(tpb, p_ref, pp_ref, pn_ref, l_ref, lp_ref, ln_ref, mu_ref, mul_ref,
                 wup_ref, aup_ref, gup_ref, w0_ref, a0_ref,
                 r_o, k_o, v_o, lw_o, a_o, g_o):
    j = pl.program_id(0) % tpb
    no_prev = j <= 1
    no_next = (j == 0) | (j == tpb - 1)

    def shifted(main, prev_blk, next_blk, mu):
        rows = lax.broadcasted_iota(jnp.int32, main.shape, 0)
        prow = jnp.where(no_prev, 0.0, prev_blk[7:8, :])
        nrow = jnp.where(no_next, 0.0, next_blk[0:1, :])
        prev = jnp.where(rows == 0, prow, pltpu.roll(main, 1, 0))
        nxt = jnp.where(rows == TM - 1, nrow, pltpu.roll(main, TM - 1, 0))
        return main + mu[0:1, :] * (prev - main) + mu[1:2, :] * (nxt - main)

    for c, out in enumerate((r_o, k_o, v_o)):
        sl = slice(c * RWKV_WIDTH, (c + 1) * RWKV_WIDTH)
        out[...] = shifted(p_ref[:, sl], pp_ref[:, sl], pn_ref[:, sl], mu_ref[:, sl])

    lo = shifted(l_ref[...], lp_ref[...], ln_ref[...], mul_ref[...])
    wl = jnp.tanh(lo[:, 0:LANES]).astype(BF16)
    al = lo[:, LANES:2 * LANES].astype(BF16)
    gl = jax.nn.sigmoid(lo[:, 2 * LANES:4 * LANES]).astype(BF16)
    w_raw = w0_ref[...] + _dot(wl, wup_ref[...])
    lw_o[...] = -math.exp(-0.5) * jax.nn.sigmoid(w_raw)
    a_o[...] = jax.nn.sigmoid(a0_ref[...] + _dot(al, aup_ref[...]))
    g_o[...] = _dot(gl, gup_ref[...])


def _prep_call(p_rkv, p_lora, mu_rkv, mu_lora, wup, aup, gup, w0, a0, tpb):
    nt = p_rkv.shape[0]
    last8 = nt // 8 - 1
    sub = TM // 8

    def halo(cols):
        return [
            pl.BlockSpec((TM, cols), lambda i: (i, 0)),
            pl.BlockSpec((8, cols), lambda i: (jnp.maximum(i * sub - 1, 0), 0)),
            pl.BlockSpec((8, cols), lambda i: (jnp.minimum((i + 1) * sub, last8), 0)),
        ]

    w2 = 2 * RWKV_WIDTH
    return pl.pallas_call(
        functools.partial(_prep_kernel, tpb),
        grid=(nt // TM,),
        in_specs=halo(COLS_RKV) + halo(COLS_LORA) + [
            _resident((2, COLS_RKV), lambda i: (0, 0)),
            _resident((2, COLS_LORA), lambda i: (0, 0)),
            _resident((LANES, w2), lambda i: (0, 0)),
            _resident((LANES, w2), lambda i: (0, 0)),
            _resident((2 * LANES, RWKV_WIDTH), lambda i: (0, 0)),
            _resident((1, w2), lambda i: (0, 0)),
            _resident((1, w2), lambda i: (0, 0)),
        ],
        out_specs=[
            pl.BlockSpec((TM, RWKV_WIDTH), lambda i: (i, 0)),
            pl.BlockSpec((TM, RWKV_WIDTH), lambda i: (i, 0)),
            pl.BlockSpec((TM, RWKV_WIDTH), lambda i: (i, 0)),
            pl.BlockSpec((TM, w2), lambda i: (i, 0)),
            pl.BlockSpec((TM, w2), lambda i: (i, 0)),
            pl.BlockSpec((TM, RWKV_WIDTH), lambda i: (i, 0)),
        ],
        out_shape=[
            jax.ShapeDtypeStruct((nt, RWKV_WIDTH), F32),
            jax.ShapeDtypeStruct((nt, RWKV_WIDTH), F32),
            jax.ShapeDtypeStruct((nt, RWKV_WIDTH), F32),
            jax.ShapeDtypeStruct((nt, w2), F32),
            jax.ShapeDtypeStruct((nt, w2), F32),
            jax.ShapeDtypeStruct((nt, RWKV_WIDTH), F32),
        ],
        compiler_params=_cparams(("arbitrary",)),
        name="prep",
    )(p_rkv, p_rkv, p_rkv, p_lora, p_lora, p_lora, mu_rkv, mu_lora, wup, aup, gup, w0, a0)


def _stack_heads(x):
    lane = lax.broadcasted_iota(jnp.int32, x.shape, 1)
    zero = jnp.zeros_like(x)
    return jnp.concatenate([jnp.where(lane < RWKV_HEAD_DIM, x, zero),
                            jnp.where(lane >= RWKV_HEAD_DIM, x, zero)], axis=0)


def _unstack_heads(z):
    half = z.shape[0] // 2
    return z[:half] + z[half:]


def _scan_kernel(r_ref, k_ref, v_ref, lw_ref, a_ref, kkey_ref, akey_ref, y_ref, s_scr):
    rev = pl.program_id(0) == 1

    @pl.when(pl.program_id(3) == 0)
    def _():
        s_scr[...] = jnp.zeros_like(s_scr)

    c2 = 2 * CHUNK
    t64 = lax.broadcasted_iota(jnp.int32, (CHUNK, CHUNK), 0)
    i64 = lax.broadcasted_iota(jnp.int32, (CHUNK, CHUNK), 1)
    cum_mat = jnp.where(jnp.where(rev, t64 - i64, i64 - t64) <= 0, 1.0, 0.0).astype(BF16)

    row = lax.broadcasted_iota(jnp.int32, (c2, LANES), 0)
    col = lax.broadcasted_iota(jnp.int32, (c2, LANES), 1)
    t_idx = row % CHUNK
    i_idx = col % CHUNK
    order = jnp.where(rev, t_idx - i_idx, i_idx - t_idx)
    keep = (order < 0) | ((order == 0) & (row >= CHUNK))
    same_head = (row // RWKV_HEAD_DIM) == (col // RWKV_HEAD_DIM)
    eye = row == col
    ones_bd = jnp.where(same_head, 1.0, 0.0).astype(BF16)
    eye_f = jnp.where(eye, 1.0, 0.0)

    zero = jnp.zeros((c2, LANES), F32)
    pairs = range(PAIRS_PER_STEP)

    def pair(x, p):
        return x[:, p * PAIR:(p + 1) * PAIR]

    r = r_ref[...]
    k = k_ref[...]
    lw = lw_ref[...]
    lr = a_ref[...]
    vb = v_ref[...].astype(BF16)
    kraw = k * kkey_ref[...]
    sq_hi, sq_lo = _split2(kraw * kraw)
    w_hi, w_mid, w_lo = _split3(lw)
    lp = _dot(cum_mat, w_hi) + _dot(cum_mat, w_mid) + _dot(cum_mat, w_lo)
    ssq = jnp.concatenate([_dot(pair(sq_hi, p), ones_bd) + _dot(pair(sq_lo, p), ones_bd) for p in pairs], axis=1)
    kk = kraw * lax.rsqrt(ssq + 1e-12)
    b = kk * lr
    kd = k * (1.0 + (lr - 1.0) * akey_ref[...])
    ltot = jnp.where(rev, lp[0:1, :], lp[CHUNK - 1:CHUNK, :])
    e_neg = jnp.exp(-lp)
    e_rest = jnp.exp(ltot - lp)
    e_tot = jnp.exp(ltot)
    at = -kk * jnp.exp(lp - lw)
    rt = r * jnp.exp(lp)
    at_b = at.astype(BF16)
    rt_b = rt.astype(BF16)
    bt_b = (b * e_neg).astype(BF16)
    kt_b = (kd * e_neg).astype(BF16)
    bh = (b * e_rest).astype(BF16)
    kh = (kd * e_rest).astype(BF16)

    sv = [_stack_heads(pair(vb, p)) for p in pairs]
    ar = [jnp.concatenate([pair(at_b, p), pair(rt_b, p)], axis=0) for p in pairs]
    ab = [jnp.where(keep, _dot_nt(ar[p], _stack_heads(pair(bt_b, p))), zero) for p in pairs]
    ak = [jnp.where(keep, _dot_nt(ar[p], _stack_heads(pair(kt_b, p))), zero) for p in pairs]
    a_rb = [ab[p][CHUNK:].astype(BF16) for p in pairs]
    a_ak = [ak[p][:CHUNK].astype(BF16) for p in pairs]
    a_rk = [ak[p][CHUNK:].astype(BF16) for p in pairs]
    x1 = [_dot(a_ak[p], sv[p]) for p in pairs]

    pw = [_stack_heads(ab[p][:CHUNK]) for p in pairs]
    tm = [eye_f + pw[p] for p in pairs]
    pw = [_dot(pw[p].astype(BF16), pw[p].astype(BF16)) for p in pairs]
    for _ in range(int(math.log2(CHUNK)) - 2):
        both = [_dot(jnp.concatenate([tm[p], pw[p]], axis=0).astype(BF16), pw[p].astype(BF16)) for p in pairs]
        tm = [tm[p] + both[p][:c2] for p in pairs]
        pw = [both[p][c2:] for p in pairs]
    tm = [tm[p] + _dot(tm[p].astype(BF16), pw[p].astype(BF16)) for p in pairs]
    t_p = [_unstack_heads(tm[p]).astype(BF16) for p in pairs]

    wg = [_dot(t_p[p], jnp.concatenate([_stack_heads(x1[p].astype(BF16)), _stack_heads(pair(at_b, p))], axis=1))
          for p in pairs]
    w_b = [wg[p][:, :LANES].astype(BF16) for p in pairs]
    g_b = [wg[p][:, LANES:].astype(BF16) for p in pairs]
    qz = [_dot(a_rb[p], jnp.concatenate([_stack_heads(g_b[p]), _stack_heads(w_b[p])], axis=1)) for p in pairs]
    z_rk = [_dot(a_rk[p], sv[p]) for p in pairs]
    m_bd = [jnp.where(same_head, _dot_tn(g_b[p], pair(bh, p)), zero).astype(BF16) for p in pairs]
    n_st = [jnp.where(same_head,
                      _dot_tn(jnp.concatenate([w_b[p], pair(vb, p)], axis=0),
                              jnp.concatenate([pair(bh, p), pair(kh, p)], axis=0)), zero) for p in pairs]

    s_old = [s_scr[p] for p in pairs]
    s_b = [s_old[p].astype(BF16) for p in pairs]
    for p in pairs:
        q = (pair(rt, p) + qz[p][:, :LANES]).astype(BF16)
        y_ref[:, p * PAIR:(p + 1) * PAIR] = _dot_nt(q, _stack_heads(s_b[p])) + qz[p][:, LANES:] + z_rk[p]
    for p in pairs:
        s_scr[p] = s_old[p] * pair(e_tot, p) + _dot(s_b[p], m_bd[p]) + _unstack_heads(n_st[p])


def _scan_call(r, k, v, lw, lr, key_k, key_a, n_batch, cpb, ctx_chunks):
    nt = r.shape[0]
    groups = N_PAIRS // PAIRS_PER_STEP
    gw = PAIRS_PER_STEP * PAIR

    def chunk_row(d, b, j):
        back = jnp.where(j < ctx_chunks, ctx_chunks - 1 - j, cpb + ctx_chunks - 1 - j)
        return b * cpb + jnp.where(d == 0, j, back)

    shared = pl.BlockSpec((CHUNK, gw), lambda d, b, g, j: (chunk_row(d, b, j), g))
    per_dir = pl.BlockSpec((CHUNK, gw), lambda d, b, g, j: (chunk_row(d, b, j), d * groups + g))
    keys = pl.BlockSpec((1, gw), lambda d, b, g, j: (0, g))
    return pl.pallas_call(
        _scan_kernel,
        grid=(2, n_batch, groups, cpb),
        in_specs=[shared, shared, shared, per_dir, per_dir, keys, keys],
        out_specs=pl.BlockSpec((None, CHUNK, gw), lambda d, b, g, j: (d, chunk_row(d, b, j), g)),
        out_shape=jax.ShapeDtypeStruct((2, nt, RWKV_WIDTH), F32),
        scratch_shapes=[pltpu.VMEM((PAIRS_PER_STEP, RWKV_HEAD_DIM, PAIR), F32)],
        compiler_params=_cparams(("arbitrary", "arbitrary", "arbitrary", "arbitrary")),
        name="scan",
    )(r, k, v, lw, lr, key_k, key_a)


def _mla_prep_kernel(p_ref, ck_ref, sk_ref, cq_ref, sq_ref, qg_ref, kvg_ref, wa_ref, wb_ref, wkv_ref,
                     q_o, k_o, v_o):
    cq = p_ref[:, 0:Q_LORA_RANK]
    cqn = (cq * lax.rsqrt(jnp.mean(cq * cq, axis=-1, keepdims=True) + NORM_EPS) * qg_ref[...]).astype(BF16)
    ckv = p_ref[:, Q_LORA_RANK:Q_LORA_RANK + KV_LORA_RANK]
    ckvn = (ckv * lax.rsqrt(jnp.mean(ckv * ckv, axis=-1, keepdims=True) + NORM_EPS) * kvg_ref[...]).astype(BF16)
    kr_a = p_ref[:, 768:896]
    kr_b = p_ref[:, 896:1024]
    k_rot = (kr_a * ck_ref[...] + kr_b * sk_ref[...]).astype(BF16)
    cos_q = cq_ref[...]
    sin_q = sq_ref[...]
    for h in range(MLA_HEADS):
        hs = slice(h * QK_PAD_DIM, (h + 1) * QK_PAD_DIM)
        q_o[:, hs] = (_dot(cqn, wa_ref[:, hs]) * cos_q + _dot(cqn, wb_ref[:, hs]) * sin_q).astype(BF16)
        k_o[:, h * QK_PAD_DIM:h * QK_PAD_DIM + QK_NOPE_DIM] = _dot(
            ckvn, wkv_ref[:, h * QK_NOPE_DIM:(h + 1) * QK_NOPE_DIM]).astype(BF16)
        k_o[:, h * QK_PAD_DIM + QK_NOPE_DIM:(h + 1) * QK_PAD_DIM] = k_rot
    v_o[...] = _dot(ckvn, wkv_ref[:, MLA_WIDTH:2 * MLA_WIDTH]).astype(BF16)


def _mla_prep_call(p_mla, tabs, q_norm_g, kv_norm_g, wa, wb, wkv, tpb):
    nt = p_mla.shape[0]
    ck, sk, cq, sq = tabs
    qw = MLA_HEADS * QK_PAD_DIM
    return pl.pallas_call(
        _mla_prep_kernel,
        grid=(nt // TM,),
        in_specs=[
            pl.BlockSpec((TM, COLS_MLA), lambda i: (i, 0)),
            pl.BlockSpec((TM, LANES), lambda i: (i % tpb, 0)),
            pl.BlockSpec((TM, LANES), lambda i: (i % tpb, 0)),
            pl.BlockSpec((TM, QK_PAD_DIM), lambda i: (i % tpb, 0)),
            pl.BlockSpec((TM, QK_PAD_DIM), lambda i: (i % tpb, 0)),
            _resident((1, Q_LORA_RANK), lambda i: (0, 0)),
            _resident((1, KV_LORA_RANK), lambda i: (0, 0)),
            _resident((Q_LORA_RANK, qw), lambda i: (0, 0)),
            _resident((Q_LORA_RANK, qw), lambda i: (0, 0)),
            _resident((KV_LORA_RANK, 2 * MLA_WIDTH), lambda i: (0, 0)),
        ],
        out_specs=[
            pl.BlockSpec((TM, qw), lambda i: (i, 0)),
            pl.BlockSpec((TM, qw), lambda i: (i, 0)),
            pl.BlockSpec((TM, MLA_WIDTH), lambda i: (i, 0)),
        ],
        out_shape=[
            jax.ShapeDtypeStruct((nt, qw), BF16),
            jax.ShapeDtypeStruct((nt, qw), BF16),
            jax.ShapeDtypeStruct((nt, MLA_WIDTH), BF16),
        ],
        compiler_params=_cparams(("arbitrary",)),
        name="mla_prep",
    )(p_mla, ck, sk, cq, sq, q_norm_g.reshape(1, -1), kv_norm_g.reshape(1, -1), wa, wb, wkv)


def _attn_kernel(n_kv, q_ref, k_ref, v_ref, o_ref):
    q = q_ref[...]

    m = jnp.full((ATTN_TQ, 1), -jnp.inf, F32)
    l = jnp.zeros((ATTN_TQ, 1), F32)
    acc = jnp.zeros((ATTN_TQ, V_HEAD_DIM), F32)
    for j in range(n_kv):
        kj = k_ref[j * ATTN_TK:(j + 1) * ATTN_TK, :]
        vj = v_ref[j * ATTN_TK:(j + 1) * ATTN_TK, :]
        s = _dot_nt(q, kj)
        m_new = jnp.maximum(m, jnp.max(s, axis=-1, keepdims=True))
        alpha = jnp.exp(m - m_new)
        p = jnp.exp(s - m_new)
        l = alpha * l + jnp.sum(p, axis=-1, keepdims=True)
        acc = alpha * acc + _dot(p.astype(BF16), vj)
        m = m_new
    o_ref[...] = (acc / l).astype(o_ref.dtype)


def _attn_call(q, k, v, n_batch, t_len, tpb):
    rows_b = tpb * TM
    assert rows_b % ATTN_TK == 0 and t_len % ATTN_TQ == 0 and CTX_LEN % ATTN_TQ == 0
    n_q = t_len // ATTN_TQ
    q_off = CTX_LEN // ATTN_TQ
    k3 = k.reshape(n_batch, rows_b, MLA_HEADS * QK_PAD_DIM)
    v3 = v.reshape(n_batch, rows_b, MLA_WIDTH)
    return pl.pallas_call(
        functools.partial(_attn_kernel, rows_b // ATTN_TK),
        grid=(n_batch, MLA_HEADS, n_q),
        in_specs=[
            pl.BlockSpec((ATTN_TQ, QK_PAD_DIM), lambda b, h, i: (b * (rows_b // ATTN_TQ) + q_off + i, h)),
            pl.BlockSpec((None, rows_b, QK_PAD_DIM), lambda b, h, i: (b, 0, h)),
            pl.BlockSpec((None, rows_b, V_HEAD_DIM), lambda b, h, i: (b, 0, h)),
        ],
        out_specs=pl.BlockSpec((ATTN_TQ, V_HEAD_DIM), lambda b, h, i: (b * n_q + i, h)),
        out_shape=jax.ShapeDtypeStruct((n_batch * t_len, MLA_WIDTH), BF16),
        compiler_params=_cparams(("arbitrary", "arbitrary", "arbitrary")),
        name="attention",
    )(q, k3, v3)


def _head_sum(x, ones_bd):
    hi, lo = _split2(x)
    return _dot(hi, ones_bd) + _dot(lo, ones_bd)


def _route(logits):
    lane = lax.broadcasted_iota(jnp.int32, logits.shape, 1)
    neg = jnp.full_like(logits, -jnp.inf)
    big = jnp.full_like(lane, 2 ** 30)
    is_grp = lane < N_GROUPS
    gl = jnp.where(is_grp, logits, neg)
    ge = jnp.exp(gl - jnp.max(gl, axis=-1, keepdims=True))
    gp = ge / jnp.sum(ge, axis=-1, keepdims=True)
    g_val = jnp.max(gp, axis=-1, keepdims=True)
    g_idx = jnp.min(jnp.where(is_grp & (gp == g_val), lane, big), axis=-1, keepdims=True)
    e_lane = lane - N_GROUPS
    in_grp = (e_lane >= g_idx * EXPERTS_PER_GROUP) & (e_lane < (g_idx + 1) * EXPERTS_PER_GROUP)
    el = jnp.where(in_grp, logits, neg)
    ee = jnp.exp(el - jnp.max(el, axis=-1, keepdims=True))
    ep = ee / jnp.sum(ee, axis=-1, keepdims=True)
    v1 = jnp.max(ep, axis=-1, keepdims=True)
    i1 = jnp.min(jnp.where(in_grp & (ep == v1), lane, big), axis=-1, keepdims=True)
    rest = in_grp & (lane != i1)
    v2 = jnp.max(jnp.where(rest, ep, neg), axis=-1, keepdims=True)
    i2 = jnp.min(jnp.where(rest & (ep == v2), lane, big), axis=-1, keepdims=True)
    denom = v1 + v2
    idx = jnp.where(lane == 0, i1 - N_GROUPS, jnp.where(lane == 1, i2 - N_GROUPS, 0))
    gate = jnp.where(lane == 0, g_val * v1 / denom, jnp.where(lane == 1, g_val * v2 / denom, 0.0))
    return idx, gate


def _mix_kernel(x_ref, attn_ref, yf_ref, yb_ref, r_ref, k_ref, v_ref, af_ref, ab_ref, g_ref,
                g1_ref, sh2_ref, sc2_ref, akey_ref, rk_ref, lng_ref, lnb_ref, ng_ref,
                ones_ref, wo_ref, wr_ref, br_ref,
                x1_o, h2_o, idx_o, gate_o):
    ones_bd = ones_ref[...]
    inv = 1.0 / RWKV_HEAD_DIM
    y = yf_ref[...] + yb_ref[...]
    mu = _head_sum(y, ones_bd) * inv
    dy = y - mu
    var = _head_sum(dy * dy, ones_bd) * inv
    yn = dy * lax.rsqrt(var + LNX_EPS) * lng_ref[...] + lnb_ref[...]
    k_sum = k_ref[...] * (2.0 + (af_ref[...] + ab_ref[...] - 2.0) * akey_ref[...])
    bonus = _head_sum(r_ref[...] * k_sum * rk_ref[...], ones_bd) * v_ref[...]
    rw = ((yn + bonus) * g_ref[...]).astype(BF16)
    o = _dot(attn_ref[...], wo_ref[0:MLA_WIDTH, :]) + _dot(rw, wo_ref[MLA_WIDTH:D_MODEL, :])
    x1 = x_ref[...] + g1_ref[...] * o
    x1_o[...] = x1
    h = x1 * lax.rsqrt(jnp.mean(x1 * x1, axis=-1, keepdims=True) + NORM_EPS) * ng_ref[...]
    h2 = h * (1.0 + sc2_ref[...]) + sh2_ref[...]
    h2_o[...] = h2
    logits = jnp.dot(h2, wr_ref[...], preferred_element_type=F32, precision=HIGHEST) + br_ref[...]
    idx, gate = _route(logits)
    idx_o[...] = idx
    gate_o[...] = gate


def _mix_call(x, attn, yscan, r, k, v, lr, g, mod_tab, key_a, bonus_rk, lnx_g, lnx_b, norm_g,
              ones_bd, w_out_b, w_router, b_router, tpb):
    n_batch, t_len, _ = x.shape
    tpl = t_len // TM
    n = n_batch * t_len
    groups = RWKV_WIDTH // RWKV_WIDTH

    def lat(i):
        return (i // tpl) * tpb + 1 + i % tpl

    def tok(cols, col_blk=0):
        return pl.BlockSpec((TM, cols), lambda i: (lat(i), col_blk))

    def mod_spec(kk):
        return pl.BlockSpec((None, 1, D_MODEL), lambda i: ((i // tpl) * 6 + kk, 0, 0))

    def vec(cols):
        return _resident((1, cols), lambda i: (0, 0))

    del groups
    return pl.pallas_call(
        _mix_kernel,
        grid=(n // TM,),
        in_specs=[
            pl.BlockSpec((None, TM, D_MODEL), lambda i: (i // tpl, i % tpl, 0)),
            pl.BlockSpec((TM, MLA_WIDTH), lambda i: (i, 0)),
            pl.BlockSpec((None, TM, RWKV_WIDTH), lambda i: (0, lat(i), 0)),
            pl.BlockSpec((None, TM, RWKV_WIDTH), lambda i: (1, lat(i), 0)),
            tok(RWKV_WIDTH), tok(RWKV_WIDTH), tok(RWKV_WIDTH),
            tok(RWKV_WIDTH, 0), tok(RWKV_WIDTH, 1), tok(RWKV_WIDTH),
            mod_spec(2), mod_spec(3), mod_spec(4),
            vec(RWKV_WIDTH), vec(RWKV_WIDTH), vec(RWKV_WIDTH), vec(RWKV_WIDTH), vec(D_MODEL),
            _resident((RWKV_WIDTH, RWKV_WIDTH), lambda i: (0, 0)),
            _resident((D_MODEL, D_MODEL), lambda i: (0, 0)),
            _resident((D_MODEL, ROUTER_COLS), lambda i: (0, 0)),
            vec(ROUTER_COLS),
        ],
        out_specs=[
            pl.BlockSpec((TM, D_MODEL), lambda i: (i, 0)),
            pl.BlockSpec((TM, D_MODEL), lambda i: (i, 0)),
            pl.BlockSpec((TM, ROUTER_COLS), lambda i: (i, 0)),
            pl.BlockSpec((TM, ROUTER_COLS), lambda i: (i, 0)),
        ],
        out_shape=[
            jax.ShapeDtypeStruct((n, D_MODEL), F32),
            jax.ShapeDtypeStruct((n, D_MODEL), F32),
            jax.ShapeDtypeStruct((n, ROUTER_COLS), jnp.int32),
            jax.ShapeDtypeStruct((n, ROUTER_COLS), F32),
        ],
        compiler_params=_cparams(("arbitrary",)),
        name="mix",
    )(x, attn, yscan, yscan, r, k, v, lr, lr, g, mod_tab, mod_tab, mod_tab,
      key_a, bonus_rk, lnx_g, lnx_b, norm_g.reshape(1, D_MODEL), ones_bd, w_out_b, w_router, b_router)


def _moe_kernel(be_ref, src_ref, h_hbm, w1_ref, w3_ref, w2_ref, y_hbm,
                xb, yb, w1b, w3b, w2b, sem_in, sem_out):
    i = pl.program_id(0)
    base = i * MOE_BLOCK

    def row_in(s, tok):
        return pltpu.make_async_copy(h_hbm.at[pl.ds(tok, 1)], xb.at[pl.ds(s, 1)], sem_in)

    def row_out(s, f):
        return pltpu.make_async_copy(yb.at[pl.ds(s, 1)], y_hbm.at[pl.ds(f, 1)], sem_out)

    @pl.when(src_ref[base] >= 0)
    def _():
        def start_in(s, c):
            row_in(s, jnp.maximum(src_ref[base + s], 0) // TOP_K).start()
            return c
        lax.fori_loop(0, MOE_BLOCK, start_in, 0)

        @pl.when((i == 0) | (be_ref[i] != be_ref[jnp.maximum(i - 1, 0)]))
        def _():
            w1b[...] = w1_ref[...].astype(BF16)
            w3b[...] = w3_ref[...].astype(BF16)
            w2b[...] = w2_ref[...].astype(BF16)

        def wait_in(s, c):
            row_in(s, 0).wait()
            return c
        lax.fori_loop(0, MOE_BLOCK, wait_in, 0)

        x = xb[...].astype(BF16)
        a1 = _dot(x, w1b[...])
        a3 = _dot(x, w3b[...])
        hm = (a1 * jax.nn.sigmoid(a1) * a3).astype(BF16)
        yb[...] = _dot(hm, w2b[...])

        def start_out(s, c):
            f = src_ref[base + s]

            @pl.when(f >= 0)
            def _():
                row_out(s, f).start()
            return c
        lax.fori_loop(0, MOE_BLOCK, start_out, 0)

        def wait_out(s, c):
            @pl.when(src_ref[base + s] >= 0)
            def _():
                row_out(s, 0).wait()
            return c
        lax.fori_loop(0, MOE_BLOCK, wait_out, 0)


def _moe_call(block_expert, slot_src, h2, w1, w3, w2, n_assign):
    n_blocks = block_expert.shape[0]

    def wspec(shape):
        return pl.BlockSpec((None,) + shape, lambda i, be, src: (be[i], 0, 0))

    return pl.pallas_call(
        _moe_kernel,
        grid_spec=pltpu.PrefetchScalarGridSpec(
            num_scalar_prefetch=2,
            grid=(n_blocks,),
            in_specs=[
                pl.BlockSpec(memory_space=pl.ANY),
                wspec((D_MODEL, D_EXPERT)),
                wspec((D_MODEL, D_EXPERT)),
                wspec((D_EXPERT, D_MODEL)),
            ],
            out_specs=pl.BlockSpec(memory_space=pl.ANY),
            scratch_shapes=[
                pltpu.VMEM((MOE_BLOCK, D_MODEL), F32),
                pltpu.VMEM((MOE_BLOCK, D_MODEL), F32),
                pltpu.VMEM((D_MODEL, D_EXPERT), BF16),
                pltpu.VMEM((D_MODEL, D_EXPERT), BF16),
                pltpu.VMEM((D_EXPERT, D_MODEL), BF16),
                pltpu.SemaphoreType.DMA,
                pltpu.SemaphoreType.DMA,
            ],
        ),
        out_shape=jax.ShapeDtypeStruct((n_assign, D_MODEL), F32),
        compiler_params=_cparams(("arbitrary",)),
        name="moe",
    )(block_expert, slot_src, h2, w1, w3, w2)


def _final_kernel(x1_ref, y_ref, gate_ref, g2_ref, ng_ref, o_ref):
    gate = gate_ref[...]
    y = y_ref[:, 0:D_MODEL] * gate[:, 0:1] + y_ref[:, D_MODEL:2 * D_MODEL] * gate[:, 1:2]
    x2 = x1_ref[...] + g2_ref[...] * y
    o_ref[...] = x2 * lax.rsqrt(jnp.mean(x2 * x2, axis=-1, keepdims=True) + NORM_EPS) * ng_ref[...]


def _final_call(x1, y2, gates, mod_tab, final_g, t_len):
    n = x1.shape[0]
    tpl = t_len // TM
    return pl.pallas_call(
        _final_kernel,
        grid=(n // TM,),
        in_specs=[
            pl.BlockSpec((TM, D_MODEL), lambda i: (i, 0)),
            pl.BlockSpec((TM, TOP_K * D_MODEL), lambda i: (i, 0)),
            pl.BlockSpec((TM, ROUTER_COLS), lambda i: (i, 0)),
            pl.BlockSpec((None, 1, D_MODEL), lambda i: ((i // tpl) * 6 + 5, 0, 0)),
            _resident((1, D_MODEL), lambda i: (0, 0)),
        ],
        out_specs=pl.BlockSpec((TM, D_MODEL), lambda i: (i, 0)),
        out_shape=jax.ShapeDtypeStruct((n, D_MODEL), F32),
        compiler_params=_cparams(("arbitrary",)),
        name="final",
    )(x1, y2, gates, mod_tab, final_g.reshape(1, D_MODEL))


def _pad_cols(w, width):
    return jnp.pad(w, ((0, 0), (0, width - w.shape[1])))


_ROPE_SWAP = np.concatenate([np.arange(16, 32), np.arange(0, 16), np.arange(48, 64), np.arange(32, 48)])


def _rope_tables(t_len):
    pos = jnp.arange(t_len)
    inv_freq = ROPE_THETA ** (-jnp.arange(0, ROPE_AXIS_DIM, 2, dtype=F32) / ROPE_AXIS_DIM)
    ang_r = (pos // GRID_W)[:, None].astype(F32) * inv_freq
    ang_c = (pos % GRID_W)[:, None].astype(F32) * inv_freq
    cos = jnp.concatenate([jnp.cos(ang_r)] * 2 + [jnp.cos(ang_c)] * 2, axis=1)
    sin = jnp.concatenate([-jnp.sin(ang_r), jnp.sin(ang_r), -jnp.sin(ang_c), jnp.sin(ang_c)], axis=1)
    cos = jnp.concatenate([jnp.ones((CTX_LEN, QK_ROPE_DIM), F32), cos], axis=0)
    sin = jnp.concatenate([jnp.zeros((CTX_LEN, QK_ROPE_DIM), F32), sin], axis=0)
    rows = cos.shape[0]
    z64 = jnp.zeros((rows, QK_ROPE_DIM), F32)
    ck = jnp.concatenate([cos, z64], axis=1)
    sk = jnp.concatenate([sin, z64], axis=1)
    cq = MLA_SCALE * jnp.concatenate([jnp.ones((rows, QK_NOPE_DIM), F32), cos, z64], axis=1)
    sq = MLA_SCALE * jnp.concatenate([jnp.zeros((rows, QK_NOPE_DIM), F32), sin, z64], axis=1)
    return ck, sk, cq, sq


def _dispatch(idx2, n_tokens):
    n_assign = n_tokens * TOP_K
    n_blocks = (n_assign + N_EXPERTS * (MOE_BLOCK - 1) + MOE_BLOCK - 1) // MOE_BLOCK
    flat_e = idx2.reshape(-1)
    order = jnp.argsort(flat_e).astype(jnp.int32)
    sorted_e = flat_e[order]
    counts = jnp.bincount(flat_e, length=N_EXPERTS)
    padded = (counts + MOE_BLOCK - 1) // MOE_BLOCK * MOE_BLOCK
    start = jnp.cumsum(counts) - counts
    pad_end = jnp.cumsum(padded)
    pad_start = pad_end - padded
    dest = pad_start[sorted_e] + jnp.arange(n_assign) - start[sorted_e]
    slot_src = jnp.full((n_blocks * MOE_BLOCK,), -1, jnp.int32).at[dest].set(order)
    block_expert = jnp.minimum(
        jnp.searchsorted(pad_end, jnp.arange(n_blocks) * MOE_BLOCK, side='right'), N_EXPERTS - 1)
    return block_expert.astype(jnp.int32), slot_src


def kernel(x, c, ctx, c_ctx, w_mod, b_mod, norm_attn_g, norm_ffn_g, w_in, shift_mu, q_norm_g, w_uq, kv_norm_g, w_ukv, decay_w0, decay_up, iclr_a0, iclr_up, gate_up, key_k, key_a, bonus_r_k, lnx_g, lnx_b, w_out, w_grp, b_grp, w_exp, b_exp, w1, w3, w2, final_norm_g):
    n_batch, t_len, _ = x.shape
    assert ctx.shape[1] == CTX_LEN == TM and t_len % TM == 0 and w_mod.shape[0] == 1
    tpb = (CTX_LEN + t_len) // TM
    cpb = (CTX_LEN + t_len) // CHUNK
    n = n_batch * t_len

    c_rows = jnp.zeros((8, D_MODEL), F32).at[:n_batch].set(c).at[n_batch].set(c_ctx)
    mod_tab = _mod_call(c_rows, w_mod[0], b_mod[0]).reshape(8 * 6, 1, D_MODEL)

    wi = w_in[0]
    w_kr = wi[:, 768:MLA_IN]
    o = MLA_IN
    w_in_p = jnp.concatenate([
        wi[:, 0:768], _pad_cols(w_kr, LANES), _pad_cols(w_kr[:, _ROPE_SWAP], LANES),
        wi[:, o:o + COLS_RKV],
        _pad_cols(wi[:, o + COLS_RKV:o + COLS_RKV + DECAY_LORA], LANES),
        _pad_cols(wi[:, o + COLS_RKV + DECAY_LORA:o + COLS_RKV + DECAY_LORA + ICLR_LORA], LANES),
        _pad_cols(wi[:, o + COLS_RKV + DECAY_LORA + ICLR_LORA:], 2 * LANES),
    ], axis=1).astype(BF16)
    p_mla, p_rkv, p_lora = _project_call(x, ctx, mod_tab, norm_attn_g[0], w_in_p, tpb)

    mu = shift_mu[0]
    mu_rkv = mu[:, 0:COLS_RKV]
    mu_lora = jnp.concatenate([
        _pad_cols(mu[:, COLS_RKV:COLS_RKV + DECAY_LORA], LANES),
        _pad_cols(mu[:, COLS_RKV + DECAY_LORA:COLS_RKV + DECAY_LORA + ICLR_LORA], LANES),
        _pad_cols(mu[:, COLS_RKV + DECAY_LORA + ICLR_LORA:], 2 * LANES)], axis=1)

    def lora_up(w):
        both = jnp.concatenate([w[0], w[1]], axis=1)
        return jnp.pad(both, ((0, LANES - both.shape[0]), (0, 0))).astype(BF16)

    gup = jnp.pad(gate_up[0], ((0, 2 * LANES - GATE_LORA), (0, 0))).astype(BF16)
    r, k, v, lw, lr, g = _prep_call(
        p_rkv, p_lora, mu_rkv, mu_lora, lora_up(decay_up[0]), lora_up(iclr_up[0]), gup,
        decay_w0[0].reshape(1, -1), iclr_a0[0].reshape(1, -1), tpb)
    key_k2 = key_k[0].reshape(1, -1)
    key_a2 = key_a[0].reshape(1, -1)
    yscan = _scan_call(r, k, v, lw, lr, key_k2, key_a2, n_batch, cpb, CTX_LEN // CHUNK)

    hd = QK_NOPE_DIM + QK_ROPE_DIM
    wq = w_uq[0].reshape(Q_LORA_RANK, MLA_HEADS, hd)
    zq = jnp.zeros((Q_LORA_RANK, MLA_HEADS, QK_ROPE_DIM), F32)
    wa = jnp.concatenate([wq, zq], axis=2).reshape(Q_LORA_RANK, -1).astype(BF16)
    wb = jnp.concatenate([jnp.zeros((Q_LORA_RANK, MLA_HEADS, QK_NOPE_DIM), F32),
                          wq[:, :, QK_NOPE_DIM:][:, :, _ROPE_SWAP], zq], axis=2
                         ).reshape(Q_LORA_RANK, -1).astype(BF16)
    wkv3 = w_ukv[0].reshape(KV_LORA_RANK, MLA_HEADS, QK_NOPE_DIM + V_HEAD_DIM)
    wkv = jnp.concatenate([wkv3[:, :, :QK_NOPE_DIM].reshape(KV_LORA_RANK, -1),
                           wkv3[:, :, QK_NOPE_DIM:].reshape(KV_LORA_RANK, -1)], axis=1).astype(BF16)
    q, kmat, vmat = _mla_prep_call(p_mla, _rope_tables(t_len), q_norm_g[0], kv_norm_g[0], wa, wb, wkv, tpb)
    attn = _attn_call(q, kmat, vmat, n_batch, t_len, tpb)

    lane = np.arange(RWKV_WIDTH)
    ones_bd = jnp.asarray((lane[:, None] // RWKV_HEAD_DIM) == (lane[None, :] // RWKV_HEAD_DIM), BF16)
    w_router = _pad_cols(jnp.concatenate([w_grp[0], w_exp[0]], axis=1), ROUTER_COLS)
    b_router = _pad_cols(jnp.concatenate([b_grp[0], b_exp[0]]).reshape(1, -1), ROUTER_COLS)
    x1, h2, idx, gates = _mix_call(
        x, attn, yscan, r, k, v, lr, g, mod_tab, key_a2, bonus_r_k[0].reshape(1, -1),
        lnx_g[0].reshape(1, -1), lnx_b[0].reshape(1, -1), norm_ffn_g[0], ones_bd,
        w_out[0].astype(BF16), w_router, b_router, tpb)

    block_expert, slot_src = _dispatch(idx[:, :TOP_K], n)
    y = _moe_call(block_expert, slot_src, h2, w1[0], w3[0], w2[0], n * TOP_K)
    out = _final_call(x1, y.reshape(n, TOP_K * D_MODEL), gates, mod_tab, final_norm_g, t_len)
    return out.reshape(n_batch, t_len, D_MODEL)
```

```python
import functools
import math

import jax
import jax.numpy as jnp
import numpy as np
from jax import lax
from jax.experimental import pallas as pl
from jax.experimental.pallas import tpu as pltpu

F32 = jnp.float32
BF16 = jnp.bfloat16
HIGHEST = lax.Precision.HIGHEST

D_MODEL = 2048
CTX_LEN = 256
GRID_W = 64
NORM_EPS = 1e-6

MLA_HEADS = 8
QK_NOPE_DIM = 128
QK_ROPE_DIM = 64
V_HEAD_DIM = 128
Q_LORA_RANK = 512
KV_LORA_RANK = 256
MLA_WIDTH = MLA_HEADS * V_HEAD_DIM
MLA_SCALE = (QK_NOPE_DIM + QK_ROPE_DIM) ** -0.5
ROPE_THETA = 10000.0
ROPE_AXIS_DIM = QK_ROPE_DIM // 2
QK_PAD_DIM = 256

RWKV_HEAD_DIM = 64
RWKV_WIDTH = D_MODEL - MLA_WIDTH
RWKV_HEADS = RWKV_WIDTH // RWKV_HEAD_DIM
DECAY_LORA = 64
ICLR_LORA = 64
GATE_LORA = 160
LNX_EPS = 64e-5

N_GROUPS = 4
EXPERTS_PER_GROUP = 8
N_EXPERTS = N_GROUPS * EXPERTS_PER_GROUP
TOP_K = 2
D_EXPERT = 512
MOE_BLOCK = 128

MLA_IN = Q_LORA_RANK + KV_LORA_RANK + QK_ROPE_DIM
LANES = 128
TM = 256
CHUNK = 64
PAIR = 2 * RWKV_HEAD_DIM
N_PAIRS = RWKV_WIDTH // PAIR
PAIRS_PER_STEP = 8
ATTN_TQ = 256
ATTN_TK = 768
DMA_UNROLL = 8
VMEM_LIMIT = 56 * 1024 * 1024

COLS_MLA = 1024
COLS_RKV = 3 * RWKV_WIDTH
COLS_LORA = 512
COLS_IN = COLS_MLA + COLS_RKV + COLS_LORA
ROUTER_COLS = 128


def _cparams(sem):
    return pltpu.CompilerParams(dimension_semantics=sem, vmem_limit_bytes=VMEM_LIMIT)


def _resident(shape, index_map):
    return pl.BlockSpec(shape, index_map, pipeline_mode=pl.Buffered(1))


def _dot(a, b):
    return jnp.dot(a, b, preferred_element_type=F32)


def _dot_nt(a, b):
    return lax.dot_general(a, b, (((1,), (1,)), ((), ())), preferred_element_type=F32)


def _dot_tn(a, b):
    return lax.dot_general(a, b, (((0,), (0,)), ((), ())), preferred_element_type=F32)


def _split2(x):
    hi = x.astype(BF16)
    lo = (x - hi.astype(F32)).astype(BF16)
    return hi, lo


def _split3(x):
    hi = x.astype(BF16)
    r1 = x - hi.astype(F32)
    mid = r1.astype(BF16)
    lo = (r1 - mid.astype(F32)).astype(BF16)
    return hi, mid, lo


def _mod_kernel(c_ref, w_ref, b_ref, o_ref):
    c = c_ref[...]
    s = c * jax.nn.sigmoid(c)
    o_ref[...] = jnp.dot(s, w_ref[...], preferred_element_type=F32, precision=HIGHEST) + b_ref[...]


def _mod_call(c_rows, w_mod, b_mod):
    n = w_mod.shape[1]
    tn = 1024
    return pl.pallas_call(
        _mod_kernel,
        grid=(n // tn,),
        in_specs=[
            pl.BlockSpec((8, D_MODEL), lambda i: (0, 0)),
            pl.BlockSpec((D_MODEL, tn), lambda i: (0, i)),
            pl.BlockSpec((1, tn), lambda i: (0, i)),
        ],
        out_specs=pl.BlockSpec((8, tn), lambda i: (0, i)),
        out_shape=jax.ShapeDtypeStruct((8, n), F32),
        compiler_params=_cparams(("arbitrary",)),
        name="mod",
    )(c_rows, w_mod, b_mod.reshape(1, n))


def _project_kernel(tpb, x_ref, ctx_ref, sh_ref, sc_ref, g_ref, w_ref, o_mla, o_rkv, o_lora):
    is_ctx = (pl.program_id(0) % tpb) == 0
    xin = jnp.where(is_ctx, ctx_ref[...], x_ref[...])
    ms = jnp.mean(xin * xin, axis=-1, keepdims=True)
    h = xin * lax.rsqrt(ms + NORM_EPS) * g_ref[...]
    hb = (h * (1.0 + sc_ref[...]) + sh_ref[...]).astype(BF16)
    o_mla[...] = _dot(hb, w_ref[:, 0:COLS_MLA])
    o_rkv[...] = _dot(hb, w_ref[:, COLS_MLA:COLS_MLA + COLS_RKV])
    o_lora[...] = _dot(hb, w_ref[:, COLS_MLA + COLS_RKV:COLS_IN])


def _mod_row(i, tpb, n_batch):
    return jnp.where(i % tpb == 0, n_batch, i // tpb)


def _project_call(x, ctx, mod_tab, norm_g, w_in_p, tpb):
    n_batch, t_len, _ = x.shape
    nt = n_batch * tpb * TM

    def mod_spec(k):
        return pl.BlockSpec((None, 1, D_MODEL), lambda i: (_mod_row(i, tpb, n_batch) * 6 + k, 0, 0))

    return pl.pallas_call(
        functools.partial(_project_kernel, tpb),
        grid=(n_batch * tpb,),
        in_specs=[
            pl.BlockSpec((None, TM, D_MODEL), lambda i: (i // tpb, jnp.maximum(i % tpb - 1, 0), 0)),
            pl.BlockSpec((None, TM, D_MODEL), lambda i: (i // tpb, 0, 0)),
            mod_spec(0),
            mod_spec(1),
            _resident((1, D_MODEL), lambda i: (0, 0)),
            _resident((D_MODEL, COLS_IN), lambda i: (0, 0)),
        ],
        out_specs=[
            pl.BlockSpec((TM, COLS_MLA), lambda i: (i, 0)),
            pl.BlockSpec((TM, COLS_RKV), lambda i: (i, 0)),
            pl.BlockSpec((TM, COLS_LORA), lambda i: (i, 0)),
        ],
        out_shape=[
            jax.ShapeDtypeStruct((nt, COLS_MLA), F32),
            jax.ShapeDtypeStruct((nt, COLS_RKV), F32),
            jax.ShapeDtypeStruct((nt, COLS_LORA), F32),
        ],
        compiler_params=_cparams(("arbitrary",)),
        name="project",
    )(x, ctx, mod_tab, mod_tab, norm_g.reshape(1, D_MODEL), w_in_p)


def _prep_kernel(tpb, p_ref, pp_ref, pn_ref, l_ref, lp_ref, ln_ref, mu_ref, mul_ref,
                 wup_ref, aup_ref, gup_ref, w0_ref, a0_ref,
                 r_o, k_o, v_o, lw_o, a_o, g_o):
    j = pl.program_id(0) % tpb
    no_prev = j <= 1
    no_next = (j == 0) | (j == tpb - 1)

    def shifted(main, prev_blk, next_blk, mu):
        rows = lax.broadcasted_iota(jnp.int32, main.shape, 0)
        prow = jnp.where(no_prev, 0.0, prev_blk[7:8, :])
        nrow = jnp.where(no_next, 0.0, next_blk[0:1, :])
        prev = jnp.where(rows == 0, prow, pltpu.roll(main, 1, 0))
        nxt = jnp.where(rows == TM - 1, nrow, pltpu.roll(main, TM - 1, 0))
        return main + mu[0:1, :] * (prev - main) + mu[1:2, :] * (nxt - main)

    for c, out in enumerate((r_o, k_o, v_o)):
        sl = slice(c * RWKV_WIDTH, (c + 1) * RWKV_WIDTH)
        out[...] = shifted(p_ref[:, sl], pp_ref[:, sl], pn_ref[:, sl], mu_ref[:, sl])

    lo = shifted(l_ref[...], lp_ref[...], ln_ref[...], mul_ref[...])
    wl = jnp.tanh(lo[:, 0:LANES]).astype(BF16)
    al = lo[:, LANES:2 * LANES].astype(BF16)
    gl = jax.nn.sigmoid(lo[:, 2 * LANES:4 * LANES]).astype(BF16)
    w_raw = w0_ref[...] + _dot(wl, wup_ref[...])
    lw_o[...] = -math.exp(-0.5) * jax.nn.sigmoid(w_raw)
    a_o[...] = jax.nn.sigmoid(a0_ref[...] + _dot(al, aup_ref[...]))
    g_o[...] = _dot(gl, gup_ref[...])


def _prep_call(p_rkv, p_lora, mu_rkv, mu_lora, wup, aup, gup, w0, a0, tpb):
    nt = p_rkv.shape[0]
    last8 = nt // 8 - 1
    sub = TM // 8

    def halo(cols):
        return [
            pl.BlockSpec((TM, cols), lambda i: (i, 0)),
            pl.BlockSpec((8, cols), lambda i: (jnp.maximum(i * sub - 1, 0), 0)),
            pl.BlockSpec((8, cols), lambda i: (jnp.minimum((i + 1) * sub, last8), 0)),
        ]

    w2 = 2 * RWKV_WIDTH
    return pl.pallas_call(
        functools.partial(_prep_kernel, tpb),
        grid=(nt // TM,),
        in_specs=halo(COLS_RKV) + halo(COLS_LORA) + [
            _resident((2, COLS_RKV), lambda i: (0, 0)),
            _resident((2, COLS_LORA), lambda i: (0, 0)),
            _resident((LANES, w2), lambda i: (0, 0)),
            _resident((LANES, w2), lambda i: (0, 0)),
            _resident((2 * LANES, RWKV_WIDTH), lambda i: (0, 0)),
            _resident((1, w2), lambda i: (0, 0)),
            _resident((1, w2), lambda i: (0, 0)),
        ],
        out_specs=[
            pl.BlockSpec((TM, RWKV_WIDTH), lambda i: (i, 0)),
            pl.BlockSpec((TM, RWKV_WIDTH), lambda i: (i, 0)),
            pl.BlockSpec((TM, RWKV_WIDTH), lambda i: (i, 0)),
            pl.BlockSpec((TM, w2), lambda i: (i, 0)),
            pl.BlockSpec((TM, w2), lambda i: (i, 0)),
            pl.BlockSpec((TM, RWKV_WIDTH), lambda i: (i, 0)),
        ],
        out_shape=[
            jax.ShapeDtypeStruct((nt, RWKV_WIDTH), F32),
            jax.ShapeDtypeStruct((nt, RWKV_WIDTH), F32),
            jax.ShapeDtypeStruct((nt, RWKV_WIDTH), F32),
            jax.ShapeDtypeStruct((nt, w2), F32),
            jax.ShapeDtypeStruct((nt, w2), F32),
            jax.ShapeDtypeStruct((nt, RWKV_WIDTH), F32),
        ],
        compiler_params=_cparams(("arbitrary",)),
        name="prep",
    )(p_rkv, p_rkv, p_rkv, p_lora, p_lora, p_lora, mu_rkv, mu_lora, wup, aup, gup, w0, a0)


def _stack_heads(x):
    lane = lax.broadcasted_iota(jnp.int32, x.shape, 1)
    zero = jnp.zeros_like(x)
    return jnp.concatenate([jnp.where(lane < RWKV_HEAD_DIM, x, zero),
                            jnp.where(lane >= RWKV_HEAD_DIM, x, zero)], axis=0)


def _unstack_heads(z):
    half = z.shape[0] // 2
    return z[:half] + z[half:]


def _scan_kernel(r_ref, k_ref, v_ref, lw_ref, a_ref, kkey_ref, akey_ref, y_ref, s_scr):
    rev = pl.program_id(0) == 1

    @pl.when(pl.program_id(3) == 0)
    def _():
        s_scr[...] = jnp.zeros_like(s_scr)

    c2 = 2 * CHUNK
    t64 = lax.broadcasted_iota(jnp.int32, (CHUNK, CHUNK), 0)
    i64 = lax.broadcasted_iota(jnp.int32, (CHUNK, CHUNK), 1)
    cum_mat = jnp.where(jnp.where(rev, t64 - i64, i64 - t64) <= 0, 1.0, 0.0).astype(BF16)

    row = lax.broadcasted_iota(jnp.int32, (c2, LANES), 0)
    col = lax.broadcasted_iota(jnp.int32, (c2, LANES), 1)
    t_idx = row % CHUNK
    i_idx = col % CHUNK
    order = jnp.where(rev, t_idx - i_idx, i_idx - t_idx)
    keep = (order < 0) | ((order == 0) & (row >= CHUNK))
    same_head = (row // RWKV_HEAD_DIM) == (col // RWKV_HEAD_DIM)
    eye = row == col
    ones_bd = jnp.where(same_head, 1.0, 0.0).astype(BF16)
    eye_f = jnp.where(eye, 1.0, 0.0)

    zero = jnp.zeros((c2, LANES), F32)
    pairs = range(PAIRS_PER_STEP)

    def pair(x, p):
        return x[:, p * PAIR:(p + 1) * PAIR]

    r = r_ref[...]
    k = k_ref[...]
    lw = lw_ref[...]
    lr = a_ref[...]
    vb = v_ref[...].astype(BF16)
    kraw = k * kkey_ref[...]
    sq_hi, sq_lo = _split2(kraw * kraw)
    w_hi, w_mid, w_lo = _split3(lw)
    lp = _dot(cum_mat, w_hi) + _dot(cum_mat, w_mid) + _dot(cum_mat, w_lo)
    ssq = jnp.concatenate([_dot(pair(sq_hi, p), ones_bd) + _dot(pair(sq_lo, p), ones_bd) for p in pairs], axis=1)
    kk = kraw * lax.rsqrt(ssq + 1e-12)
    b = kk * lr
    kd = k * (1.0 + (lr - 1.0) * akey_ref[...])
    ltot = jnp.where(rev, lp[0:1, :], lp[CHUNK - 1:CHUNK, :])
    e_neg = jnp.exp(-lp)
    e_rest = jnp.exp(ltot - lp)
    e_tot = jnp.exp(ltot)
    at = -kk * jnp.exp(lp - lw)
    rt = r * jnp.exp(lp)
    at_b = at.astype(BF16)
    rt_b = rt.astype(BF16)
    bt_b = (b * e_neg).astype(BF16)
    kt_b = (kd * e_neg).astype(BF16)
    bh = (b * e_rest).astype(BF16)
    kh = (kd * e_rest).astype(BF16)

    sv = [_stack_heads(pair(vb, p)) for p in pairs]
    ar = [jnp.concatenate([pair(at_b, p), pair(rt_b, p)], axis=0) for p in pairs]
    ab = [jnp.where(keep, _dot_nt(ar[p], _stack_heads(pair(bt_b, p))), zero) for p in pairs]
    ak = [jnp.where(keep, _dot_nt(ar[p], _stack_heads(pair(kt_b, p))), zero) for p in pairs]
    a_rb = [ab[p][CHUNK:].astype(BF16) for p in pairs]
    a_ak = [ak[p][:CHUNK].astype(BF16) for p in pairs]
    a_rk = [ak[p][CHUNK:].astype(BF16) for p in pairs]
    x1 = [_dot(a_ak[p], sv[p]) for p in pairs]

    pw = [_stack_heads(ab[p][:CHUNK]) for p in pairs]
    tm = [eye_f + pw[p] for p in pairs]
    pw = [_dot(pw[p].astype(BF16), pw[p].astype(BF16)) for p in pairs]
    for _ in range(int(math.log2(CHUNK)) - 2):
        both = [_dot(jnp.concatenate([tm[p], pw[p]], axis=0).astype(BF16), pw[p].astype(BF16)) for p in pairs]
        tm = [tm[p] + both[p][:c2] for p in pairs]
        pw = [both[p][c2:] for p in pairs]
    tm = [tm[p] + _dot(tm[p].astype(BF16), pw[p].astype(BF16)) for p in pairs]
    t_p = [_unstack_heads(tm[p]).astype(BF16) for p in pairs]

    wg = [_dot(t_p[p], jnp.concatenate([_stack_heads(x1[p].astype(BF16)), _stack_heads(pair(at_b, p))], axis=1))
          for p in pairs]
    w_b = [wg[p][:, :LANES].astype(BF16) for p in pairs]
    g_b = [wg[p][:, LANES:].astype(BF16) for p in pairs]
    qz = [_dot(a_rb[p], jnp.concatenate([_stack_heads(g_b[p]), _stack_heads(w_b[p])], axis=1)) for p in pairs]
    z_rk = [_dot(a_rk[p], sv[p]) for p in pairs]
    m_bd = [jnp.where(same_head, _dot_tn(g_b[p], pair(bh, p)), zero).astype(BF16) for p in pairs]
    n_st = [jnp.where(same_head,
                      _dot_tn(jnp.concatenate([w_b[p], pair(vb, p)], axis=0),
                              jnp.concatenate([pair(bh, p), pair(kh, p)], axis=0)), zero) for p in pairs]

    s_old = [s_scr[p] for p in pairs]
    s_b = [s_old[p].astype(BF16) for p in pairs]
    for p in pairs:
        q = (pair(rt, p) + qz[p][:, :LANES]).astype(BF16)
        y_ref[:, p * PAIR:(p + 1) * PAIR] = _dot_nt(q, _stack_heads(s_b[p])) + qz[p][:, LANES:] + z_rk[p]
    for p in pairs:
        s_scr[p] = s_old[p] * pair(e_tot, p) + _dot(s_b[p], m_bd[p]) + _unstack_heads(n_st[p])


def _scan_call(r, k, v, lw, lr, key_k, key_a, n_batch, cpb, ctx_chunks):
    nt = r.shape[0]
    groups = N_PAIRS // PAIRS_PER_STEP
    gw = PAIRS_PER_STEP * PAIR

    def chunk_row(d, b, j):
        back = jnp.where(j < ctx_chunks, ctx_chunks - 1 - j, cpb + ctx_chunks - 1 - j)
        return b * cpb + jnp.where(d == 0, j, back)

    shared = pl.BlockSpec((CHUNK, gw), lambda d, b, g, j: (chunk_row(d, b, j), g))
    per_dir = pl.BlockSpec((CHUNK, gw), lambda d, b, g, j: (chunk_row(d, b, j), d * groups + g))
    keys = pl.BlockSpec((1, gw), lambda d, b, g, j: (0, g))
    return pl.pallas_call(
        _scan_kernel,
        grid=(2, n_batch, groups, cpb),
        in_specs=[shared, shared, shared, per_dir, per_dir, keys, keys],
        out_specs=pl.BlockSpec((None, CHUNK, gw), lambda d, b, g, j: (d, chunk_row(d, b, j), g)),
        out_shape=jax.ShapeDtypeStruct((2, nt, RWKV_WIDTH), F32),
        scratch_shapes=[pltpu.VMEM((PAIRS_PER_STEP, RWKV_HEAD_DIM, PAIR), F32)],
        compiler_params=_cparams(("arbitrary", "arbitrary", "arbitrary", "arbitrary")),
        name="scan",
    )(r, k, v, lw, lr, key_k, key_a)


def _mla_prep_kernel(p_ref, ck_ref, sk_ref, cq_ref, sq_ref, qg_ref, kvg_ref, wa_ref, wb_ref, wkv_ref,
                     q_o, k_o, v_o):
    cq = p_ref[:, 0:Q_LORA_RANK]
    cqn = (cq * lax.rsqrt(jnp.mean(cq * cq, axis=-1, keepdims=True) + NORM_EPS) * qg_ref[...]).astype(BF16)
    ckv = p_ref[:, Q_LORA_RANK:Q_LORA_RANK + KV_LORA_RANK]
    ckvn = (ckv * lax.rsqrt(jnp.mean(ckv * ckv, axis=-1, keepdims=True) + NORM_EPS) * kvg_ref[...]).astype(BF16)
    kr_a = p_ref[:, 768:896]
    kr_b = p_ref[:, 896:1024]
    k_rot = (kr_a * ck_ref[...] + kr_b * sk_ref[...]).astype(BF16)
    cos_q = cq_ref[...]
    sin_q = sq_ref[...]
    for h in range(MLA_HEADS):
        hs = slice(h * QK_PAD_DIM, (h + 1) * QK_PAD_DIM)
        q_o[:, hs] = (_dot(cqn, wa_ref[:, hs]) * cos_q + _dot(cqn, wb_ref[:, hs]) * sin_q).astype(BF16)
        k_o[:, h * QK_PAD_DIM:h * QK_PAD_DIM + QK_NOPE_DIM] = _dot(
            ckvn, wkv_ref[:, h * QK_NOPE_DIM:(h + 1) * QK_NOPE_DIM]).astype(BF16)
        k_o[:, h * QK_PAD_DIM + QK_NOPE_DIM:(h + 1) * QK_PAD_DIM] = k_rot
    v_o[...] = _dot(ckvn, wkv_ref[:, MLA_WIDTH:2 * MLA_WIDTH]).astype(BF16)


def _mla_prep_call(p_mla, tabs, q_norm_g, kv_norm_g, wa, wb, wkv, tpb):
    nt = p_mla.shape[0]
    ck, sk, cq, sq = tabs
    qw = MLA_HEADS * QK_PAD_DIM
    return pl.pallas_call(
        _mla_prep_kernel,
        grid=(nt // TM,),
        in_specs=[
            pl.BlockSpec((TM, COLS_MLA), lambda i: (i, 0)),
            pl.BlockSpec((TM, LANES), lambda i: (i % tpb, 0)),
            pl.BlockSpec((TM, LANES), lambda i: (i % tpb, 0)),
            pl.BlockSpec((TM, QK_PAD_DIM), lambda i: (i % tpb, 0)),
            pl.BlockSpec((TM, QK_PAD_DIM), lambda i: (i % tpb, 0)),
            _resident((1, Q_LORA_RANK), lambda i: (0, 0)),
            _resident((1, KV_LORA_RANK), lambda i: (0, 0)),
            _resident((Q_LORA_RANK, qw), lambda i: (0, 0)),
            _resident((Q_LORA_RANK, qw), lambda i: (0, 0)),
            _resident((KV_LORA_RANK, 2 * MLA_WIDTH), lambda i: (0, 0)),
        ],
        out_specs=[
            pl.BlockSpec((TM, qw), lambda i: (i, 0)),
            pl.BlockSpec((TM, qw), lambda i: (i, 0)),
            pl.BlockSpec((TM, MLA_WIDTH), lambda i: (i, 0)),
        ],
        out_shape=[
            jax.ShapeDtypeStruct((nt, qw), BF16),
            jax.ShapeDtypeStruct((nt, qw), BF16),
            jax.ShapeDtypeStruct((nt, MLA_WIDTH), BF16),
        ],
        compiler_params=_cparams(("arbitrary",)),
        name="mla_prep",
    )(p_mla, ck, sk, cq, sq, q_norm_g.reshape(1, -1), kv_norm_g.reshape(1, -1), wa, wb, wkv)


def _attn_kernel(n_kv, q_ref, k_ref, v_ref, o_ref):
    q = q_ref[...]

    m = jnp.full((ATTN_TQ, 1), -jnp.inf, F32)
    l = jnp.zeros((ATTN_TQ, 1), F32)
    acc = jnp.zeros((ATTN_TQ, V_HEAD_DIM), F32)
    for j in range(n_kv):
        kj = k_ref[j * ATTN_TK:(j + 1) * ATTN_TK, :]
        vj = v_ref[j * ATTN_TK:(j + 1) * ATTN_TK, :]
        s = _dot_nt(q, kj)
        m_new = jnp.maximum(m, jnp.max(s, axis=-1, keepdims=True))
        alpha = jnp.exp(m - m_new)
        p = jnp.exp(s - m_new)
        l = alpha * l + jnp.sum(p, axis=-1, keepdims=True)
        acc = alpha * acc + _dot(p.astype(BF16), vj)
        m = m_new
    o_ref[...] = (acc / l).astype(o_ref.dtype)


def _attn_call(q, k, v, n_batch, t_len, tpb):
    rows_b = tpb * TM
    assert rows_b % ATTN_TK == 0 and t_len % ATTN_TQ == 0 and CTX_LEN % ATTN_TQ == 0
    n_q = t_len // ATTN_TQ
    q_off = CTX_LEN // ATTN_TQ
    k3 = k.reshape(n_batch, rows_b, MLA_HEADS * QK_PAD_DIM)
    v3 = v.reshape(n_batch, rows_b, MLA_WIDTH)
    return pl.pallas_call(
        functools.partial(_attn_kernel, rows_b // ATTN_TK),
        grid=(n_batch, MLA_HEADS, n_q),
        in_specs=[
            pl.BlockSpec((ATTN_TQ, QK_PAD_DIM), lambda b, h, i: (b * (rows_b // ATTN_TQ) + q_off + i, h)),
            pl.BlockSpec((None, rows_b, QK_PAD_DIM), lambda b, h, i: (b, 0, h)),
            pl.BlockSpec((None, rows_b, V_HEAD_DIM), lambda b, h, i: (b, 0, h)),
        ],
        out_specs=pl.BlockSpec((ATTN_TQ, V_HEAD_DIM), lambda b, h, i: (b * n_q + i, h)),
        out_shape=jax.ShapeDtypeStruct((n_batch * t_len, MLA_WIDTH), BF16),
        compiler_params=_cparams(("arbitrary", "arbitrary", "arbitrary")),
        name="attention",
    )(q, k3, v3)


def _head_sum(x, ones_bd):
    hi, lo = _split2(x)
    cols = []
    for c in range(x.shape[1] // LANES):
        sl = slice(c * LANES, (c + 1) * LANES)
        cols.append(_dot(hi[:, sl], ones_bd) + _dot(lo[:, sl], ones_bd))
    return jnp.concatenate(cols, axis=1)


def _slot_rank(idx, run_ref):
    lane = lax.broadcasted_iota(jnp.int32, idx.shape, 1)
    oh0 = lane == idx[:, 0:1]
    oh1 = lane == idx[:, 1:2]
    both = jnp.where(oh0 | oh1, 1.0, 0.0)
    t_row = lax.broadcasted_iota(jnp.int32, (TM, TM), 0)
    t_col = lax.broadcasted_iota(jnp.int32, (TM, TM), 1)
    earlier = jnp.where(t_col < t_row, 1.0, 0.0).astype(BF16)
    seen = _dot(earlier, both.astype(BF16)) + run_ref[...]
    r0 = jnp.sum(jnp.where(oh0, seen, 0.0), axis=-1, keepdims=True)
    r1 = jnp.sum(jnp.where(oh1, seen, 0.0), axis=-1, keepdims=True)
    run_ref[...] = run_ref[...] + jnp.sum(both, axis=0, keepdims=True)
    return jnp.where(lane == 0, r0, jnp.where(lane == 1, r1, 0.0)).astype(jnp.int32)


def _route(logits):
    lane = lax.broadcasted_iota(jnp.int32, logits.shape, 1)
    neg = jnp.full_like(logits, -jnp.inf)
    big = jnp.full_like(lane, 2 ** 30)
    is_grp = lane < N_GROUPS
    gl = jnp.where(is_grp, logits, neg)
    ge = jnp.exp(gl - jnp.max(gl, axis=-1, keepdims=True))
    gp = ge / jnp.sum(ge, axis=-1, keepdims=True)
    g_val = jnp.max(gp, axis=-1, keepdims=True)
    g_idx = jnp.min(jnp.where(is_grp & (gp == g_val), lane, big), axis=-1, keepdims=True)
    e_lane = lane - N_GROUPS
    in_grp = (e_lane >= g_idx * EXPERTS_PER_GROUP) & (e_lane < (g_idx + 1) * EXPERTS_PER_GROUP)
    el = jnp.where(in_grp, logits, neg)
    ee = jnp.exp(el - jnp.max(el, axis=-1, keepdims=True))
    ep = ee / jnp.sum(ee, axis=-1, keepdims=True)
    v1 = jnp.max(ep, axis=-1, keepdims=True)
    i1 = jnp.min(jnp.where(in_grp & (ep == v1), lane, big), axis=-1, keepdims=True)
    rest = in_grp & (lane != i1)
    v2 = jnp.max(jnp.where(rest, ep, neg), axis=-1, keepdims=True)
    i2 = jnp.min(jnp.where(rest & (ep == v2), lane, big), axis=-1, keepdims=True)
    denom = v1 + v2
    idx = jnp.where(lane == 0, i1 - N_GROUPS, jnp.where(lane == 1, i2 - N_GROUPS, 0))
    gate = jnp.where(lane == 0, g_val * v1 / denom, jnp.where(lane == 1, g_val * v2 / denom, 0.0))
    return idx, gate


def _mix_kernel(x_ref, attn_ref, yf_ref, yb_ref, r_ref, k_ref, v_ref, af_ref, ab_ref, g_ref,
                g1_ref, sh2_ref, sc2_ref, akey_ref, rk_ref, lng_ref, lnb_ref, ng_ref,
                ones_ref, wo_ref, wr_ref, br_ref,
                x1_o, h2_o, idx_o, gate_o, rank_o, cnt_o, run_scr):
    @pl.when(pl.program_id(0) == 0)
    def _():
        run_scr[...] = jnp.zeros_like(run_scr)

    ones_bd = ones_ref[...]
    inv = 1.0 / RWKV_HEAD_DIM
    y = yf_ref[...] + yb_ref[...]
    mu = _head_sum(y, ones_bd) * inv
    dy = y - mu
    var = _head_sum(dy * dy, ones_bd) * inv
    yn = dy * lax.rsqrt(var + LNX_EPS) * lng_ref[...] + lnb_ref[...]
    k_sum = k_ref[...] * (2.0 + (af_ref[...] + ab_ref[...] - 2.0) * akey_ref[...])
    bonus = _head_sum(r_ref[...] * k_sum * rk_ref[...], ones_bd) * v_ref[...]
    rw = ((yn + bonus) * g_ref[...]).astype(BF16)
    o = _dot(attn_ref[...], wo_ref[0:MLA_WIDTH, :]) + _dot(rw, wo_ref[MLA_WIDTH:D_MODEL, :])
    x1 = x_ref[...] + g1_ref[...] * o
    x1_o[...] = x1
    h = x1 * lax.rsqrt(jnp.mean(x1 * x1, axis=-1, keepdims=True) + NORM_EPS) * ng_ref[...]
    h2 = h * (1.0 + sc2_ref[...]) + sh2_ref[...]
    h2_o[...] = h2
    h_hi, h_lo = _split2(h2)
    w_hi = wr_ref[0]
    logits = _dot(h_hi, w_hi) + _dot(h_lo, w_hi) + _dot(h_hi, wr_ref[1]) + br_ref[...]
    idx, gate = _route(logits)
    idx_o[...] = idx
    gate_o[...] = gate
    rank_o[...] = _slot_rank(idx, run_scr)
    cnt_o[...] = run_scr[...]


def _mix_call(x, attn, yscan, r, k, v, lr, g, mod_tab, key_a, bonus_rk, lnx_g, lnx_b, norm_g,
              ones_bd, w_out_b, w_router, b_router, tpb):
    n_batch, t_len, _ = x.shape
    tpl = t_len // TM
    n = n_batch * t_len

    def lat(i):
        return (i // tpl) * tpb + 1 + i % tpl

    def tok(cols, col_blk=0):
        return pl.BlockSpec((TM, cols), lambda i: (lat(i), col_blk))

    def mod_spec(kk):
        return pl.BlockSpec((None, 1, D_MODEL), lambda i: ((i // tpl) * 6 + kk, 0, 0))

    def vec(cols):
        return _resident((1, cols), lambda i: (0, 0))

    tile_out = pl.BlockSpec((TM, ROUTER_COLS), lambda i: (i, 0))
    return pl.pallas_call(
        _mix_kernel,
        grid=(n // TM,),
        in_specs=[
            pl.BlockSpec((None, TM, D_MODEL), lambda i: (i // tpl, i % tpl, 0)),
            pl.BlockSpec((TM, MLA_WIDTH), lambda i: (i, 0)),
            pl.BlockSpec((None, TM, RWKV_WIDTH), lambda i: (0, lat(i), 0)),
            pl.BlockSpec((None, TM, RWKV_WIDTH), lambda i: (1, lat(i), 0)),
            tok(RWKV_WIDTH), tok(RWKV_WIDTH), tok(RWKV_WIDTH),
            tok(RWKV_WIDTH, 0), tok(RWKV_WIDTH, 1), tok(RWKV_WIDTH),
            mod_spec(2), mod_spec(3), mod_spec(4),
            vec(RWKV_WIDTH), vec(RWKV_WIDTH), vec(RWKV_WIDTH), vec(RWKV_WIDTH), vec(D_MODEL),
            _resident((LANES, LANES), lambda i: (0, 0)),
            _resident((D_MODEL, D_MODEL), lambda i: (0, 0)),
            _resident((2, D_MODEL, ROUTER_COLS), lambda i: (0, 0, 0)),
            vec(ROUTER_COLS),
        ],
        out_specs=[
            pl.BlockSpec((TM, D_MODEL), lambda i: (i, 0)),
            pl.BlockSpec((TM, D_MODEL), lambda i: (i, 0)),
            tile_out, tile_out, tile_out,
            pl.BlockSpec((1, ROUTER_COLS), lambda i: (0, 0)),
        ],
        out_shape=[
            jax.ShapeDtypeStruct((n, D_MODEL), F32),
            jax.ShapeDtypeStruct((n, D_MODEL), F32),
            jax.ShapeDtypeStruct((n, ROUTER_COLS), jnp.int32),
            jax.ShapeDtypeStruct((n, ROUTER_COLS), F32),
            jax.ShapeDtypeStruct((n, ROUTER_COLS), jnp.int32),
            jax.ShapeDtypeStruct((1, ROUTER_COLS), F32),
        ],
        scratch_shapes=[pltpu.VMEM((1, ROUTER_COLS), F32)],
        compiler_params=_cparams(("arbitrary",)),
        name="mix",
    )(x, attn, yscan, yscan, r, k, v, lr, lr, g, mod_tab, mod_tab, mod_tab,
      key_a, bonus_rk, lnx_g, lnx_b, norm_g.reshape(1, D_MODEL), ones_bd, w_out_b, w_router, b_router)


def _scatter_kernel(dest_ref, h_ref, init_hbm, xs_hbm, sem):
    del init_hbm
    base = pl.program_id(0) * (TM * TOP_K)

    def row(t, slot):
        return pltpu.make_async_copy(h_ref.at[pl.ds(t, 1)], xs_hbm.at[pl.ds(slot, 1)], sem)

    def start(t, c):
        for kk in range(TOP_K):
            row(t, dest_ref[base + t * TOP_K + kk]).start()
        return c
    lax.fori_loop(0, TM, start, 0, unroll=DMA_UNROLL)

    def wait(t, c):
        for _ in range(TOP_K):
            row(t, 0).wait()
        return c
    lax.fori_loop(0, TM, wait, 0, unroll=DMA_UNROLL)


def _scatter_call(dest, h2, n_slots):
    n = h2.shape[0]
    return pl.pallas_call(
        _scatter_kernel,
        grid_spec=pltpu.PrefetchScalarGridSpec(
            num_scalar_prefetch=1,
            grid=(n // TM,),
            in_specs=[
                pl.BlockSpec((TM, D_MODEL), lambda i, dest: (i, 0)),
                pl.BlockSpec(memory_space=pl.ANY),
            ],
            out_specs=pl.BlockSpec(memory_space=pl.ANY),
            scratch_shapes=[pltpu.SemaphoreType.DMA],
        ),
        out_shape=jax.ShapeDtypeStruct((n_slots, D_MODEL), F32),
        input_output_aliases={2: 0},
        compiler_params=_cparams(("arbitrary",)),
        name="scatter",
    )(dest, h2, jnp.zeros((n_slots, D_MODEL), F32))


def _moe_kernel(be_ref, used_ref, x_ref, w1_ref, w3_ref, w2_ref, y_ref, w1b, w3b, w2b):
    i = pl.program_id(0)

    @pl.when(i < used_ref[0])
    def _():
        @pl.when((i == 0) | (be_ref[i] != be_ref[jnp.maximum(i - 1, 0)]))
        def _():
            w1b[...] = w1_ref[...].astype(BF16)
            w3b[...] = w3_ref[...].astype(BF16)
            w2b[...] = w2_ref[...].astype(BF16)

        x = x_ref[...].astype(BF16)
        a1 = _dot(x, w1b[...])
        a3 = _dot(x, w3b[...])
        hm = (a1 * jax.nn.sigmoid(a1) * a3).astype(BF16)
        y_ref[...] = _dot(hm, w2b[...])

    @pl.when(i >= used_ref[0])
    def _():
        y_ref[...] = jnp.zeros_like(y_ref)


def _moe_call(block_expert, n_used, xs, w1, w3, w2):
    n_blocks = block_expert.shape[0]

    def wspec(shape):
        return pl.BlockSpec((None,) + shape, lambda i, be, used: (be[i], 0, 0))

    return pl.pallas_call(
        _moe_kernel,
        grid_spec=pltpu.PrefetchScalarGridSpec(
            num_scalar_prefetch=2,
            grid=(n_blocks,),
            in_specs=[
                pl.BlockSpec((MOE_BLOCK, D_MODEL), lambda i, be, used: (jnp.minimum(i, used[0] - 1), 0)),
                wspec((D_MODEL, D_EXPERT)),
                wspec((D_MODEL, D_EXPERT)),
                wspec((D_EXPERT, D_MODEL)),
            ],
            out_specs=pl.BlockSpec((MOE_BLOCK, D_MODEL), lambda i, be, used: (i, 0)),
            scratch_shapes=[
                pltpu.VMEM((D_MODEL, D_EXPERT), BF16),
                pltpu.VMEM((D_MODEL, D_EXPERT), BF16),
                pltpu.VMEM((D_EXPERT, D_MODEL), BF16),
            ],
        ),
        out_shape=jax.ShapeDtypeStruct(xs.shape, F32),
        compiler_params=_cparams(("arbitrary",)),
        name="moe",
    )(block_expert, n_used, xs, w1, w3, w2)


def _final_kernel(dest_ref, x1_ref, gate_ref, g2_ref, ng_ref, ys_hbm, o_ref, ybuf, sem):
    base = pl.program_id(0) * (TM * TOP_K)

    def row(t, kk, slot):
        return pltpu.make_async_copy(ys_hbm.at[pl.ds(slot, 1)], ybuf.at[kk, pl.ds(t, 1)], sem)

    def start(t, c):
        for kk in range(TOP_K):
            row(t, kk, dest_ref[base + t * TOP_K + kk]).start()
        return c
    lax.fori_loop(0, TM, start, 0, unroll=DMA_UNROLL)

    def wait(t, c):
        for kk in range(TOP_K):
            row(t, kk, 0).wait()
        return c
    lax.fori_loop(0, TM, wait, 0, unroll=DMA_UNROLL)

    gate = gate_ref[...]
    y = ybuf[0] * gate[:, 0:1] + ybuf[1] * gate[:, 1:2]
    x2 = x1_ref[...] + g2_ref[...] * y
    o_ref[...] = x2 * lax.rsqrt(jnp.mean(x2 * x2, axis=-1, keepdims=True) + NORM_EPS) * ng_ref[...]


def _final_call(dest, x1, ys, gates, mod_tab, final_g, t_len):
    n = x1.shape[0]
    tpl = t_len // TM
    return pl.pallas_call(
        _final_kernel,
        grid_spec=pltpu.PrefetchScalarGridSpec(
            num_scalar_prefetch=1,
            grid=(n // TM,),
            in_specs=[
                pl.BlockSpec((TM, D_MODEL), lambda i, dest: (i, 0)),
                pl.BlockSpec((TM, ROUTER_COLS), lambda i, dest: (i, 0)),
                pl.BlockSpec((None, 1, D_MODEL), lambda i, dest: ((i // tpl) * 6 + 5, 0, 0)),
                _resident((1, D_MODEL), lambda i, dest: (0, 0)),
                pl.BlockSpec(memory_space=pl.ANY),
            ],
            out_specs=pl.BlockSpec((TM, D_MODEL), lambda i, dest: (i, 0)),
            scratch_shapes=[pltpu.VMEM((TOP_K, TM, D_MODEL), F32), pltpu.SemaphoreType.DMA],
        ),
        out_shape=jax.ShapeDtypeStruct((n, D_MODEL), F32),
        compiler_params=_cparams(("arbitrary",)),
        name="final",
    )(dest, x1, gates, mod_tab, final_g.reshape(1, D_MODEL), ys)


def _pad_cols(w, width):
    return jnp.pad(w, ((0, 0), (0, width - w.shape[1])))


_ROPE_SWAP = np.concatenate([np.arange(16, 32), np.arange(0, 16), np.arange(48, 64), np.arange(32, 48)])


def _rope_tables(t_len):
    pos = jnp.arange(t_len)
    inv_freq = ROPE_THETA ** (-jnp.arange(0, ROPE_AXIS_DIM, 2, dtype=F32) / ROPE_AXIS_DIM)
    ang_r = (pos // GRID_W)[:, None].astype(F32) * inv_freq
    ang_c = (pos % GRID_W)[:, None].astype(F32) * inv_freq
    cos = jnp.concatenate([jnp.cos(ang_r)] * 2 + [jnp.cos(ang_c)] * 2, axis=1)
    sin = jnp.concatenate([-jnp.sin(ang_r), jnp.sin(ang_r), -jnp.sin(ang_c), jnp.sin(ang_c)], axis=1)
    cos = jnp.concatenate([jnp.ones((CTX_LEN, QK_ROPE_DIM), F32), cos], axis=0)
    sin = jnp.concatenate([jnp.zeros((CTX_LEN, QK_ROPE_DIM), F32), sin], axis=0)
    rows = cos.shape[0]
    z64 = jnp.zeros((rows, QK_ROPE_DIM), F32)
    ck = jnp.concatenate([cos, z64], axis=1)
    sk = jnp.concatenate([sin, z64], axis=1)
    cq = MLA_SCALE * jnp.concatenate([jnp.ones((rows, QK_NOPE_DIM), F32), cos, z64], axis=1)
    sq = MLA_SCALE * jnp.concatenate([jnp.zeros((rows, QK_NOPE_DIM), F32), sin, z64], axis=1)
    return ck, sk, cq, sq


def _slot_tables(idx2, rank2, counts, n_tokens):
    n_blocks = (n_tokens * TOP_K + N_EXPERTS * (MOE_BLOCK - 1) + MOE_BLOCK - 1) // MOE_BLOCK
    padded = (counts + MOE_BLOCK - 1) // MOE_BLOCK * MOE_BLOCK
    pad_end = jnp.cumsum(padded)
    pad_start = pad_end - padded
    experts = jnp.arange(N_EXPERTS, dtype=jnp.int32)
    first = jnp.sum(jnp.where(idx2[..., None] == experts, pad_start, 0), axis=-1)
    dest = (first + rank2).reshape(-1).astype(jnp.int32)
    block_start = jnp.arange(n_blocks, dtype=jnp.int32) * MOE_BLOCK
    block_expert = jnp.minimum(jnp.sum(block_start[:, None] >= pad_end[None, :], axis=1), N_EXPERTS - 1)
    n_used = (pad_end[-1] // MOE_BLOCK).reshape(1)
    return dest, block_expert.astype(jnp.int32), n_used.astype(jnp.int32), n_blocks * MOE_BLOCK


def kernel(x, c, ctx, c_ctx, w_mod, b_mod, norm_attn_g, norm_ffn_g, w_in, shift_mu, q_norm_g, w_uq, kv_norm_g, w_ukv, decay_w0, decay_up, iclr_a0, iclr_up, gate_up, key_k, key_a, bonus_r_k, lnx_g, lnx_b, w_out, w_grp, b_grp, w_exp, b_exp, w1, w3, w2, final_norm_g):
    n_batch, t_len, _ = x.shape
    assert ctx.shape[1] == CTX_LEN == TM and t_len % TM == 0 and w_mod.shape[0] == 1
    tpb = (CTX_LEN + t_len) // TM
    cpb = (CTX_LEN + t_len) // CHUNK
    n = n_batch * t_len

    c_rows = jnp.zeros((8, D_MODEL), F32).at[:n_batch].set(c).at[n_batch].set(c_ctx)
    mod_tab = _mod_call(c_rows, w_mod[0], b_mod[0]).reshape(8 * 6, 1, D_MODEL)

    wi = w_in[0]
    w_kr = wi[:, 768:MLA_IN]
    o = MLA_IN
    w_in_p = jnp.concatenate([
        wi[:, 0:768], _pad_cols(w_kr, LANES), _pad_cols(w_kr[:, _ROPE_SWAP], LANES),
        wi[:, o:o + COLS_RKV],
        _pad_cols(wi[:, o + COLS_RKV:o + COLS_RKV + DECAY_LORA], LANES),
        _pad_cols(wi[:, o + COLS_RKV + DECAY_LORA:o + COLS_RKV + DECAY_LORA + ICLR_LORA], LANES),
        _pad_cols(wi[:, o + COLS_RKV + DECAY_LORA + ICLR_LORA:], 2 * LANES),
    ], axis=1).astype(BF16)
    p_mla, p_rkv, p_lora = _project_call(x, ctx, mod_tab, norm_attn_g[0], w_in_p, tpb)

    mu = shift_mu[0]
    mu_rkv = mu[:, 0:COLS_RKV]
    mu_lora = jnp.concatenate([
        _pad_cols(mu[:, COLS_RKV:COLS_RKV + DECAY_LORA], LANES),
        _pad_cols(mu[:, COLS_RKV + DECAY_LORA:COLS_RKV + DECAY_LORA + ICLR_LORA], LANES),
        _pad_cols(mu[:, COLS_RKV + DECAY_LORA + ICLR_LORA:], 2 * LANES)], axis=1)

    def lora_up(w):
        both = jnp.concatenate([w[0], w[1]], axis=1)
        return jnp.pad(both, ((0, LANES - both.shape[0]), (0, 0))).astype(BF16)

    gup = jnp.pad(gate_up[0], ((0, 2 * LANES - GATE_LORA), (0, 0))).astype(BF16)
    r, k, v, lw, lr, g = _prep_call(
        p_rkv, p_lora, mu_rkv, mu_lora, lora_up(decay_up[0]), lora_up(iclr_up[0]), gup,
        decay_w0[0].reshape(1, -1), iclr_a0[0].reshape(1, -1), tpb)
    key_k2 = key_k[0].reshape(1, -1)
    key_a2 = key_a[0].reshape(1, -1)
    yscan = _scan_call(r, k, v, lw, lr, key_k2, key_a2, n_batch, cpb, CTX_LEN // CHUNK)

    hd = QK_NOPE_DIM + QK_ROPE_DIM
    wq = w_uq[0].reshape(Q_LORA_RANK, MLA_HEADS, hd)
    zq = jnp.zeros((Q_LORA_RANK, MLA_HEADS, QK_ROPE_DIM), F32)
    wa = jnp.concatenate([wq, zq], axis=2).reshape(Q_LORA_RANK, -1).astype(BF16)
    wb = jnp.concatenate([jnp.zeros((Q_LORA_RANK, MLA_HEADS, QK_NOPE_DIM), F32),
                          wq[:, :, QK_NOPE_DIM:][:, :, _ROPE_SWAP], zq], axis=2
                         ).reshape(Q_LORA_RANK, -1).astype(BF16)
    wkv3 = w_ukv[0].reshape(KV_LORA_RANK, MLA_HEADS, QK_NOPE_DIM + V_HEAD_DIM)
    wkv = jnp.concatenate([wkv3[:, :, :QK_NOPE_DIM].reshape(KV_LORA_RANK, -1),
                           wkv3[:, :, QK_NOPE_DIM:].reshape(KV_LORA_RANK, -1)], axis=1).astype(BF16)
    q, kmat, vmat = _mla_prep_call(p_mla, _rope_tables(t_len), q_norm_g[0], kv_norm_g[0], wa, wb, wkv, tpb)
    attn = _attn_call(q, kmat, vmat, n_batch, t_len, tpb)

    lane = np.arange(LANES)
    ones_bd = jnp.asarray((lane[:, None] // RWKV_HEAD_DIM) == (lane[None, :] // RWKV_HEAD_DIM), BF16)
    w_router = _pad_cols(jnp.concatenate([w_grp[0], w_exp[0]], axis=1), ROUTER_COLS)
    w_router_hi = w_router.astype(BF16)
    w_router2 = jnp.stack([w_router_hi, (w_router - w_router_hi.astype(F32)).astype(BF16)])
    b_router = _pad_cols(jnp.concatenate([b_grp[0], b_exp[0]]).reshape(1, -1), ROUTER_COLS)
    x1, h2, idx, gates, rank, counts = _mix_call(
        x, attn, yscan, r, k, v, lr, g, mod_tab, key_a2, bonus_r_k[0].reshape(1, -1),
        lnx_g[0].reshape(1, -1), lnx_b[0].reshape(1, -1), norm_ffn_g[0], ones_bd,
        w_out[0].astype(BF16), w_router2, b_router, tpb)

    dest, block_expert, n_used, n_slots = _slot_tables(
        idx[:, :TOP_K], rank[:, :TOP_K], counts[0, :N_EXPERTS].astype(jnp.int32), n)
    xs = _scatter_call(dest, h2, n_slots)
    ys = _moe_call(block_expert, n_used, xs, w1[0], w3[0], w2[0])
    out = _final_call(dest, x1, ys, gates, mod_tab, final_norm_g, t_len)
    return out.reshape(n_batch, t_len, D_MODEL)
```

```python
import functools
import math

import jax
import jax.numpy as jnp
import numpy as np
from jax import lax
from jax.experimental import pallas as pl
from jax.experimental.pallas import tpu as pltpu

F32 = jnp.float32
BF16 = jnp.bfloat16
HIGHEST = lax.Precision.HIGHEST

D_MODEL = 2048
CTX_LEN = 256
GRID_W = 64
NORM_EPS = 1e-6

MLA_HEADS = 8
QK_NOPE_DIM = 128
QK_ROPE_DIM = 64
V_HEAD_DIM = 128
Q_LORA_RANK = 512
KV_LORA_RANK = 256
MLA_WIDTH = MLA_HEADS * V_HEAD_DIM
MLA_SCALE = (QK_NOPE_DIM + QK_ROPE_DIM) ** -0.5
ROPE_THETA = 10000.0
ROPE_AXIS_DIM = QK_ROPE_DIM // 2
QK_PAD_DIM = 256

RWKV_HEAD_DIM = 64
RWKV_WIDTH = D_MODEL - MLA_WIDTH
RWKV_HEADS = RWKV_WIDTH // RWKV_HEAD_DIM
DECAY_LORA = 64
ICLR_LORA = 64
GATE_LORA = 160
LNX_EPS = 64e-5

N_GROUPS = 4
EXPERTS_PER_GROUP = 8
N_EXPERTS = N_GROUPS * EXPERTS_PER_GROUP
TOP_K = 2
D_EXPERT = 512
MOE_BLOCK = 128

MLA_IN = Q_LORA_RANK + KV_LORA_RANK + QK_ROPE_DIM
LANES = 128
TM = 256
CHUNK = 64
PAIR = 2 * RWKV_HEAD_DIM
N_PAIRS = RWKV_WIDTH // PAIR
PAIRS_PER_STEP = 8
ATTN_TQ = 512
ATTN_TK = 768
DMA_UNROLL = 8
VMEM_LIMIT = 56 * 1024 * 1024

COLS_MLA = 1024
COLS_RKV = 3 * RWKV_WIDTH
COLS_LORA = 512
COLS_IN = COLS_MLA + COLS_RKV + COLS_LORA
ROUTER_COLS = 128


def _cparams(sem):
    return pltpu.CompilerParams(dimension_semantics=sem, vmem_limit_bytes=VMEM_LIMIT)


def _resident(shape, index_map):
    return pl.BlockSpec(shape, index_map, pipeline_mode=pl.Buffered(1))


def _dot(a, b):
    return jnp.dot(a, b, preferred_element_type=F32)


def _dot_nt(a, b):
    return lax.dot_general(a, b, (((1,), (1,)), ((), ())), preferred_element_type=F32)


def _dot_tn(a, b):
    return lax.dot_general(a, b, (((0,), (0,)), ((), ())), preferred_element_type=F32)


def _split2(x):
    hi = x.astype(BF16)
    lo = (x - hi.astype(F32)).astype(BF16)
    return hi, lo


def _split3(x):
    hi = x.astype(BF16)
    r1 = x - hi.astype(F32)
    mid = r1.astype(BF16)
    lo = (r1 - mid.astype(F32)).astype(BF16)
    return hi, mid, lo


def _mod_kernel(c_ref, w_ref, b_ref, o_ref):
    c = c_ref[...]
    s = c * jax.nn.sigmoid(c)
    o_ref[...] = jnp.dot(s, w_ref[...], preferred_element_type=F32, precision=HIGHEST) + b_ref[...]


def _mod_call(c_rows, w_mod, b_mod):
    n = w_mod.shape[1]
    tn = 1024
    return pl.pallas_call(
        _mod_kernel,
        grid=(n // tn,),
        in_specs=[
            pl.BlockSpec((8, D_MODEL), lambda i: (0, 0)),
            pl.BlockSpec((D_MODEL, tn), lambda i: (0, i)),
            pl.BlockSpec((1, tn), lambda i: (0, i)),
        ],
        out_specs=pl.BlockSpec((8, tn), lambda i: (0, i)),
        out_shape=jax.ShapeDtypeStruct((8, n), F32),
        compiler_params=_cparams(("arbitrary",)),
        name="mod",
    )(c_rows, w_mod, b_mod.reshape(1, n))


def _project_kernel(tpb, x_ref, ctx_ref, sh_ref, sc_ref, g_ref, w_ref, o_mla, o_rkv, o_lora):
    is_ctx = (pl.program_id(0) % tpb) == 0
    xin = jnp.where(is_ctx, ctx_ref[...], x_ref[...])
    ms = jnp.mean(xin * xin, axis=-1, keepdims=True)
    h = xin * lax.rsqrt(ms + NORM_EPS) * g_ref[...]
    hb = (h * (1.0 + sc_ref[...]) + sh_ref[...]).astype(BF16)
    o_mla[...] = _dot(hb, w_ref[:, 0:COLS_MLA])
    o_rkv[...] = _dot(hb, w_ref[:, COLS_MLA:COLS_MLA + COLS_RKV])
    o_lora[...] = _dot(hb, w_ref[:, COLS_MLA + COLS_RKV:COLS_IN])


def _mod_row(i, tpb, n_batch):
    return jnp.where(i % tpb == 0, n_batch, i // tpb)


def _project_call(x, ctx, mod_tab, norm_g, w_in_p, tpb):
    n_batch, t_len, _ = x.shape
    nt = n_batch * tpb * TM

    def mod_spec(k):
        return pl.BlockSpec((None, 1, D_MODEL), lambda i: (_mod_row(i, tpb, n_batch) * 6 + k, 0, 0))

    return pl.pallas_call(
        functools.partial(_project_kernel, tpb),
        grid=(n_batch * tpb,),
        in_specs=[
            pl.BlockSpec((None, TM, D_MODEL), lambda i: (i // tpb, jnp.maximum(i % tpb - 1, 0), 0)),
            pl.BlockSpec((None, TM, D_MODEL), lambda i: (i // tpb, 0, 0)),
            mod_spec(0),
            mod_spec(1),
            _resident((1, D_MODEL), lambda i: (0, 0)),
            _resident((D_MODEL, COLS_IN), lambda i: (0, 0)),
        ],
        out_specs=[
            pl.BlockSpec((TM, COLS_MLA), lambda i: (i, 0)),
            pl.BlockSpec((TM, COLS_RKV), lambda i: (i, 0)),
            pl.BlockSpec((TM, COLS_LORA), lambda i: (i, 0)),
        ],
        out_shape=[
            jax.ShapeDtypeStruct((nt, COLS_MLA), F32),
            jax.ShapeDtypeStruct((nt, COLS_RKV), F32),
            jax.ShapeDtypeStruct((nt, COLS_LORA), F32),
        ],
        compiler_params=_cparams(("arbitrary",)),
        name="project",
    )(x, ctx, mod_tab, mod_tab, norm_g.reshape(1, D_MODEL), w_in_p)


def _prep_kernel(tpb, p_ref, pp_ref, pn_ref, l_ref, lp_ref, ln_ref, mu_ref, mul_ref,
                 wup_ref, aup_ref, gup_ref, w0_ref, a0_ref,
                 r_o, k_o, v_o, lw_o, a_o, g_o):
    j = pl.program_id(0) % tpb
    no_prev = j <= 1
    no_next = (j == 0) | (j == tpb - 1)

    def shifted(main, prev_blk, next_blk, mu):
        rows = lax.broadcasted_iota(jnp.int32, main.shape, 0)
        prow = jnp.where(no_prev, 0.0, prev_blk[7:8, :])
        nrow = jnp.where(no_next, 0.0, next_blk[0:1, :])
        prev = jnp.where(rows == 0, prow, pltpu.roll(main, 1, 0))
        nxt = jnp.where(rows == TM - 1, nrow, pltpu.roll(main, TM - 1, 0))
        return main + mu[0:1, :] * (prev - main) + mu[1:2, :] * (nxt - main)

    for c, out in enumerate((r_o, k_o, v_o)):
        sl = slice(c * RWKV_WIDTH, (c + 1) * RWKV_WIDTH)
        out[...] = shifted(p_ref[:, sl], pp_ref[:, sl], pn_ref[:, sl], mu_ref[:, sl])

    lo = shifted(l_ref[...], lp_ref[...], ln_ref[...], mul_ref[...])
    wl = jnp.tanh(lo[:, 0:LANES]).astype(BF16)
    al = lo[:, LANES:2 * LANES].astype(BF16)
    gl = jax.nn.sigmoid(lo[:, 2 * LANES:4 * LANES]).astype(BF16)
    w_raw = w0_ref[...] + _dot(wl, wup_ref[...])
    lw_o[...] = -math.exp(-0.5) * jax.nn.sigmoid(w_raw)
    a_o[...] = jax.nn.sigmoid(a0_ref[...] + _dot(al, aup_ref[...]))
    g_o[...] = _dot(gl, gup_ref[...])


def _prep_call(p_rkv, p_lora, mu_rkv, mu_lora, wup, aup, gup, w0, a0, tpb):
    nt = p_rkv.shape[0]
    last8 = nt // 8 - 1
    sub = TM // 8

    def halo(cols):
        return [
            pl.BlockSpec((TM, cols), lambda i: (i, 0)),
            pl.BlockSpec((8, cols), lambda i: (jnp.maximum(i * sub - 1, 0), 0)),
            pl.BlockSpec((8, cols), lambda i: (jnp.minimum((i + 1) * sub, last8), 0)),
        ]

    w2 = 2 * RWKV_WIDTH
    return pl.pallas_call(
        functools.partial(_prep_kernel, tpb),
        grid=(nt // TM,),
        in_specs=halo(COLS_RKV) + halo(COLS_LORA) + [
            _resident((2, COLS_RKV), lambda i: (0, 0)),
            _resident((2, COLS_LORA), lambda i: (0, 0)),
            _resident((LANES, w2), lambda i: (0, 0)),
            _resident((LANES, w2), lambda i: (0, 0)),
            _resident((2 * LANES, RWKV_WIDTH), lambda i: (0, 0)),
            _resident((1, w2), lambda i: (0, 0)),
            _resident((1, w2), lambda i: (0, 0)),
        ],
        out_specs=[
            pl.BlockSpec((TM, RWKV_WIDTH), lambda i: (i, 0)),
            pl.BlockSpec((TM, RWKV_WIDTH), lambda i: (i, 0)),
            pl.BlockSpec((TM, RWKV_WIDTH), lambda i: (i, 0)),
            pl.BlockSpec((TM, w2), lambda i: (i, 0)),
            pl.BlockSpec((TM, w2), lambda i: (i, 0)),
            pl.BlockSpec((TM, RWKV_WIDTH), lambda i: (i, 0)),
        ],
        out_shape=[
            jax.ShapeDtypeStruct((nt, RWKV_WIDTH), F32),
            jax.ShapeDtypeStruct((nt, RWKV_WIDTH), F32),
            jax.ShapeDtypeStruct((nt, RWKV_WIDTH), F32),
            jax.ShapeDtypeStruct((nt, w2), F32),
            jax.ShapeDtypeStruct((nt, w2), F32),
            jax.ShapeDtypeStruct((nt, RWKV_WIDTH), F32),
        ],
        compiler_params=_cparams(("arbitrary",)),
        name="prep",
    )(p_rkv, p_rkv, p_rkv, p_lora, p_lora, p_lora, mu_rkv, mu_lora, wup, aup, gup, w0, a0)


def _stack_heads(x):
    lane = lax.broadcasted_iota(jnp.int32, x.shape, 1)
    zero = jnp.zeros_like(x)
    return jnp.concatenate([jnp.where(lane < RWKV_HEAD_DIM, x, zero),
                            jnp.where(lane >= RWKV_HEAD_DIM, x, zero)], axis=0)


def _unstack_heads(z):
    half = z.shape[0] // 2
    return z[:half] + z[half:]


def _scan_kernel(r_ref, k_ref, v_ref, lw_ref, a_ref, kkey_ref, akey_ref, y_ref, s_scr):
    rev = pl.program_id(0) == 1

    @pl.when(pl.program_id(3) == 0)
    def _():
        s_scr[...] = jnp.zeros_like(s_scr)

    c2 = 2 * CHUNK
    t64 = lax.broadcasted_iota(jnp.int32, (CHUNK, CHUNK), 0)
    i64 = lax.broadcasted_iota(jnp.int32, (CHUNK, CHUNK), 1)
    cum_mat = jnp.where(jnp.where(rev, t64 - i64, i64 - t64) <= 0, 1.0, 0.0).astype(BF16)

    row = lax.broadcasted_iota(jnp.int32, (c2, LANES), 0)
    col = lax.broadcasted_iota(jnp.int32, (c2, LANES), 1)
    t_idx = row % CHUNK
    i_idx = col % CHUNK
    order = jnp.where(rev, t_idx - i_idx, i_idx - t_idx)
    keep = (order < 0) | ((order == 0) & (row >= CHUNK))
    same_head = (row // RWKV_HEAD_DIM) == (col // RWKV_HEAD_DIM)
    eye = row == col
    ones_bd = jnp.where(same_head, 1.0, 0.0).astype(BF16)
    eye_f = jnp.where(eye, 1.0, 0.0)

    zero = jnp.zeros((c2, LANES), F32)
    pairs = range(PAIRS_PER_STEP)

    def pair(x, p):
        return x[:, p * PAIR:(p + 1) * PAIR]

    r = r_ref[...]
    k = k_ref[...]
    lw = lw_ref[...]
    lr = a_ref[...]
    vb = v_ref[...].astype(BF16)
    kraw = k * kkey_ref[...]
    sq_hi, sq_lo = _split2(kraw * kraw)
    w_hi, w_mid, w_lo = _split3(lw)
    lp = _dot(cum_mat, w_hi) + _dot(cum_mat, w_mid) + _dot(cum_mat, w_lo)
    ssq = jnp.concatenate([_dot(pair(sq_hi, p), ones_bd) + _dot(pair(sq_lo, p), ones_bd) for p in pairs], axis=1)
    kk = kraw * lax.rsqrt(ssq + 1e-12)
    b = kk * lr
    kd = k * (1.0 + (lr - 1.0) * akey_ref[...])
    ltot = jnp.where(rev, lp[0:1, :], lp[CHUNK - 1:CHUNK, :])
    e_neg = jnp.exp(-lp)
    e_rest = jnp.exp(ltot - lp)
    e_tot = jnp.exp(ltot)
    at = -kk * jnp.exp(lp - lw)
    rt = r * jnp.exp(lp)
    at_b = at.astype(BF16)
    rt_b = rt.astype(BF16)
    bt_b = (b * e_neg).astype(BF16)
    kt_b = (kd * e_neg).astype(BF16)
    bh = (b * e_rest).astype(BF16)
    kh = (kd * e_rest).astype(BF16)

    sv = [_stack_heads(pair(vb, p)) for p in pairs]
    ar = [jnp.concatenate([pair(at_b, p), pair(rt_b, p)], axis=0) for p in pairs]
    ab = [jnp.where(keep, _dot_nt(ar[p], _stack_heads(pair(bt_b, p))), zero) for p in pairs]
    ak = [jnp.where(keep, _dot_nt(ar[p], _stack_heads(pair(kt_b, p))), zero) for p in pairs]
    a_rb = [ab[p][CHUNK:].astype(BF16) for p in pairs]
    a_ak = [ak[p][:CHUNK].astype(BF16) for p in pairs]
    a_rk = [ak[p][CHUNK:].astype(BF16) for p in pairs]
    x1 = [_dot(a_ak[p], sv[p]) for p in pairs]

    pw = [_stack_heads(ab[p][:CHUNK]) for p in pairs]
    tm = [eye_f + pw[p] for p in pairs]
    pw = [_dot(pw[p].astype(BF16), pw[p].astype(BF16)) for p in pairs]
    for _ in range(int(math.log2(CHUNK)) - 2):
        both = [_dot(jnp.concatenate([tm[p], pw[p]], axis=0).astype(BF16), pw[p].astype(BF16)) for p in pairs]
        tm = [tm[p] + both[p][:c2] for p in pairs]
        pw = [both[p][c2:] for p in pairs]
    tm = [tm[p] + _dot(tm[p].astype(BF16), pw[p].astype(BF16)) for p in pairs]
    t_p = [_unstack_heads(tm[p]).astype(BF16) for p in pairs]

    wg = [_dot(t_p[p], jnp.concatenate([_stack_heads(x1[p].astype(BF16)), _stack_heads(pair(at_b, p))], axis=1))
          for p in pairs]
    w_b = [wg[p][:, :LANES].astype(BF16) for p in pairs]
    g_b = [wg[p][:, LANES:].astype(BF16) for p in pairs]
    qz = [_dot(a_rb[p], jnp.concatenate([_stack_heads(g_b[p]), _stack_heads(w_b[p])], axis=1)) for p in pairs]
    z_rk = [_dot(a_rk[p], sv[p]) for p in pairs]
    m_bd = [jnp.where(same_head, _dot_tn(g_b[p], pair(bh, p)), zero).astype(BF16) for p in pairs]
    n_st = [jnp.where(same_head,
                      _dot_tn(jnp.concatenate([w_b[p], pair(vb, p)], axis=0),
                              jnp.concatenate([pair(bh, p), pair(kh, p)], axis=0)), zero) for p in pairs]

    s_old = [s_scr[p] for p in pairs]
    s_b = [s_old[p].astype(BF16) for p in pairs]
    for p in pairs:
        q = (pair(rt, p) + qz[p][:, :LANES]).astype(BF16)
        y_ref[:, p * PAIR:(p + 1) * PAIR] = _dot_nt(q, _stack_heads(s_b[p])) + qz[p][:, LANES:] + z_rk[p]
    for p in pairs:
        s_scr[p] = s_old[p] * pair(e_tot, p) + _dot(s_b[p], m_bd[p]) + _unstack_heads(n_st[p])


def _scan_call(r, k, v, lw, lr, key_k, key_a, n_batch, cpb, ctx_chunks):
    nt = r.shape[0]
    groups = N_PAIRS // PAIRS_PER_STEP
    gw = PAIRS_PER_STEP * PAIR

    def chunk_row(d, b, j):
        back = jnp.where(j < ctx_chunks, ctx_chunks - 1 - j, cpb + ctx_chunks - 1 - j)
        return b * cpb + jnp.where(d == 0, j, back)

    shared = pl.BlockSpec((CHUNK, gw), lambda d, b, g, j: (chunk_row(d, b, j), g))
    per_dir = pl.BlockSpec((CHUNK, gw), lambda d, b, g, j: (chunk_row(d, b, j), d * groups + g))
    keys = pl.BlockSpec((1, gw), lambda d, b, g, j: (0, g))
    return pl.pallas_call(
        _scan_kernel,
        grid=(2, n_batch, groups, cpb),
        in_specs=[shared, shared, shared, per_dir, per_dir, keys, keys],
        out_specs=pl.BlockSpec((None, CHUNK, gw), lambda d, b, g, j: (d, chunk_row(d, b, j), g)),
        out_shape=jax.ShapeDtypeStruct((2, nt, RWKV_WIDTH), F32),
        scratch_shapes=[pltpu.VMEM((PAIRS_PER_STEP, RWKV_HEAD_DIM, PAIR), F32)],
        compiler_params=_cparams(("arbitrary", "arbitrary", "arbitrary", "arbitrary")),
        name="scan",
    )(r, k, v, lw, lr, key_k, key_a)


def _mla_prep_kernel(p_ref, ck_ref, sk_ref, cq_ref, sq_ref, qg_ref, kvg_ref, wa_ref, wb_ref, wkv_ref,
                     q_o, k_o, v_o):
    cq = p_ref[:, 0:Q_LORA_RANK]
    cqn = (cq * lax.rsqrt(jnp.mean(cq * cq, axis=-1, keepdims=True) + NORM_EPS) * qg_ref[...]).astype(BF16)
    ckv = p_ref[:, Q_LORA_RANK:Q_LORA_RANK + KV_LORA_RANK]
    ckvn = (ckv * lax.rsqrt(jnp.mean(ckv * ckv, axis=-1, keepdims=True) + NORM_EPS) * kvg_ref[...]).astype(BF16)
    kr_a = p_ref[:, 768:896]
    kr_b = p_ref[:, 896:1024]
    k_rot = (kr_a * ck_ref[...] + kr_b * sk_ref[...]).astype(BF16)
    cos_q = cq_ref[...]
    sin_q = sq_ref[...]
    for h in range(MLA_HEADS):
        hs = slice(h * QK_PAD_DIM, (h + 1) * QK_PAD_DIM)
        q_o[:, hs] = (_dot(cqn, wa_ref[:, hs]) * cos_q + _dot(cqn, wb_ref[:, hs]) * sin_q).astype(BF16)
        k_o[:, h * QK_PAD_DIM:h * QK_PAD_DIM + QK_NOPE_DIM] = _dot(
            ckvn, wkv_ref[:, h * QK_NOPE_DIM:(h + 1) * QK_NOPE_DIM]).astype(BF16)
        k_o[:, h * QK_PAD_DIM + QK_NOPE_DIM:(h + 1) * QK_PAD_DIM] = k_rot
    v_o[...] = _dot(ckvn, wkv_ref[:, MLA_WIDTH:2 * MLA_WIDTH]).astype(BF16)


def _mla_prep_call(p_mla, tabs, q_norm_g, kv_norm_g, wa, wb, wkv, tpb):
    nt = p_mla.shape[0]
    ck, sk, cq, sq = tabs
    qw = MLA_HEADS * QK_PAD_DIM
    return pl.pallas_call(
        _mla_prep_kernel,
        grid=(nt // TM,),
        in_specs=[
            pl.BlockSpec((TM, COLS_MLA), lambda i: (i, 0)),
            pl.BlockSpec((TM, LANES), lambda i: (i % tpb, 0)),
            pl.BlockSpec((TM, LANES), lambda i: (i % tpb, 0)),
            pl.BlockSpec((TM, QK_PAD_DIM), lambda i: (i % tpb, 0)),
            pl.BlockSpec((TM, QK_PAD_DIM), lambda i: (i % tpb, 0)),
            _resident((1, Q_LORA_RANK), lambda i: (0, 0)),
            _resident((1, KV_LORA_RANK), lambda i: (0, 0)),
            _resident((Q_LORA_RANK, qw), lambda i: (0, 0)),
            _resident((Q_LORA_RANK, qw), lambda i: (0, 0)),
            _resident((KV_LORA_RANK, 2 * MLA_WIDTH), lambda i: (0, 0)),
        ],
        out_specs=[
            pl.BlockSpec((TM, qw), lambda i: (i, 0)),
            pl.BlockSpec((TM, qw), lambda i: (i, 0)),
            pl.BlockSpec((TM, MLA_WIDTH), lambda i: (i, 0)),
        ],
        out_shape=[
            jax.ShapeDtypeStruct((nt, qw), BF16),
            jax.ShapeDtypeStruct((nt, qw), BF16),
            jax.ShapeDtypeStruct((nt, MLA_WIDTH), BF16),
        ],
        compiler_params=_cparams(("arbitrary",)),
        name="mla_prep",
    )(p_mla, ck, sk, cq, sq, q_norm_g.reshape(1, -1), kv_norm_g.reshape(1, -1), wa, wb, wkv)


def _attn_kernel(n_kv, *refs):
    q_refs, (k_ref, v_ref, o_ref) = refs[:-3], refs[-3:]
    q = jnp.concatenate([q_ref[...] for q_ref in q_refs], axis=0)

    m = jnp.full((1, ATTN_TQ), -jnp.inf, F32)
    l = jnp.zeros((1, ATTN_TQ), F32)
    acc = jnp.zeros((V_HEAD_DIM, ATTN_TQ), F32)
    def scores(j):
        return _dot_nt(k_ref[j * ATTN_TK:(j + 1) * ATTN_TK, :], q)

    s_next = scores(0)
    for j in range(n_kv):
        vj = v_ref[j * ATTN_TK:(j + 1) * ATTN_TK, :]
        s = s_next
        if j + 1 < n_kv:
            s_next = scores(j + 1)
        m_new = jnp.maximum(m, jnp.max(s, axis=0, keepdims=True))
        alpha = jnp.exp2(m - m_new)
        p = jnp.exp2(s - m_new)
        l = alpha * l + jnp.sum(p, axis=0, keepdims=True)
        acc = alpha * acc + _dot_tn(vj, p.astype(BF16))
        m = m_new
    o_ref[...] = jnp.transpose(acc / l).astype(o_ref.dtype)


def _attn_call(q, k, v, n_batch, t_len, tpb):
    rows_b = tpb * TM
    assert rows_b % ATTN_TK == 0 and t_len % ATTN_TQ == 0 and ATTN_TQ % TM == 0
    n_q = t_len // ATTN_TQ
    sub = ATTN_TQ // TM
    k3 = k.reshape(n_batch, rows_b, MLA_HEADS * QK_PAD_DIM)
    v3 = v.reshape(n_batch, rows_b, MLA_WIDTH)

    def q_spec(u):
        return pl.BlockSpec((TM, QK_PAD_DIM), lambda b, h, i: (b * tpb + 1 + i * sub + u, h))

    return pl.pallas_call(
        functools.partial(_attn_kernel, rows_b // ATTN_TK),
        grid=(n_batch, MLA_HEADS, n_q),
        in_specs=[q_spec(u) for u in range(sub)] + [
            pl.BlockSpec((None, rows_b, QK_PAD_DIM), lambda b, h, i: (b, 0, h)),
            pl.BlockSpec((None, rows_b, V_HEAD_DIM), lambda b, h, i: (b, 0, h)),
        ],
        out_specs=pl.BlockSpec((ATTN_TQ, V_HEAD_DIM), lambda b, h, i: (b * n_q + i, h)),
        out_shape=jax.ShapeDtypeStruct((n_batch * t_len, MLA_WIDTH), BF16),
        compiler_params=_cparams(("arbitrary", "arbitrary", "arbitrary")),
        name="attention",
    )(*([q] * sub), k3, v3)


def _head_sum(x, ones_bd):
    hi, lo = _split2(x)
    cols = []
    for c in range(x.shape[1] // LANES):
        sl = slice(c * LANES, (c + 1) * LANES)
        cols.append(_dot(hi[:, sl], ones_bd) + _dot(lo[:, sl], ones_bd))
    return jnp.concatenate(cols, axis=1)


def _slot_rank(idx, run_ref):
    lane = lax.broadcasted_iota(jnp.int32, idx.shape, 1)
    oh0 = lane == idx[:, 0:1]
    oh1 = lane == idx[:, 1:2]
    both = jnp.where(oh0 | oh1, 1.0, 0.0)
    t_row = lax.broadcasted_iota(jnp.int32, (TM, TM), 0)
    t_col = lax.broadcasted_iota(jnp.int32, (TM, TM), 1)
    earlier = jnp.where(t_col < t_row, 1.0, 0.0).astype(BF16)
    seen = _dot(earlier, both.astype(BF16)) + run_ref[...]
    r0 = jnp.sum(jnp.where(oh0, seen, 0.0), axis=-1, keepdims=True)
    r1 = jnp.sum(jnp.where(oh1, seen, 0.0), axis=-1, keepdims=True)
    run_ref[...] = run_ref[...] + jnp.sum(both, axis=0, keepdims=True)
    return jnp.where(lane == 0, r0, jnp.where(lane == 1, r1, 0.0)).astype(jnp.int32)


def _route(logits):
    lane = lax.broadcasted_iota(jnp.int32, logits.shape, 1)
    neg = jnp.full_like(logits, -jnp.inf)
    big = jnp.full_like(lane, 2 ** 30)
    is_grp = lane < N_GROUPS
    gl = jnp.where(is_grp, logits, neg)
    ge = jnp.exp(gl - jnp.max(gl, axis=-1, keepdims=True))
    gp = ge / jnp.sum(ge, axis=-1, keepdims=True)
    g_val = jnp.max(gp, axis=-1, keepdims=True)
    g_idx = jnp.min(jnp.where(is_grp & (gp == g_val), lane, big), axis=-1, keepdims=True)
    e_lane = lane - N_GROUPS
    in_grp = (e_lane >= g_idx * EXPERTS_PER_GROUP) & (e_lane < (g_idx + 1) * EXPERTS_PER_GROUP)
    el = jnp.where(in_grp, logits, neg)
    ee = jnp.exp(el - jnp.max(el, axis=-1, keepdims=True))
    ep = ee / jnp.sum(ee, axis=-1, keepdims=True)
    v1 = jnp.max(ep, axis=-1, keepdims=True)
    i1 = jnp.min(jnp.where(in_grp & (ep == v1), lane, big), axis=-1, keepdims=True)
    rest = in_grp & (lane != i1)
    v2 = jnp.max(jnp.where(rest, ep, neg), axis=-1, keepdims=True)
    i2 = jnp.min(jnp.where(rest & (ep == v2), lane, big), axis=-1, keepdims=True)
    denom = v1 + v2
    idx = jnp.where(lane == 0, i1 - N_GROUPS, jnp.where(lane == 1, i2 - N_GROUPS, 0))
    gate = jnp.where(lane == 0, g_val * v1 / denom, jnp.where(lane == 1, g_val * v2 / denom, 0.0))
    return idx, gate


def _mix_kernel(x_ref, attn_ref, yf_ref, yb_ref, r_ref, k_ref, v_ref, af_ref, ab_ref, g_ref,
                g1_ref, sh2_ref, sc2_ref, akey_ref, rk_ref, lng_ref, lnb_ref, ng_ref,
                ones_ref, wo_ref, wr_ref, br_ref,
                x1_o, h2_o, idx_o, gate_o, rank_o, cnt_o, run_scr):
    @pl.when(pl.program_id(0) == 0)
    def _():
        run_scr[...] = jnp.zeros_like(run_scr)

    ones_bd = ones_ref[...]
    inv = 1.0 / RWKV_HEAD_DIM
    y = yf_ref[...] + yb_ref[...]
    mu = _head_sum(y, ones_bd) * inv
    dy = y - mu
    var = _head_sum(dy * dy, ones_bd) * inv
    yn = dy * lax.rsqrt(var + LNX_EPS) * lng_ref[...] + lnb_ref[...]
    k_sum = k_ref[...] * (2.0 + (af_ref[...] + ab_ref[...] - 2.0) * akey_ref[...])
    bonus = _head_sum(r_ref[...] * k_sum * rk_ref[...], ones_bd) * v_ref[...]
    rw = ((yn + bonus) * g_ref[...]).astype(BF16)
    o = _dot(attn_ref[...], wo_ref[0:MLA_WIDTH, :]) + _dot(rw, wo_ref[MLA_WIDTH:D_MODEL, :])
    x1 = x_ref[...] + g1_ref[...] * o
    x1_o[...] = x1
    h = x1 * lax.rsqrt(jnp.mean(x1 * x1, axis=-1, keepdims=True) + NORM_EPS) * ng_ref[...]
    h2 = h * (1.0 + sc2_ref[...]) + sh2_ref[...]
    h2_o[...] = h2
    h_hi, h_lo = _split2(h2)
    w_hi = wr_ref[0]
    logits = _dot(h_hi, w_hi) + _dot(h_lo, w_hi) + _dot(h_hi, wr_ref[1]) + br_ref[...]
    idx, gate = _route(logits)
    idx_o[...] = idx
    gate_o[...] = gate
    rank_o[...] = _slot_rank(idx, run_scr)
    cnt_o[...] = run_scr[...]


def _mix_call(x, attn, yscan, r, k, v, lr, g, mod_tab, key_a, bonus_rk, lnx_g, lnx_b, norm_g,
              ones_bd, w_out_b, w_router, b_router, tpb):
    n_batch, t_len, _ = x.shape
    tpl = t_len // TM
    n = n_batch * t_len

    def lat(i):
        return (i // tpl) * tpb + 1 + i % tpl

    def tok(cols, col_blk=0):
        return pl.BlockSpec((TM, cols), lambda i: (lat(i), col_blk))

    def mod_spec(kk):
        return pl.BlockSpec((None, 1, D_MODEL), lambda i: ((i // tpl) * 6 + kk, 0, 0))

    def vec(cols):
        return _resident((1, cols), lambda i: (0, 0))

    tile_out = pl.BlockSpec((TM, ROUTER_COLS), lambda i: (i, 0))
    return pl.pallas_call(
        _mix_kernel,
        grid=(n // TM,),
        in_specs=[
            pl.BlockSpec((None, TM, D_MODEL), lambda i: (i // tpl, i % tpl, 0)),
            pl.BlockSpec((TM, MLA_WIDTH), lambda i: (i, 0)),
            pl.BlockSpec((None, TM, RWKV_WIDTH), lambda i: (0, lat(i), 0)),
            pl.BlockSpec((None, TM, RWKV_WIDTH), lambda i: (1, lat(i), 0)),
            tok(RWKV_WIDTH), tok(RWKV_WIDTH), tok(RWKV_WIDTH),
            tok(RWKV_WIDTH, 0), tok(RWKV_WIDTH, 1), tok(RWKV_WIDTH),
            mod_spec(2), mod_spec(3), mod_spec(4),
            vec(RWKV_WIDTH), vec(RWKV_WIDTH), vec(RWKV_WIDTH), vec(RWKV_WIDTH), vec(D_MODEL),
            _resident((LANES, LANES), lambda i: (0, 0)),
            _resident((D_MODEL, D_MODEL), lambda i: (0, 0)),
            _resident((2, D_MODEL, ROUTER_COLS), lambda i: (0, 0, 0)),
            vec(ROUTER_COLS),
        ],
        out_specs=[
            pl.BlockSpec((TM, D_MODEL), lambda i: (i, 0)),
            pl.BlockSpec((TM, D_MODEL), lambda i: (i, 0)),
            tile_out, tile_out, tile_out,
            pl.BlockSpec((1, ROUTER_COLS), lambda i: (0, 0)),
        ],
        out_shape=[
            jax.ShapeDtypeStruct((n, D_MODEL), F32),
            jax.ShapeDtypeStruct((n, D_MODEL), F32),
            jax.ShapeDtypeStruct((n, ROUTER_COLS), jnp.int32),
            jax.ShapeDtypeStruct((n, ROUTER_COLS), F32),
            jax.ShapeDtypeStruct((n, ROUTER_COLS), jnp.int32),
            jax.ShapeDtypeStruct((1, ROUTER_COLS), F32),
        ],
        scratch_shapes=[pltpu.VMEM((1, ROUTER_COLS), F32)],
        compiler_params=_cparams(("arbitrary",)),
        name="mix",
    )(x, attn, yscan, yscan, r, k, v, lr, lr, g, mod_tab, mod_tab, mod_tab,
      key_a, bonus_rk, lnx_g, lnx_b, norm_g.reshape(1, D_MODEL), ones_bd, w_out_b, w_router, b_router)


def _scatter_kernel(dest_ref, h_ref, init_hbm, xs_hbm, sem):
    del init_hbm
    base = pl.program_id(0) * (TM * TOP_K)

    def row(t, slot):
        return pltpu.make_async_copy(h_ref.at[pl.ds(t, 1)], xs_hbm.at[pl.ds(slot, 1)], sem)

    def start(t, c):
        for kk in range(TOP_K):
            row(t, dest_ref[base + t * TOP_K + kk]).start()
        return c
    lax.fori_loop(0, TM, start, 0, unroll=DMA_UNROLL)

    def wait(t, c):
        for _ in range(TOP_K):
            row(t, 0).wait()
        return c
    lax.fori_loop(0, TM, wait, 0, unroll=DMA_UNROLL)


def _scatter_call(dest, h2, n_slots):
    n = h2.shape[0]
    return pl.pallas_call(
        _scatter_kernel,
        grid_spec=pltpu.PrefetchScalarGridSpec(
            num_scalar_prefetch=1,
            grid=(n // TM,),
            in_specs=[
                pl.BlockSpec((TM, D_MODEL), lambda i, dest: (i, 0)),
                pl.BlockSpec(memory_space=pl.ANY),
            ],
            out_specs=pl.BlockSpec(memory_space=pl.ANY),
            scratch_shapes=[pltpu.SemaphoreType.DMA],
        ),
        out_shape=jax.ShapeDtypeStruct((n_slots, D_MODEL), F32),
        input_output_aliases={2: 0},
        compiler_params=_cparams(("arbitrary",)),
        name="scatter",
    )(dest, h2, jnp.zeros((n_slots, D_MODEL), F32))


def _moe_kernel(be_ref, used_ref, x_ref, w1_ref, w3_ref, w2_ref, y_ref, w1b, w3b, w2b):
    i = pl.program_id(0)

    @pl.when(i < used_ref[0])
    def _():
        @pl.when((i == 0) | (be_ref[i] != be_ref[jnp.maximum(i - 1, 0)]))
        def _():
            w1b[...] = w1_ref[...].astype(BF16)
            w3b[...] = w3_ref[...].astype(BF16)
            w2b[...] = w2_ref[...].astype(BF16)

        x = x_ref[...].astype(BF16)
        a1 = _dot(x, w1b[...])
        a3 = _dot(x, w3b[...])
        hm = (a1 * jax.nn.sigmoid(a1) * a3).astype(BF16)
        y_ref[...] = _dot(hm, w2b[...])

    @pl.when(i >= used_ref[0])
    def _():
        y_ref[...] = jnp.zeros_like(y_ref)


def _moe_call(block_expert, n_used, xs, w1, w3, w2):
    n_blocks = block_expert.shape[0]

    def wspec(shape):
        return pl.BlockSpec((None,) + shape, lambda i, be, used: (be[i], 0, 0))

    return pl.pallas_call(
        _moe_kernel,
        grid_spec=pltpu.PrefetchScalarGridSpec(
            num_scalar_prefetch=2,
            grid=(n_blocks,),
            in_specs=[
                pl.BlockSpec((MOE_BLOCK, D_MODEL), lambda i, be, used: (jnp.minimum(i, used[0] - 1), 0)),
                wspec((D_MODEL, D_EXPERT)),
                wspec((D_MODEL, D_EXPERT)),
                wspec((D_EXPERT, D_MODEL)),
            ],
            out_specs=pl.BlockSpec((MOE_BLOCK, D_MODEL), lambda i, be, used: (i, 0)),
            scratch_shapes=[
                pltpu.VMEM((D_MODEL, D_EXPERT), BF16),
                pltpu.VMEM((D_MODEL, D_EXPERT), BF16),
                pltpu.VMEM((D_EXPERT, D_MODEL), BF16),
            ],
        ),
        out_shape=jax.ShapeDtypeStruct(xs.shape, F32),
        compiler_params=_cparams(("arbitrary",)),
        name="moe",
    )(block_expert, n_used, xs, w1, w3, w2)


def _final_kernel(dest_ref, x1_ref, gate_ref, g2_ref, ng_ref, ys_hbm, o_ref, ybuf, sem):
    base = pl.program_id(0) * (TM * TOP_K)

    def row(t, kk, slot):
        return pltpu.make_async_copy(ys_hbm.at[pl.ds(slot, 1)], ybuf.at[kk, pl.ds(t, 1)], sem)

    def start(t, c):
        for kk in range(TOP_K):
            row(t, kk, dest_ref[base + t * TOP_K + kk]).start()
        return c
    lax.fori_loop(0, TM, start, 0, unroll=DMA_UNROLL)

    def wait(t, c):
        for kk in range(TOP_K):
            row(t, kk, 0).wait()
        return c
    lax.fori_loop(0, TM, wait, 0, unroll=DMA_UNROLL)

    gate = gate_ref[...]
    y = ybuf[0] * gate[:, 0:1] + ybuf[1] * gate[:, 1:2]
    x2 = x1_ref[...] + g2_ref[...] * y
    o_ref[...] = x2 * lax.rsqrt(jnp.mean(x2 * x2, axis=-1, keepdims=True) + NORM_EPS) * ng_ref[...]


def _final_call(dest, x1, ys, gates, mod_tab, final_g, t_len):
    n = x1.shape[0]
    tpl = t_len // TM
    return pl.pallas_call(
        _final_kernel,
        grid_spec=pltpu.PrefetchScalarGridSpec(
            num_scalar_prefetch=1,
            grid=(n // TM,),
            in_specs=[
                pl.BlockSpec((TM, D_MODEL), lambda i, dest: (i, 0)),
                pl.BlockSpec((TM, ROUTER_COLS), lambda i, dest: (i, 0)),
                pl.BlockSpec((None, 1, D_MODEL), lambda i, dest: ((i // tpl) * 6 + 5, 0, 0)),
                _resident((1, D_MODEL), lambda i, dest: (0, 0)),
                pl.BlockSpec(memory_space=pl.ANY),
            ],
            out_specs=pl.BlockSpec((TM, D_MODEL), lambda i, dest: (i, 0)),
            scratch_shapes=[pltpu.VMEM((TOP_K, TM, D_MODEL), F32), pltpu.SemaphoreType.DMA],
        ),
        out_shape=jax.ShapeDtypeStruct((n, D_MODEL), F32),
        compiler_params=_cparams(("arbitrary",)),
        name="final",
    )(dest, x1, gates, mod_tab, final_g.reshape(1, D_MODEL), ys)


def _pad_cols(w, width):
    return jnp.pad(w, ((0, 0), (0, width - w.shape[1])))


_ROPE_SWAP = np.concatenate([np.arange(16, 32), np.arange(0, 16), np.arange(48, 64), np.arange(32, 48)])


def _rope_tables(t_len):
    pos = jnp.arange(t_len)
    inv_freq = ROPE_THETA ** (-jnp.arange(0, ROPE_AXIS_DIM, 2, dtype=F32) / ROPE_AXIS_DIM)
    ang_r = (pos // GRID_W)[:, None].astype(F32) * inv_freq
    ang_c = (pos % GRID_W)[:, None].astype(F32) * inv_freq
    cos = jnp.concatenate([jnp.cos(ang_r)] * 2 + [jnp.cos(ang_c)] * 2, axis=1)
    sin = jnp.concatenate([-jnp.sin(ang_r), jnp.sin(ang_r), -jnp.sin(ang_c), jnp.sin(ang_c)], axis=1)
    cos = jnp.concatenate([jnp.ones((CTX_LEN, QK_ROPE_DIM), F32), cos], axis=0)
    sin = jnp.concatenate([jnp.zeros((CTX_LEN, QK_ROPE_DIM), F32), sin], axis=0)
    rows = cos.shape[0]
    z64 = jnp.zeros((rows, QK_ROPE_DIM), F32)
    ck = jnp.concatenate([cos, z64], axis=1)
    sk = jnp.concatenate([sin, z64], axis=1)
    q_scale = MLA_SCALE * math.log2(math.e)
    cq = q_scale * jnp.concatenate([jnp.ones((rows, QK_NOPE_DIM), F32), cos, z64], axis=1)
    sq = q_scale * jnp.concatenate([jnp.zeros((rows, QK_NOPE_DIM), F32), sin, z64], axis=1)
    return ck, sk, cq, sq


def _slot_tables(idx2, rank2, counts, n_tokens):
    n_blocks = (n_tokens * TOP_K + N_EXPERTS * (MOE_BLOCK - 1) + MOE_BLOCK - 1) // MOE_BLOCK
    padded = (counts + MOE_BLOCK - 1) // MOE_BLOCK * MOE_BLOCK
    pad_end = jnp.cumsum(padded)
    pad_start = pad_end - padded
    experts = jnp.arange(N_EXPERTS, dtype=jnp.int32)
    first = jnp.sum(jnp.where(idx2[..., None] == experts, pad_start, 0), axis=-1)
    dest = (first + rank2).reshape(-1).astype(jnp.int32)
    block_start = jnp.arange(n_blocks, dtype=jnp.int32) * MOE_BLOCK
    block_expert = jnp.minimum(jnp.sum(block_start[:, None] >= pad_end[None, :], axis=1), N_EXPERTS - 1)
    n_used = (pad_end[-1] // MOE_BLOCK).reshape(1)
    return dest, block_expert.astype(jnp.int32), n_used.astype(jnp.int32), n_blocks * MOE_BLOCK


def kernel(x, c, ctx, c_ctx, w_mod, b_mod, norm_attn_g, norm_ffn_g, w_in, shift_mu, q_norm_g, w_uq, kv_norm_g, w_ukv, decay_w0, decay_up, iclr_a0, iclr_up, gate_up, key_k, key_a, bonus_r_k, lnx_g, lnx_b, w_out, w_grp, b_grp, w_exp, b_exp, w1, w3, w2, final_norm_g):
    n_batch, t_len, _ = x.shape
    assert ctx.shape[1] == CTX_LEN == TM and t_len % TM == 0 and w_mod.shape[0] == 1
    tpb = (CTX_LEN + t_len) // TM
    cpb = (CTX_LEN + t_len) // CHUNK
    n = n_batch * t_len

    c_rows = jnp.zeros((8, D_MODEL), F32).at[:n_batch].set(c).at[n_batch].set(c_ctx)
    mod_tab = _mod_call(c_rows, w_mod[0], b_mod[0]).reshape(8 * 6, 1, D_MODEL)

    wi = w_in[0]
    w_kr = wi[:, 768:MLA_IN]
    o = MLA_IN
    w_in_p = jnp.concatenate([
        wi[:, 0:768], _pad_cols(w_kr, LANES), _pad_cols(w_kr[:, _ROPE_SWAP], LANES),
        wi[:, o:o + COLS_RKV],
        _pad_cols(wi[:, o + COLS_RKV:o + COLS_RKV + DECAY_LORA], LANES),
        _pad_cols(wi[:, o + COLS_RKV + DECAY_LORA:o + COLS_RKV + DECAY_LORA + ICLR_LORA], LANES),
        _pad_cols(wi[:, o + COLS_RKV + DECAY_LORA + ICLR_LORA:], 2 * LANES),
    ], axis=1).astype(BF16)
    p_mla, p_rkv, p_lora = _project_call(x, ctx, mod_tab, norm_attn_g[0], w_in_p, tpb)

    mu = shift_mu[0]
    mu_rkv = mu[:, 0:COLS_RKV]
    mu_lora = jnp.concatenate([
        _pad_cols(mu[:, COLS_RKV:COLS_RKV + DECAY_LORA], LANES),
        _pad_cols(mu[:, COLS_RKV + DECAY_LORA:COLS_RKV + DECAY_LORA + ICLR_LORA], LANES),
        _pad_cols(mu[:, COLS_RKV + DECAY_LORA + ICLR_LORA:], 2 * LANES)], axis=1)

    def lora_up(w):
        both = jnp.concatenate([w[0], w[1]], axis=1)
        return jnp.pad(both, ((0, LANES - both.shape[0]), (0, 0))).astype(BF16)

    gup = jnp.pad(gate_up[0], ((0, 2 * LANES - GATE_LORA), (0, 0))).astype(BF16)
    r, k, v, lw, lr, g = _prep_call(
        p_rkv, p_lora, mu_rkv, mu_lora, lora_up(decay_up[0]), lora_up(iclr_up[0]), gup,
        decay_w0[0].reshape(1, -1), iclr_a0[0].reshape(1, -1), tpb)
    key_k2 = key_k[0].reshape(1, -1)
    key_a2 = key_a[0].reshape(1, -1)
    yscan = _scan_call(r, k, v, lw, lr, key_k2, key_a2, n_batch, cpb, CTX_LEN // CHUNK)

    hd = QK_NOPE_DIM + QK_ROPE_DIM
    wq = w_uq[0].reshape(Q_LORA_RANK, MLA_HEADS, hd)
    zq = jnp.zeros((Q_LORA_RANK, MLA_HEADS, QK_ROPE_DIM), F32)
    wa = jnp.concatenate([wq, zq], axis=2).reshape(Q_LORA_RANK, -1).astype(BF16)
    wb = jnp.concatenate([jnp.zeros((Q_LORA_RANK, MLA_HEADS, QK_NOPE_DIM), F32),
                          wq[:, :, QK_NOPE_DIM:][:, :, _ROPE_SWAP], zq], axis=2
                         ).reshape(Q_LORA_RANK, -1).astype(BF16)
    wkv3 = w_ukv[0].reshape(KV_LORA_RANK, MLA_HEADS, QK_NOPE_DIM + V_HEAD_DIM)
    wkv = jnp.concatenate([wkv3[:, :, :QK_NOPE_DIM].reshape(KV_LORA_RANK, -1),
                           wkv3[:, :, QK_NOPE_DIM:].reshape(KV_LORA_RANK, -1)], axis=1).astype(BF16)
    q, kmat, vmat = _mla_prep_call(p_mla, _rope_tables(t_len), q_norm_g[0], kv_norm_g[0], wa, wb, wkv, tpb)
    attn = _attn_call(q, kmat, vmat, n_batch, t_len, tpb)

    lane = np.arange(LANES)
    ones_bd = jnp.asarray((lane[:, None] // RWKV_HEAD_DIM) == (lane[None, :] // RWKV_HEAD_DIM), BF16)
    w_router = _pad_cols(jnp.concatenate([w_grp[0], w_exp[0]], axis=1), ROUTER_COLS)
    w_router_hi = w_router.astype(BF16)
    w_router2 = jnp.stack([w_router_hi, (w_router - w_router_hi.astype(F32)).astype(BF16)])
    b_router = _pad_cols(jnp.concatenate([b_grp[0], b_exp[0]]).reshape(1, -1), ROUTER_COLS)
    x1, h2, idx, gates, rank, counts = _mix_call(
        x, attn, yscan, r, k, v, lr, g, mod_tab, key_a2, bonus_r_k[0].reshape(1, -1),
        lnx_g[0].reshape(1, -1), lnx_b[0].reshape(1, -1), norm_ffn_g[0], ones_bd,
        w_out[0].astype(BF16), w_router2, b_router, tpb)

    dest, block_expert, n_used, n_slots = _slot_tables(
        idx[:, :TOP_K], rank[:, :TOP_K], counts[0, :N_EXPERTS].astype(jnp.int32), n)
    xs = _scatter_call(dest, h2, n_slots)
    ys = _moe_call(block_expert, n_used, xs, w1[0], w3[0], w2[0])
    out = _final_call(dest, x1, ys, gates, mod_tab, final_norm_g, t_len)
    return out.reshape(n_batch, t_len, D_MODEL)
```

```python
import functools
import math

import jax
import jax.numpy as jnp
import numpy as np
from jax import lax
from jax.experimental import pallas as pl
from jax.experimental.pallas import tpu as pltpu

F32 = jnp.float32
BF16 = jnp.bfloat16
HIGHEST = lax.Precision.HIGHEST

D_MODEL = 2048
CTX_LEN = 256
GRID_W = 64
NORM_EPS = 1e-6

MLA_HEADS = 8
QK_NOPE_DIM = 128
QK_ROPE_DIM = 64
V_HEAD_DIM = 128
Q_LORA_RANK = 512
KV_LORA_RANK = 256
MLA_WIDTH = MLA_HEADS * V_HEAD_DIM
MLA_SCALE = (QK_NOPE_DIM + QK_ROPE_DIM) ** -0.5
ROPE_THETA = 10000.0
ROPE_AXIS_DIM = QK_ROPE_DIM // 2
QK_PAD_DIM = 256

RWKV_HEAD_DIM = 64
RWKV_WIDTH = D_MODEL - MLA_WIDTH
RWKV_HEADS = RWKV_WIDTH // RWKV_HEAD_DIM
DECAY_LORA = 64
ICLR_LORA = 64
GATE_LORA = 160
LNX_EPS = 64e-5

N_GROUPS = 4
EXPERTS_PER_GROUP = 8
N_EXPERTS = N_GROUPS * EXPERTS_PER_GROUP
TOP_K = 2
D_EXPERT = 512
MOE_BLOCK = 256

MLA_IN = Q_LORA_RANK + KV_LORA_RANK + QK_ROPE_DIM
LANES = 128
TM = 256
CHUNK = 64
PAIR = 2 * RWKV_HEAD_DIM
N_PAIRS = RWKV_WIDTH // PAIR
HEAD_GROUP = 256
PAIRS_PER_STEP = 8
ATTN_TQ = 512
ATTN_TK = 768
MIX_SUB = 2
DMA_UNROLL = 8
VMEM_LIMIT = 56 * 1024 * 1024

COLS_MLA = 1024
COLS_RKV = 3 * RWKV_WIDTH
COLS_LORA = 512
COLS_IN = COLS_MLA + COLS_RKV + COLS_LORA
ROUTER_COLS = 128


def _cparams(sem):
    return pltpu.CompilerParams(dimension_semantics=sem, vmem_limit_bytes=VMEM_LIMIT)


def _resident(shape, index_map):
    return pl.BlockSpec(shape, index_map, pipeline_mode=pl.Buffered(1))


def _dot(a, b):
    return jnp.dot(a, b, preferred_element_type=F32)


def _dot_nt(a, b):
    return lax.dot_general(a, b, (((1,), (1,)), ((), ())), preferred_element_type=F32)


def _dot_tn(a, b):
    return lax.dot_general(a, b, (((0,), (0,)), ((), ())), preferred_element_type=F32)


def _split2(x):
    hi = x.astype(BF16)
    lo = (x - hi.astype(F32)).astype(BF16)
    return hi, lo


def _head_ones():
    row = lax.broadcasted_iota(jnp.int32, (HEAD_GROUP, HEAD_GROUP), 0)
    col = lax.broadcasted_iota(jnp.int32, (HEAD_GROUP, HEAD_GROUP), 1)
    return jnp.where(row // RWKV_HEAD_DIM == col // RWKV_HEAD_DIM, 1.0, 0.0).astype(BF16)


def _head_sum(x, ones_bd):
    rows = x.shape[0]
    n = x.shape[1] // HEAD_GROUP
    parts = [half[:, c * HEAD_GROUP:(c + 1) * HEAD_GROUP] for half in _split2(x) for c in range(n)]
    res = _dot(jnp.concatenate(parts, axis=0), ones_bd)
    return jnp.concatenate([res[c * rows:(c + 1) * rows] + res[(n + c) * rows:(n + c + 1) * rows]
                            for c in range(n)], axis=1)


def _split3(x):
    hi = x.astype(BF16)
    r1 = x - hi.astype(F32)
    mid = r1.astype(BF16)
    lo = (r1 - mid.astype(F32)).astype(BF16)
    return hi, mid, lo


def _mod_kernel(c_ref, w_ref, b_ref, o_ref):
    c = c_ref[...]
    s = c * jax.nn.sigmoid(c)
    o_ref[...] = jnp.dot(s, w_ref[...], preferred_element_type=F32, precision=HIGHEST) + b_ref[...]


def _mod_call(c_rows, w_mod, b_mod):
    n = w_mod.shape[1]
    tn = 1024
    return pl.pallas_call(
        _mod_kernel,
        grid=(n // tn,),
        in_specs=[
            pl.BlockSpec((8, D_MODEL), lambda i: (0, 0)),
            pl.BlockSpec((D_MODEL, tn), lambda i: (0, i)),
            pl.BlockSpec((1, tn), lambda i: (0, i)),
        ],
        out_specs=pl.BlockSpec((8, tn), lambda i: (0, i)),
        out_shape=jax.ShapeDtypeStruct((8, n), F32),
        compiler_params=_cparams(("arbitrary",)),
        name="mod",
    )(c_rows, w_mod, b_mod.reshape(1, n))


def _project_kernel(tpb, x_ref, ctx_ref, sh_ref, sc_ref, g_ref, w_ref, o_mla, o_rkv, o_lora):
    is_ctx = (pl.program_id(0) % tpb) == 0
    xin = jnp.where(is_ctx, ctx_ref[...], x_ref[...])
    ms = jnp.mean(xin * xin, axis=-1, keepdims=True)
    h = xin * lax.rsqrt(ms + NORM_EPS) * g_ref[...]
    hb = (h * (1.0 + sc_ref[...]) + sh_ref[...]).astype(BF16)
    o_mla[...] = _dot(hb, w_ref[:, 0:COLS_MLA])
    o_rkv[...] = _dot(hb, w_ref[:, COLS_MLA:COLS_MLA + COLS_RKV])
    o_lora[...] = _dot(hb, w_ref[:, COLS_MLA + COLS_RKV:COLS_IN])


def _mod_row(i, tpb, n_batch):
    return jnp.where(i % tpb == 0, n_batch, i // tpb)


def _project_call(x, ctx, mod_tab, norm_g, w_in_p, tpb):
    n_batch, t_len, _ = x.shape
    nt = n_batch * tpb * TM

    def mod_spec(k):
        return pl.BlockSpec((None, 1, D_MODEL), lambda i: (_mod_row(i, tpb, n_batch) * 6 + k, 0, 0))

    return pl.pallas_call(
        functools.partial(_project_kernel, tpb),
        grid=(n_batch * tpb,),
        in_specs=[
            pl.BlockSpec((None, TM, D_MODEL), lambda i: (i // tpb, jnp.maximum(i % tpb - 1, 0), 0)),
            pl.BlockSpec((None, TM, D_MODEL), lambda i: (i // tpb, 0, 0)),
            mod_spec(0),
            mod_spec(1),
            _resident((1, D_MODEL), lambda i: (0, 0)),
            _resident((D_MODEL, COLS_IN), lambda i: (0, 0)),
        ],
        out_specs=[
            pl.BlockSpec((TM, COLS_MLA), lambda i: (i, 0)),
            pl.BlockSpec((TM, COLS_RKV), lambda i: (i, 0)),
            pl.BlockSpec((TM, COLS_LORA), lambda i: (i, 0)),
        ],
        out_shape=[
            jax.ShapeDtypeStruct((nt, COLS_MLA), F32),
            jax.ShapeDtypeStruct((nt, COLS_RKV), F32),
            jax.ShapeDtypeStruct((nt, COLS_LORA), F32),
        ],
        compiler_params=_cparams(("arbitrary",)),
        name="project",
    )(x, ctx, mod_tab, mod_tab, norm_g.reshape(1, D_MODEL), w_in_p)


def _prep_kernel(tpb, p_ref, pp_ref, pn_ref, l_ref, lp_ref, ln_ref, mu_ref, mul_ref,
                 wup_ref, aup_ref, gup_ref, w0_ref, a0_ref,
                 r_o, k_o, v_o, lw_o, a_o, g_o):
    j = pl.program_id(0) % tpb
    no_prev = j <= 1
    no_next = (j == 0) | (j == tpb - 1)

    def shifted(main, prev_blk, next_blk, mu):
        rows = lax.broadcasted_iota(jnp.int32, main.shape, 0)
        prow = jnp.where(no_prev, 0.0, prev_blk[7:8, :])
        nrow = jnp.where(no_next, 0.0, next_blk[0:1, :])
        prev = jnp.where(rows == 0, prow, pltpu.roll(main, 1, 0))
        nxt = jnp.where(rows == TM - 1, nrow, pltpu.roll(main, TM - 1, 0))
        return main + mu[0:1, :] * (prev - main) + mu[1:2, :] * (nxt - main)

    for c, out in enumerate((r_o, k_o, v_o)):
        sl = slice(c * RWKV_WIDTH, (c + 1) * RWKV_WIDTH)
        out[...] = shifted(p_ref[:, sl], pp_ref[:, sl], pn_ref[:, sl], mu_ref[:, sl])

    lo = shifted(l_ref[...], lp_ref[...], ln_ref[...], mul_ref[...])
    wl = jnp.tanh(lo[:, 0:LANES]).astype(BF16)
    al = lo[:, LANES:2 * LANES].astype(BF16)
    gl = jax.nn.sigmoid(lo[:, 2 * LANES:4 * LANES]).astype(BF16)
    w_raw = w0_ref[...] + _dot(wl, wup_ref[...])
    lw_o[...] = -math.exp(-0.5) * jax.nn.sigmoid(w_raw)
    a_o[...] = jax.nn.sigmoid(a0_ref[...] + _dot(al, aup_ref[...]))
    g_o[...] = _dot(gl, gup_ref[...])


def _prep_call(p_rkv, p_lora, mu_rkv, mu_lora, wup, aup, gup, w0, a0, tpb):
    nt = p_rkv.shape[0]
    last8 = nt // 8 - 1
    sub = TM // 8

    def halo(cols):
        return [
            pl.BlockSpec((TM, cols), lambda i: (i, 0)),
            pl.BlockSpec((8, cols), lambda i: (jnp.maximum(i * sub - 1, 0), 0)),
            pl.BlockSpec((8, cols), lambda i: (jnp.minimum((i + 1) * sub, last8), 0)),
        ]

    w2 = 2 * RWKV_WIDTH
    return pl.pallas_call(
        functools.partial(_prep_kernel, tpb),
        grid=(nt // TM,),
        in_specs=halo(COLS_RKV) + halo(COLS_LORA) + [
            _resident((2, COLS_RKV), lambda i: (0, 0)),
            _resident((2, COLS_LORA), lambda i: (0, 0)),
            _resident((LANES, w2), lambda i: (0, 0)),
            _resident((LANES, w2), lambda i: (0, 0)),
            _resident((2 * LANES, RWKV_WIDTH), lambda i: (0, 0)),
            _resident((1, w2), lambda i: (0, 0)),
            _resident((1, w2), lambda i: (0, 0)),
        ],
        out_specs=[
            pl.BlockSpec((TM, RWKV_WIDTH), lambda i: (i, 0)),
            pl.BlockSpec((TM, RWKV_WIDTH), lambda i: (i, 0)),
            pl.BlockSpec((TM, RWKV_WIDTH), lambda i: (i, 0)),
            pl.BlockSpec((TM, w2), lambda i: (i, 0)),
            pl.BlockSpec((TM, w2), lambda i: (i, 0)),
            pl.BlockSpec((TM, RWKV_WIDTH), lambda i: (i, 0)),
        ],
        out_shape=[
            jax.ShapeDtypeStruct((nt, RWKV_WIDTH), F32),
            jax.ShapeDtypeStruct((nt, RWKV_WIDTH), F32),
            jax.ShapeDtypeStruct((nt, RWKV_WIDTH), F32),
            jax.ShapeDtypeStruct((nt, w2), F32),
            jax.ShapeDtypeStruct((nt, w2), F32),
            jax.ShapeDtypeStruct((nt, RWKV_WIDTH), F32),
        ],
        compiler_params=_cparams(("arbitrary",)),
        name="prep",
    )(p_rkv, p_rkv, p_rkv, p_lora, p_lora, p_lora, mu_rkv, mu_lora, wup, aup, gup, w0, a0)


def _stack_heads(x):
    lane = lax.broadcasted_iota(jnp.int32, x.shape, 1)
    zero = jnp.zeros_like(x)
    return jnp.concatenate([jnp.where(lane < RWKV_HEAD_DIM, x, zero),
                            jnp.where(lane >= RWKV_HEAD_DIM, x, zero)], axis=0)


def _unstack_heads(z):
    half = z.shape[0] // 2
    return z[:half] + z[half:]


def _scan_kernel(r_ref, k_ref, v_ref, lw_ref, a_ref, kkey_ref, akey_ref, y_ref, s_scr):
    rev = pl.program_id(0) == 1

    @pl.when(pl.program_id(3) == 0)
    def _():
        s_scr[...] = jnp.zeros_like(s_scr)

    c2 = 2 * CHUNK
    t64 = lax.broadcasted_iota(jnp.int32, (CHUNK, CHUNK), 0)
    i64 = lax.broadcasted_iota(jnp.int32, (CHUNK, CHUNK), 1)
    cum_mat = jnp.where(jnp.where(rev, t64 - i64, i64 - t64) <= 0, 1.0, 0.0).astype(BF16)

    row = lax.broadcasted_iota(jnp.int32, (c2, LANES), 0)
    col = lax.broadcasted_iota(jnp.int32, (c2, LANES), 1)
    t_idx = row % CHUNK
    i_idx = col % CHUNK
    order = jnp.where(rev, t_idx - i_idx, i_idx - t_idx)
    keep = (order < 0) | ((order == 0) & (row >= CHUNK))
    same_head = (row // RWKV_HEAD_DIM) == (col // RWKV_HEAD_DIM)
    eye = row == col
    eye_f = jnp.where(eye, 1.0, 0.0)

    zero = jnp.zeros((c2, LANES), F32)
    pairs = range(PAIRS_PER_STEP)

    def pair(x, p):
        return x[:, p * PAIR:(p + 1) * PAIR]

    r = r_ref[...]
    k = k_ref[...]
    lw = lw_ref[...]
    lr = a_ref[...]
    vb = v_ref[...].astype(BF16)
    kraw = k * kkey_ref[...]
    w_hi, w_mid, w_lo = _split3(lw)
    lp = _dot(cum_mat, w_hi) + _dot(cum_mat, w_mid) + _dot(cum_mat, w_lo)
    kk = kraw * lax.rsqrt(_head_sum(kraw * kraw, _head_ones()) + 1e-12)
    b = kk * lr
    kd = k * (1.0 + (lr - 1.0) * akey_ref[...])
    ltot = jnp.where(rev, lp[0:1, :], lp[CHUNK - 1:CHUNK, :])
    e_neg = jnp.exp(-lp)
    e_rest = jnp.exp(ltot - lp)
    e_tot = jnp.exp(ltot)
    at = -kk * jnp.exp(lp - lw)
    rt = r * jnp.exp(lp)
    at_b = at.astype(BF16)
    rt_b = rt.astype(BF16)
    bt_b = (b * e_neg).astype(BF16)
    kt_b = (kd * e_neg).astype(BF16)
    bh = (b * e_rest).astype(BF16)
    kh = (kd * e_rest).astype(BF16)

    sv = [_stack_heads(pair(vb, p)) for p in pairs]
    ar = [jnp.concatenate([pair(at_b, p), pair(rt_b, p)], axis=0) for p in pairs]
    ab = [jnp.where(keep, _dot_nt(ar[p], _stack_heads(pair(bt_b, p))), zero) for p in pairs]
    ak = [jnp.where(keep, _dot_nt(ar[p], _stack_heads(pair(kt_b, p))), zero) for p in pairs]
    a_rb = [ab[p][CHUNK:].astype(BF16) for p in pairs]
    akv = [_dot(ak[p].astype(BF16), sv[p]) for p in pairs]
    x1 = [akv[p][:CHUNK] for p in pairs]
    z_rk = [akv[p][CHUNK:] for p in pairs]

    pw = [_stack_heads(ab[p][:CHUNK]) for p in pairs]
    tm = [eye_f + pw[p] for p in pairs]
    pw = [_dot(pw[p].astype(BF16), pw[p].astype(BF16)) for p in pairs]
    for _ in range(int(math.log2(CHUNK)) - 2):
        both = [_dot(jnp.concatenate([tm[p], pw[p]], axis=0).astype(BF16), pw[p].astype(BF16)) for p in pairs]
        tm = [tm[p] + both[p][:c2] for p in pairs]
        pw = [both[p][c2:] for p in pairs]
    tm = [tm[p] + _dot(tm[p].astype(BF16), pw[p].astype(BF16)) for p in pairs]
    t_p = [_unstack_heads(tm[p]).astype(BF16) for p in pairs]

    wg = [_dot(t_p[p], jnp.concatenate([_stack_heads(x1[p].astype(BF16)), _stack_heads(pair(at_b, p))], axis=1))
          for p in pairs]
    w_b = [wg[p][:, :LANES].astype(BF16) for p in pairs]
    g_b = [wg[p][:, LANES:].astype(BF16) for p in pairs]
    qz = [_dot(a_rb[p], jnp.concatenate([_stack_heads(g_b[p]), _stack_heads(w_b[p])], axis=1)) for p in pairs]
    gz = jnp.zeros((CHUNK, LANES), BF16)
    mn = [_dot_tn(jnp.concatenate([jnp.concatenate([w_b[p], g_b[p]], axis=1),
                                   jnp.concatenate([pair(vb, p), gz], axis=1)], axis=0),
                  jnp.concatenate([pair(bh, p), pair(kh, p)], axis=0)) for p in pairs]
    n_st = [jnp.where(same_head, mn[p][:c2], zero) for p in pairs]
    m_bd = [jnp.where(same_head, mn[p][c2:], zero).astype(BF16) for p in pairs]

    s_old = [s_scr[p] for p in pairs]
    s_b = [s_old[p].astype(BF16) for p in pairs]
    for p in pairs:
        q = (pair(rt, p) + qz[p][:, :LANES]).astype(BF16)
        y_ref[:, p * PAIR:(p + 1) * PAIR] = _dot_nt(q, _stack_heads(s_b[p])) + qz[p][:, LANES:] + z_rk[p]
    for p in pairs:
        s_scr[p] = s_old[p] * pair(e_tot, p) + _dot(s_b[p], m_bd[p]) + _unstack_heads(n_st[p])


def _scan_call(r, k, v, lw, lr, key_k, key_a, n_batch, cpb, ctx_chunks):
    nt = r.shape[0]
    groups = N_PAIRS // PAIRS_PER_STEP
    gw = PAIRS_PER_STEP * PAIR

    def chunk_row(d, b, j):
        back = jnp.where(j < ctx_chunks, ctx_chunks - 1 - j, cpb + ctx_chunks - 1 - j)
        return b * cpb + jnp.where(d == 0, j, back)

    shared = pl.BlockSpec((CHUNK, gw), lambda d, b, g, j: (chunk_row(d, b, j), g))
    per_dir = pl.BlockSpec((CHUNK, gw), lambda d, b, g, j: (chunk_row(d, b, j), d * groups + g))
    keys = pl.BlockSpec((1, gw), lambda d, b, g, j: (0, g))
    return pl.pallas_call(
        _scan_kernel,
        grid=(2, n_batch, groups, cpb),
        in_specs=[shared, shared, shared, per_dir, per_dir, keys, keys],
        out_specs=pl.BlockSpec((None, CHUNK, gw), lambda d, b, g, j: (d, chunk_row(d, b, j), g)),
        out_shape=jax.ShapeDtypeStruct((2, nt, RWKV_WIDTH), F32),
        scratch_shapes=[pltpu.VMEM((PAIRS_PER_STEP, RWKV_HEAD_DIM, PAIR), F32)],
        compiler_params=_cparams(("arbitrary", "arbitrary", "arbitrary", "arbitrary")),
        name="scan",
    )(r, k, v, lw, lr, key_k, key_a)


def _mla_prep_kernel(p_ref, ck_ref, sk_ref, cq_ref, sq_ref, qg_ref, kvg_ref, wa_ref, wb_ref, wkv_ref,
                     q_o, k_o, v_o):
    cq = p_ref[:, 0:Q_LORA_RANK]
    cqn = (cq * lax.rsqrt(jnp.mean(cq * cq, axis=-1, keepdims=True) + NORM_EPS) * qg_ref[...]).astype(BF16)
    ckv = p_ref[:, Q_LORA_RANK:Q_LORA_RANK + KV_LORA_RANK]
    ckvn = (ckv * lax.rsqrt(jnp.mean(ckv * ckv, axis=-1, keepdims=True) + NORM_EPS) * kvg_ref[...]).astype(BF16)
    kr_a = p_ref[:, 768:896]
    kr_b = p_ref[:, 896:1024]
    k_rot = (kr_a * ck_ref[...] + kr_b * sk_ref[...]).astype(BF16)
    cos_q = cq_ref[...]
    sin_q = sq_ref[...]
    for h in range(MLA_HEADS):
        hs = slice(h * QK_PAD_DIM, (h + 1) * QK_PAD_DIM)
        q_o[:, hs] = (_dot(cqn, wa_ref[:, hs]) * cos_q + _dot(cqn, wb_ref[:, hs]) * sin_q).astype(BF16)
        k_o[:, h * QK_PAD_DIM:h * QK_PAD_DIM + QK_NOPE_DIM] = _dot(
            ckvn, wkv_ref[:, h * QK_NOPE_DIM:(h + 1) * QK_NOPE_DIM]).astype(BF16)
        k_o[:, h * QK_PAD_DIM + QK_NOPE_DIM:(h + 1) * QK_PAD_DIM] = k_rot
    v_o[...] = _dot(ckvn, wkv_ref[:, MLA_WIDTH:2 * MLA_WIDTH]).astype(BF16)


def _mla_prep_call(p_mla, tabs, q_norm_g, kv_norm_g, wa, wb, wkv, tpb):
    nt = p_mla.shape[0]
    ck, sk, cq, sq = tabs
    qw = MLA_HEADS * QK_PAD_DIM
    return pl.pallas_call(
        _mla_prep_kernel,
        grid=(nt // TM,),
        in_specs=[
            pl.BlockSpec((TM, COLS_MLA), lambda i: (i, 0)),
            pl.BlockSpec((TM, LANES), lambda i: (i % tpb, 0)),
            pl.BlockSpec((TM, LANES), lambda i: (i % tpb, 0)),
            pl.BlockSpec((TM, QK_PAD_DIM), lambda i: (i % tpb, 0)),
            pl.BlockSpec((TM, QK_PAD_DIM), lambda i: (i % tpb, 0)),
            _resident((1, Q_LORA_RANK), lambda i: (0, 0)),
            _resident((1, KV_LORA_RANK), lambda i: (0, 0)),
            _resident((Q_LORA_RANK, qw), lambda i: (0, 0)),
            _resident((Q_LORA_RANK, qw), lambda i: (0, 0)),
            _resident((KV_LORA_RANK, 2 * MLA_WIDTH), lambda i: (0, 0)),
        ],
        out_specs=[
            pl.BlockSpec((TM, qw), lambda i: (i, 0)),
            pl.BlockSpec((TM, qw), lambda i: (i, 0)),
            pl.BlockSpec((TM, MLA_WIDTH), lambda i: (i, 0)),
        ],
        out_shape=[
            jax.ShapeDtypeStruct((nt, qw), BF16),
            jax.ShapeDtypeStruct((nt, qw), BF16),
            jax.ShapeDtypeStruct((nt, MLA_WIDTH), BF16),
        ],
        compiler_params=_cparams(("arbitrary",)),
        name="mla_prep",
    )(p_mla, ck, sk, cq, sq, q_norm_g.reshape(1, -1), kv_norm_g.reshape(1, -1), wa, wb, wkv)


def _attn_kernel(n_kv, *refs):
    q_refs, (k_ref, v_ref, o_ref) = refs[:-3], refs[-3:]
    q = jnp.concatenate([q_ref[...] for q_ref in q_refs], axis=0)

    m = jnp.full((1, ATTN_TQ), -jnp.inf, F32)
    l = jnp.zeros((1, ATTN_TQ), F32)
    acc = jnp.zeros((V_HEAD_DIM, ATTN_TQ), F32)
    def scores(j):
        return _dot_nt(k_ref[j * ATTN_TK:(j + 1) * ATTN_TK, :], q)

    s_next = scores(0)
    for j in range(n_kv):
        vj = v_ref[j * ATTN_TK:(j + 1) * ATTN_TK, :]
        s = s_next
        if j + 1 < n_kv:
            s_next = scores(j + 1)
        m_new = jnp.maximum(m, jnp.max(s, axis=0, keepdims=True))
        alpha = jnp.exp2(m - m_new)
        p = jnp.exp2(s - m_new)
        l = alpha * l + jnp.sum(p, axis=0, keepdims=True)
        acc = alpha * acc + _dot_tn(vj, p.astype(BF16))
        m = m_new
    o_ref[...] = jnp.transpose(acc / l).astype(o_ref.dtype)


def _attn_call(q, k, v, n_batch, t_len, tpb):
    rows_b = tpb * TM
    assert rows_b % ATTN_TK == 0 and t_len % ATTN_TQ == 0 and ATTN_TQ % TM == 0
    n_q = t_len // ATTN_TQ
    sub = ATTN_TQ // TM
    k3 = k.reshape(n_batch, rows_b, MLA_HEADS * QK_PAD_DIM)
    v3 = v.reshape(n_batch, rows_b, MLA_WIDTH)

    def q_spec(u):
        return pl.BlockSpec((TM, QK_PAD_DIM), lambda b, h, i: (b * tpb + 1 + i * sub + u, h))

    return pl.pallas_call(
        functools.partial(_attn_kernel, rows_b // ATTN_TK),
        grid=(n_batch, MLA_HEADS, n_q),
        in_specs=[q_spec(u) for u in range(sub)] + [
            pl.BlockSpec((None, rows_b, QK_PAD_DIM), lambda b, h, i: (b, 0, h)),
            pl.BlockSpec((None, rows_b, V_HEAD_DIM), lambda b, h, i: (b, 0, h)),
        ],
        out_specs=pl.BlockSpec((ATTN_TQ, V_HEAD_DIM), lambda b, h, i: (b * n_q + i, h)),
        out_shape=jax.ShapeDtypeStruct((n_batch * t_len, MLA_WIDTH), BF16),
        compiler_params=_cparams(("arbitrary", "arbitrary", "arbitrary")),
        name="attention",
    )(*([q] * sub), k3, v3)


def _slot_rank(idx, run_ref):
    lane = lax.broadcasted_iota(jnp.int32, idx.shape, 1)
    oh0 = lane == idx[:, 0:1]
    oh1 = lane == idx[:, 1:2]
    both = jnp.where(oh0 | oh1, 1.0, 0.0)
    t_row = lax.broadcasted_iota(jnp.int32, (TM, TM), 0)
    t_col = lax.broadcasted_iota(jnp.int32, (TM, TM), 1)
    earlier = jnp.where(t_col < t_row, 1.0, 0.0).astype(BF16)
    seen = _dot(earlier, both.astype(BF16)) + run_ref[...]
    r0 = jnp.sum(jnp.where(oh0, seen, 0.0), axis=-1, keepdims=True)
    r1 = jnp.sum(jnp.where(oh1, seen, 0.0), axis=-1, keepdims=True)
    run_ref[...] = run_ref[...] + jnp.sum(both, axis=0, keepdims=True)
    return jnp.where(lane == 0, r0, jnp.where(lane == 1, r1, 0.0)).astype(jnp.int32)


def _route(logits):
    lane = lax.broadcasted_iota(jnp.int32, logits.shape, 1)
    neg = jnp.full_like(logits, -jnp.inf)
    big = jnp.full_like(lane, 2 ** 30)
    is_grp = lane < N_GROUPS
    gl = jnp.where(is_grp, logits, neg)
    ge = jnp.exp(gl - jnp.max(gl, axis=-1, keepdims=True))
    gp = ge / jnp.sum(ge, axis=-1, keepdims=True)
    g_val = jnp.max(gp, axis=-1, keepdims=True)
    g_idx = jnp.min(jnp.where(is_grp & (gp == g_val), lane, big), axis=-1, keepdims=True)
    e_lane = lane - N_GROUPS
    in_grp = (e_lane >= g_idx * EXPERTS_PER_GROUP) & (e_lane < (g_idx + 1) * EXPERTS_PER_GROUP)
    el = jnp.where(in_grp, logits, neg)
    ee = jnp.exp(el - jnp.max(el, axis=-1, keepdims=True))
    ep = ee / jnp.sum(ee, axis=-1, keepdims=True)
    v1 = jnp.max(ep, axis=-1, keepdims=True)
    i1 = jnp.min(jnp.where(in_grp & (ep == v1), lane, big), axis=-1, keepdims=True)
    rest = in_grp & (lane != i1)
    v2 = jnp.max(jnp.where(rest, ep, neg), axis=-1, keepdims=True)
    i2 = jnp.min(jnp.where(rest & (ep == v2), lane, big), axis=-1, keepdims=True)
    denom = v1 + v2
    idx = jnp.where(lane == 0, i1 - N_GROUPS, jnp.where(lane == 1, i2 - N_GROUPS, 0))
    gate = jnp.where(lane == 0, g_val * v1 / denom, jnp.where(lane == 1, g_val * v2 / denom, 0.0))
    return idx, gate


def _mix_kernel(x_ref, attn_ref, yf_ref, yb_ref, r_ref, k_ref, v_ref, af_ref, ab_ref, g_ref,
                g1_ref, sh2_ref, sc2_ref, akey_ref, rk_ref, lng_ref, lnb_ref, ng_ref,
                wo_ref, wr_ref, br_ref,
                x1_o, h2_o, idx_o, gate_o, rank_o, cnt_o, run_scr):
    @pl.when(pl.program_id(0) == 0)
    def _():
        run_scr[...] = jnp.zeros_like(run_scr)

    subs = range(MIX_SUB)
    rows = TM // MIX_SUB

    def part(ref, u):
        return ref[u * rows:(u + 1) * rows, :]

    ones_bd = _head_ones()
    inv = 1.0 / RWKV_HEAD_DIM
    y = [part(yf_ref, u) + part(yb_ref, u) for u in subs]
    mu = [_head_sum(y[u], ones_bd) * inv for u in subs]
    dy = [y[u] - mu[u] for u in subs]
    var = [_head_sum(dy[u] * dy[u], ones_bd) * inv for u in subs]
    k_sum = [part(k_ref, u) * (2.0 + (part(af_ref, u) + part(ab_ref, u) - 2.0) * akey_ref[...]) for u in subs]
    bonus = [_head_sum(part(r_ref, u) * k_sum[u] * rk_ref[...], ones_bd) * part(v_ref, u) for u in subs]
    yn = [dy[u] * lax.rsqrt(var[u] + LNX_EPS) * lng_ref[...] + lnb_ref[...] for u in subs]
    rw = [((yn[u] + bonus[u]) * part(g_ref, u)).astype(BF16) for u in subs]
    o = [_dot(part(attn_ref, u), wo_ref[0:MLA_WIDTH, :]) + _dot(rw[u], wo_ref[MLA_WIDTH:D_MODEL, :])
         for u in subs]
    x1 = [part(x_ref, u) + g1_ref[...] * o[u] for u in subs]
    h2 = [x1[u] * lax.rsqrt(jnp.mean(x1[u] * x1[u], axis=-1, keepdims=True) + NORM_EPS) * ng_ref[...]
          * (1.0 + sc2_ref[...]) + sh2_ref[...] for u in subs]
    h_hl = [_split2(h2[u]) for u in subs]
    both = [_dot(h_hl[u][0], wr_ref[...]) for u in subs]
    lo_hi = [_dot(h_hl[u][1], wr_ref[:, 0:ROUTER_COLS]) for u in subs]
    routed = [_route(both[u][:, 0:ROUTER_COLS] + both[u][:, ROUTER_COLS:2 * ROUTER_COLS] + lo_hi[u] + br_ref[...])
              for u in subs]
    for u in subs:
        sl = slice(u * rows, (u + 1) * rows)
        x1_o[sl, :] = x1[u]
        h2_o[sl, :] = h2[u]
        idx_o[sl, :] = routed[u][0]
        gate_o[sl, :] = routed[u][1]
    rank_o[...] = _slot_rank(jnp.concatenate([routed[u][0] for u in subs], axis=0), run_scr)
    cnt_o[...] = run_scr[...]


def _mix_call(x, attn, yscan, r, k, v, lr, g, mod_tab, key_a, bonus_rk, lnx_g, lnx_b, norm_g,
              w_out_b, w_router, b_router, tpb):
    n_batch, t_len, _ = x.shape
    tpl = t_len // TM
    n = n_batch * t_len

    def lat(i):
        return (i // tpl) * tpb + 1 + i % tpl

    def tok(cols, col_blk=0):
        return pl.BlockSpec((TM, cols), lambda i: (lat(i), col_blk))

    def mod_spec(kk):
        return pl.BlockSpec((None, 1, D_MODEL), lambda i: ((i // tpl) * 6 + kk, 0, 0))

    def vec(cols):
        return _resident((1, cols), lambda i: (0, 0))

    tile_out = pl.BlockSpec((TM, ROUTER_COLS), lambda i: (i, 0))
    return pl.pallas_call(
        _mix_kernel,
        grid=(n // TM,),
        in_specs=[
            pl.BlockSpec((None, TM, D_MODEL), lambda i: (i // tpl, i % tpl, 0)),
            pl.BlockSpec((TM, MLA_WIDTH), lambda i: (i, 0)),
            pl.BlockSpec((None, TM, RWKV_WIDTH), lambda i: (0, lat(i), 0)),
            pl.BlockSpec((None, TM, RWKV_WIDTH), lambda i: (1, lat(i), 0)),
            tok(RWKV_WIDTH), tok(RWKV_WIDTH), tok(RWKV_WIDTH),
            tok(RWKV_WIDTH, 0), tok(RWKV_WIDTH, 1), tok(RWKV_WIDTH),
            mod_spec(2), mod_spec(3), mod_spec(4),
            vec(RWKV_WIDTH), vec(RWKV_WIDTH), vec(RWKV_WIDTH), vec(RWKV_WIDTH), vec(D_MODEL),
            _resident((D_MODEL, D_MODEL), lambda i: (0, 0)),
            _resident((D_MODEL, 2 * ROUTER_COLS), lambda i: (0, 0)),
            vec(ROUTER_COLS),
        ],
        out_specs=[
            pl.BlockSpec((TM, D_MODEL), lambda i: (i, 0)),
            pl.BlockSpec((TM, D_MODEL), lambda i: (i, 0)),
            tile_out, tile_out, tile_out,
            pl.BlockSpec((1, ROUTER_COLS), lambda i: (0, 0)),
        ],
        out_shape=[
            jax.ShapeDtypeStruct((n, D_MODEL), F32),
            jax.ShapeDtypeStruct((n, D_MODEL), F32),
            jax.ShapeDtypeStruct((n, ROUTER_COLS), jnp.int32),
            jax.ShapeDtypeStruct((n, ROUTER_COLS), F32),
            jax.ShapeDtypeStruct((n, ROUTER_COLS), jnp.int32),
            jax.ShapeDtypeStruct((1, ROUTER_COLS), F32),
        ],
        scratch_shapes=[pltpu.VMEM((1, ROUTER_COLS), F32)],
        compiler_params=_cparams(("arbitrary",)),
        name="mix",
    )(x, attn, yscan, yscan, r, k, v, lr, lr, g, mod_tab, mod_tab, mod_tab,
      key_a, bonus_rk, lnx_g, lnx_b, norm_g.reshape(1, D_MODEL), w_out_b, w_router, b_router)


def _scatter_kernel(dest_ref, h_ref, init_hbm, xs_hbm, sem):
    del init_hbm
    base = pl.program_id(0) * (TM * TOP_K)

    def row(t, slot):
        return pltpu.make_async_copy(h_ref.at[pl.ds(t, 1)], xs_hbm.at[pl.ds(slot, 1)], sem)

    def start(t, c):
        for kk in range(TOP_K):
            row(t, dest_ref[base + t * TOP_K + kk]).start()
        return c
    lax.fori_loop(0, TM, start, 0, unroll=DMA_UNROLL)

    def wait(t, c):
        for _ in range(TOP_K):
            row(t, 0).wait()
        return c
    lax.fori_loop(0, TM, wait, 0, unroll=DMA_UNROLL)


def _scatter_call(dest, h2, n_slots):
    n = h2.shape[0]
    return pl.pallas_call(
        _scatter_kernel,
        grid_spec=pltpu.PrefetchScalarGridSpec(
            num_scalar_prefetch=1,
            grid=(n // TM,),
            in_specs=[
                pl.BlockSpec((TM, D_MODEL), lambda i, dest: (i, 0)),
                pl.BlockSpec(memory_space=pl.ANY),
            ],
            out_specs=pl.BlockSpec(memory_space=pl.ANY),
            scratch_shapes=[pltpu.SemaphoreType.DMA],
        ),
        out_shape=jax.ShapeDtypeStruct((n_slots, D_MODEL), F32),
        input_output_aliases={2: 0},
        compiler_params=_cparams(("arbitrary",)),
        name="scatter",
    )(dest, h2, jnp.zeros((n_slots, D_MODEL), F32))


def _moe_kernel(be_ref, used_ref, x_ref, w1_ref, w3_ref, w2_ref, y_ref, w1b, w3b, w2b):
    i = pl.program_id(0)

    @pl.when(i < used_ref[0])
    def _():
        @pl.when((i == 0) | (be_ref[i] != be_ref[jnp.maximum(i - 1, 0)]))
        def _():
            w1b[...] = w1_ref[...].astype(BF16)
            w3b[...] = w3_ref[...].astype(BF16)
            w2b[...] = w2_ref[...].astype(BF16)

        x = x_ref[...].astype(BF16)
        a1 = _dot(x, w1b[...])
        a3 = _dot(x, w3b[...])
        hm = (a1 * jax.nn.sigmoid(a1) * a3).astype(BF16)
        y_ref[...] = _dot(hm, w2b[...])

    @pl.when(i >= used_ref[0])
    def _():
        y_ref[...] = jnp.zeros_like(y_ref)


def _moe_call(block_expert, n_used, xs, w1, w3, w2):
    n_blocks = block_expert.shape[0]

    def wspec(shape):
        return pl.BlockSpec((None,) + shape, lambda i, be, used: (be[i], 0, 0))

    return pl.pallas_call(
        _moe_kernel,
        grid_spec=pltpu.PrefetchScalarGridSpec(
            num_scalar_prefetch=2,
            grid=(n_blocks,),
            in_specs=[
                pl.BlockSpec((MOE_BLOCK, D_MODEL), lambda i, be, used: (jnp.minimum(i, used[0] - 1), 0)),
                wspec((D_MODEL, D_EXPERT)),
                wspec((D_MODEL, D_EXPERT)),
                wspec((D_EXPERT, D_MODEL)),
            ],
            out_specs=pl.BlockSpec((MOE_BLOCK, D_MODEL), lambda i, be, used: (i, 0)),
            scratch_shapes=[
                pltpu.VMEM((D_MODEL, D_EXPERT), BF16),
                pltpu.VMEM((D_MODEL, D_EXPERT), BF16),
                pltpu.VMEM((D_EXPERT, D_MODEL), BF16),
            ],
        ),
        out_shape=jax.ShapeDtypeStruct(xs.shape, F32),
        compiler_params=_cparams(("arbitrary",)),
        name="moe",
    )(block_expert, n_used, xs, w1, w3, w2)


def _final_kernel(dest_ref, x1_ref, gate_ref, g2_ref, ng_ref, ys_hbm, o_ref, ybuf, sem):
    base = pl.program_id(0) * (TM * TOP_K)

    def row(t, kk, slot):
        return pltpu.make_async_copy(ys_hbm.at[pl.ds(slot, 1)], ybuf.at[kk, pl.ds(t, 1)], sem)

    def start(t, c):
        for kk in range(TOP_K):
            row(t, kk, dest_ref[base + t * TOP_K + kk]).start()
        return c
    lax.fori_loop(0, TM, start, 0, unroll=DMA_UNROLL)

    def wait(t, c):
        for kk in range(TOP_K):
            row(t, kk, 0).wait()
        return c
    lax.fori_loop(0, TM, wait, 0, unroll=DMA_UNROLL)

    gate = gate_ref[...]
    y = ybuf[0] * gate[:, 0:1] + ybuf[1] * gate[:, 1:2]
    x2 = x1_ref[...] + g2_ref[...] * y
    o_ref[...] = x2 * lax.rsqrt(jnp.mean(x2 * x2, axis=-1, keepdims=True) + NORM_EPS) * ng_ref[...]


def _final_call(dest, x1, ys, gates, mod_tab, final_g, t_len):
    n = x1.shape[0]
    tpl = t_len // TM
    return pl.pallas_call(
        _final_kernel,
        grid_spec=pltpu.PrefetchScalarGridSpec(
            num_scalar_prefetch=1,
            grid=(n // TM,),
            in_specs=[
                pl.BlockSpec((TM, D_MODEL), lambda i, dest: (i, 0)),
                pl.BlockSpec((TM, ROUTER_COLS), lambda i, dest: (i, 0)),
                pl.BlockSpec((None, 1, D_MODEL), lambda i, dest: ((i // tpl) * 6 + 5, 0, 0)),
                _resident((1, D_MODEL), lambda i, dest: (0, 0)),
                pl.BlockSpec(memory_space=pl.ANY),
            ],
            out_specs=pl.BlockSpec((TM, D_MODEL), lambda i, dest: (i, 0)),
            scratch_shapes=[pltpu.VMEM((TOP_K, TM, D_MODEL), F32), pltpu.SemaphoreType.DMA],
        ),
        out_shape=jax.ShapeDtypeStruct((n, D_MODEL), F32),
        compiler_params=_cparams(("arbitrary",)),
        name="final",
    )(dest, x1, gates, mod_tab, final_g.reshape(1, D_MODEL), ys)


def _pad_cols(w, width):
    return jnp.pad(w, ((0, 0), (0, width - w.shape[1])))


_ROPE_SWAP = np.concatenate([np.arange(16, 32), np.arange(0, 16), np.arange(48, 64), np.arange(32, 48)])


def _rope_tables(t_len):
    pos = jnp.arange(t_len)
    inv_freq = ROPE_THETA ** (-jnp.arange(0, ROPE_AXIS_DIM, 2, dtype=F32) / ROPE_AXIS_DIM)
    ang_r = (pos // GRID_W)[:, None].astype(F32) * inv_freq
    ang_c = (pos % GRID_W)[:, None].astype(F32) * inv_freq
    cos = jnp.concatenate([jnp.cos(ang_r)] * 2 + [jnp.cos(ang_c)] * 2, axis=1)
    sin = jnp.concatenate([-jnp.sin(ang_r), jnp.sin(ang_r), -jnp.sin(ang_c), jnp.sin(ang_c)], axis=1)
    cos = jnp.concatenate([jnp.ones((CTX_LEN, QK_ROPE_DIM), F32), cos], axis=0)
    sin = jnp.concatenate([jnp.zeros((CTX_LEN, QK_ROPE_DIM), F32), sin], axis=0)
    rows = cos.shape[0]
    z64 = jnp.zeros((rows, QK_ROPE_DIM), F32)
    ck = jnp.concatenate([cos, z64], axis=1)
    sk = jnp.concatenate([sin, z64], axis=1)
    q_scale = MLA_SCALE * math.log2(math.e)
    cq = q_scale * jnp.concatenate([jnp.ones((rows, QK_NOPE_DIM), F32), cos, z64], axis=1)
    sq = q_scale * jnp.concatenate([jnp.zeros((rows, QK_NOPE_DIM), F32), sin, z64], axis=1)
    return ck, sk, cq, sq


def _slot_tables(idx2, rank2, counts, n_tokens):
    n_blocks = (n_tokens * TOP_K + N_EXPERTS * (MOE_BLOCK - 1) + MOE_BLOCK - 1) // MOE_BLOCK
    padded = (counts + MOE_BLOCK - 1) // MOE_BLOCK * MOE_BLOCK
    pad_end = jnp.cumsum(padded)
    pad_start = pad_end - padded
    experts = jnp.arange(N_EXPERTS, dtype=jnp.int32)
    first = jnp.sum(jnp.where(idx2[..., None] == experts, pad_start, 0), axis=-1)
    dest = (first + rank2).reshape(-1).astype(jnp.int32)
    block_start = jnp.arange(n_blocks, dtype=jnp.int32) * MOE_BLOCK
    block_expert = jnp.minimum(jnp.sum(block_start[:, None] >= pad_end[None, :], axis=1), N_EXPERTS - 1)
    n_used = (pad_end[-1] // MOE_BLOCK).reshape(1)
    return dest, block_expert.astype(jnp.int32), n_used.astype(jnp.int32), n_blocks * MOE_BLOCK


def kernel(x, c, ctx, c_ctx, w_mod, b_mod, norm_attn_g, norm_ffn_g, w_in, shift_mu, q_norm_g, w_uq, kv_norm_g, w_ukv, decay_w0, decay_up, iclr_a0, iclr_up, gate_up, key_k, key_a, bonus_r_k, lnx_g, lnx_b, w_out, w_grp, b_grp, w_exp, b_exp, w1, w3, w2, final_norm_g):
    n_batch, t_len, _ = x.shape
    assert ctx.shape[1] == CTX_LEN == TM and t_len % TM == 0 and w_mod.shape[0] == 1
    tpb = (CTX_LEN + t_len) // TM
    cpb = (CTX_LEN + t_len) // CHUNK
    n = n_batch * t_len

    c_rows = jnp.zeros((8, D_MODEL), F32).at[:n_batch].set(c).at[n_batch].set(c_ctx)
    mod_tab = _mod_call(c_rows, w_mod[0], b_mod[0]).reshape(8 * 6, 1, D_MODEL)

    wi = w_in[0]
    w_kr = wi[:, 768:MLA_IN]
    o = MLA_IN
    w_in_p = jnp.concatenate([
        wi[:, 0:768], _pad_cols(w_kr, LANES), _pad_cols(w_kr[:, _ROPE_SWAP], LANES),
        wi[:, o:o + COLS_RKV],
        _pad_cols(wi[:, o + COLS_RKV:o + COLS_RKV + DECAY_LORA], LANES),
        _pad_cols(wi[:, o + COLS_RKV + DECAY_LORA:o + COLS_RKV + DECAY_LORA + ICLR_LORA], LANES),
        _pad_cols(wi[:, o + COLS_RKV + DECAY_LORA + ICLR_LORA:], 2 * LANES),
    ], axis=1).astype(BF16)
    p_mla, p_rkv, p_lora = _project_call(x, ctx, mod_tab, norm_attn_g[0], w_in_p, tpb)

    mu = shift_mu[0]
    mu_rkv = mu[:, 0:COLS_RKV]
    mu_lora = jnp.concatenate([
        _pad_cols(mu[:, COLS_RKV:COLS_RKV + DECAY_LORA], LANES),
        _pad_cols(mu[:, COLS_RKV + DECAY_LORA:COLS_RKV + DECAY_LORA + ICLR_LORA], LANES),
        _pad_cols(mu[:, COLS_RKV + DECAY_LORA + ICLR_LORA:], 2 * LANES)], axis=1)

    def lora_up(w):
        both = jnp.concatenate([w[0], w[1]], axis=1)
        return jnp.pad(both, ((0, LANES - both.shape[0]), (0, 0))).astype(BF16)

    gup = jnp.pad(gate_up[0], ((0, 2 * LANES - GATE_LORA), (0, 0))).astype(BF16)
    r, k, v, lw, lr, g = _prep_call(
        p_rkv, p_lora, mu_rkv, mu_lora, lora_up(decay_up[0]), lora_up(iclr_up[0]), gup,
        decay_w0[0].reshape(1, -1), iclr_a0[0].reshape(1, -1), tpb)
    key_k2 = key_k[0].reshape(1, -1)
    key_a2 = key_a[0].reshape(1, -1)
    yscan = _scan_call(r, k, v, lw, lr, key_k2, key_a2, n_batch, cpb, CTX_LEN // CHUNK)

    hd = QK_NOPE_DIM + QK_ROPE_DIM
    wq = w_uq[0].reshape(Q_LORA_RANK, MLA_HEADS, hd)
    zq = jnp.zeros((Q_LORA_RANK, MLA_HEADS, QK_ROPE_DIM), F32)
    wa = jnp.concatenate([wq, zq], axis=2).reshape(Q_LORA_RANK, -1).astype(BF16)
    wb = jnp.concatenate([jnp.zeros((Q_LORA_RANK, MLA_HEADS, QK_NOPE_DIM), F32),
                          wq[:, :, QK_NOPE_DIM:][:, :, _ROPE_SWAP], zq], axis=2
                         ).reshape(Q_LORA_RANK, -1).astype(BF16)
    wkv3 = w_ukv[0].reshape(KV_LORA_RANK, MLA_HEADS, QK_NOPE_DIM + V_HEAD_DIM)
    wkv = jnp.concatenate([wkv3[:, :, :QK_NOPE_DIM].reshape(KV_LORA_RANK, -1),
                           wkv3[:, :, QK_NOPE_DIM:].reshape(KV_LORA_RANK, -1)], axis=1).astype(BF16)
    q, kmat, vmat = _mla_prep_call(p_mla, _rope_tables(t_len), q_norm_g[0], kv_norm_g[0], wa, wb, wkv, tpb)
    attn = _attn_call(q, kmat, vmat, n_batch, t_len, tpb)

    w_router = _pad_cols(jnp.concatenate([w_grp[0], w_exp[0]], axis=1), ROUTER_COLS)
    w_router_hi = w_router.astype(BF16)
    w_router2 = jnp.concatenate([w_router_hi, (w_router - w_router_hi.astype(F32)).astype(BF16)], axis=1)
    b_router = _pad_cols(jnp.concatenate([b_grp[0], b_exp[0]]).reshape(1, -1), ROUTER_COLS)
    x1, h2, idx, gates, rank, counts = _mix_call(
        x, attn, yscan, r, k, v, lr, g, mod_tab, key_a2, bonus_r_k[0].reshape(1, -1),
        lnx_g[0].reshape(1, -1), lnx_b[0].reshape(1, -1), norm_ffn_g[0],
        w_out[0].astype(BF16), w_router2, b_router, tpb)

    dest, block_expert, n_used, n_slots = _slot_tables(
        idx[:, :TOP_K], rank[:, :TOP_K], counts[0, :N_EXPERTS].astype(jnp.int32), n)
    xs = _scatter_call(dest, h2, n_slots)
    ys = _moe_call(block_expert, n_used, xs, w1[0], w3[0], w2[0])
    out = _final_call(dest, x1, ys, gates, mod_tab, final_norm_g, t_len)
    return out.reshape(n_batch, t_len, D_MODEL)
```

```python
import functools
import math

import jax
import jax.numpy as jnp
import numpy as np
from jax import lax
from jax.experimental import pallas as pl
from jax.experimental.pallas import tpu as pltpu

F32 = jnp.float32
BF16 = jnp.bfloat16
HIGHEST = lax.Precision.HIGHEST

D_MODEL = 2048
CTX_LEN = 256
GRID_W = 64
NORM_EPS = 1e-6

MLA_HEADS = 8
QK_NOPE_DIM = 128
QK_ROPE_DIM = 64
V_HEAD_DIM = 128
Q_LORA_RANK = 512
KV_LORA_RANK = 256
MLA_WIDTH = MLA_HEADS * V_HEAD_DIM
MLA_SCALE = (QK_NOPE_DIM + QK_ROPE_DIM) ** -0.5
ROPE_THETA = 10000.0
ROPE_AXIS_DIM = QK_ROPE_DIM // 2
QK_PAD_DIM = 256
VT_ROWS = V_HEAD_DIM + 16

RWKV_HEAD_DIM = 64
RWKV_WIDTH = D_MODEL - MLA_WIDTH
RWKV_HEADS = RWKV_WIDTH // RWKV_HEAD_DIM
DECAY_LORA = 64
ICLR_LORA = 64
GATE_LORA = 160
LNX_EPS = 64e-5

N_GROUPS = 4
EXPERTS_PER_GROUP = 8
N_EXPERTS = N_GROUPS * EXPERTS_PER_GROUP
TOP_K = 2
D_EXPERT = 512
MOE_BLOCK = 256

MLA_IN = Q_LORA_RANK + KV_LORA_RANK + QK_ROPE_DIM
LANES = 128
TM = 256
CHUNK = 64
PAIR = 2 * RWKV_HEAD_DIM
N_PAIRS = RWKV_WIDTH // PAIR
HEAD_GROUP = 256
PAIRS_PER_STEP = 8
SCAN_SUB = 4
ATTN_TQ = 512
ATTN_TK = 768
MIX_SUB = 2
DMA_UNROLL = 8
VMEM_LIMIT = 56 * 1024 * 1024

COLS_MLA = 1024
COLS_RKV = 3 * RWKV_WIDTH
COLS_LORA = 512
COLS_IN = COLS_MLA + COLS_RKV + COLS_LORA
ROUTER_COLS = 128


def _cparams(sem):
    return pltpu.CompilerParams(dimension_semantics=sem, vmem_limit_bytes=VMEM_LIMIT)


def _resident(shape, index_map):
    return pl.BlockSpec(shape, index_map, pipeline_mode=pl.Buffered(1))


def _dot(a, b):
    return jnp.dot(a, b, preferred_element_type=F32)


def _dot_nt(a, b):
    return lax.dot_general(a, b, (((1,), (1,)), ((), ())), preferred_element_type=F32)


def _dot_tn(a, b):
    return lax.dot_general(a, b, (((0,), (0,)), ((), ())), preferred_element_type=F32)


def _split2(x):
    hi = x.astype(BF16)
    lo = (x - hi.astype(F32)).astype(BF16)
    return hi, lo


def _head_ones():
    row = lax.broadcasted_iota(jnp.int32, (HEAD_GROUP, HEAD_GROUP), 0)
    col = lax.broadcasted_iota(jnp.int32, (HEAD_GROUP, HEAD_GROUP), 1)
    return jnp.where(row // RWKV_HEAD_DIM == col // RWKV_HEAD_DIM, 1.0, 0.0).astype(BF16)


def _head_sum(x, ones_bd):
    rows = x.shape[0]
    n = x.shape[1] // HEAD_GROUP
    parts = [half[:, c * HEAD_GROUP:(c + 1) * HEAD_GROUP] for half in _split2(x) for c in range(n)]
    res = _dot(jnp.concatenate(parts, axis=0), ones_bd)
    return jnp.concatenate([res[c * rows:(c + 1) * rows] + res[(n + c) * rows:(n + c + 1) * rows]
                            for c in range(n)], axis=1)


def _split3(x):
    hi = x.astype(BF16)
    r1 = x - hi.astype(F32)
    mid = r1.astype(BF16)
    lo = (r1 - mid.astype(F32)).astype(BF16)
    return hi, mid, lo


def _mod_kernel(c_ref, w_ref, b_ref, o_ref):
    c = c_ref[...]
    s = c * jax.nn.sigmoid(c)
    o_ref[...] = jnp.dot(s, w_ref[...], preferred_element_type=F32, precision=HIGHEST) + b_ref[...]


def _mod_call(c_rows, w_mod, b_mod):
    n = w_mod.shape[1]
    tn = 1024
    return pl.pallas_call(
        _mod_kernel,
        grid=(n // tn,),
        in_specs=[
            pl.BlockSpec((8, D_MODEL), lambda i: (0, 0)),
            pl.BlockSpec((D_MODEL, tn), lambda i: (0, i)),
            pl.BlockSpec((1, tn), lambda i: (0, i)),
        ],
        out_specs=pl.BlockSpec((8, tn), lambda i: (0, i)),
        out_shape=jax.ShapeDtypeStruct((8, n), F32),
        compiler_params=_cparams(("arbitrary",)),
        name="mod",
    )(c_rows, w_mod, b_mod.reshape(1, n))


def _project_kernel(tpb, x_ref, ctx_ref, sh_ref, sc_ref, g_ref, w_ref, o_mla, o_rkv, o_lora):
    is_ctx = (pl.program_id(0) % tpb) == 0
    xin = jnp.where(is_ctx, ctx_ref[...], x_ref[...])
    ms = jnp.mean(xin * xin, axis=-1, keepdims=True)
    h = xin * lax.rsqrt(ms + NORM_EPS) * g_ref[...]
    hb = (h * (1.0 + sc_ref[...]) + sh_ref[...]).astype(BF16)
    o_mla[...] = _dot(hb, w_ref[:, 0:COLS_MLA])
    o_rkv[...] = _dot(hb, w_ref[:, COLS_MLA:COLS_MLA + COLS_RKV])
    o_lora[...] = _dot(hb, w_ref[:, COLS_MLA + COLS_RKV:COLS_IN])


def _mod_row(i, tpb, n_batch):
    return jnp.where(i % tpb == 0, n_batch, i // tpb)


def _project_call(x, ctx, mod_tab, norm_g, w_in_p, tpb):
    n_batch, t_len, _ = x.shape
    nt = n_batch * tpb * TM

    def mod_spec(k):
        return pl.BlockSpec((None, 1, D_MODEL), lambda i: (_mod_row(i, tpb, n_batch) * 6 + k, 0, 0))

    return pl.pallas_call(
        functools.partial(_project_kernel, tpb),
        grid=(n_batch * tpb,),
        in_specs=[
            pl.BlockSpec((None, TM, D_MODEL), lambda i: (i // tpb, jnp.maximum(i % tpb - 1, 0), 0)),
            pl.BlockSpec((None, TM, D_MODEL), lambda i: (i // tpb, 0, 0)),
            mod_spec(0),
            mod_spec(1),
            _resident((1, D_MODEL), lambda i: (0, 0)),
            _resident((D_MODEL, COLS_IN), lambda i: (0, 0)),
        ],
        out_specs=[
            pl.BlockSpec((TM, COLS_MLA), lambda i: (i, 0)),
            pl.BlockSpec((TM, COLS_RKV), lambda i: (i, 0)),
            pl.BlockSpec((TM, COLS_LORA), lambda i: (i, 0)),
        ],
        out_shape=[
            jax.ShapeDtypeStruct((nt, COLS_MLA), F32),
            jax.ShapeDtypeStruct((nt, COLS_RKV), F32),
            jax.ShapeDtypeStruct((nt, COLS_LORA), F32),
        ],
        compiler_params=_cparams(("arbitrary",)),
        name="project",
    )(x, ctx, mod_tab, mod_tab, norm_g.reshape(1, D_MODEL), w_in_p)


def _prep_kernel(tpb, p_ref, pp_ref, pn_ref, l_ref, lp_ref, ln_ref, mu_ref, mul_ref,
                 wup_ref, aup_ref, gup_ref, w0_ref, a0_ref,
                 r_o, k_o, v_o, lw_o, a_o, g_o):
    j = pl.program_id(0) % tpb
    no_prev = j <= 1
    no_next = (j == 0) | (j == tpb - 1)

    def shifted(main, prev_blk, next_blk, mu):
        rows = lax.broadcasted_iota(jnp.int32, main.shape, 0)
        prow = jnp.where(no_prev, 0.0, prev_blk[7:8, :])
        nrow = jnp.where(no_next, 0.0, next_blk[0:1, :])
        prev = jnp.where(rows == 0, prow, pltpu.roll(main, 1, 0))
        nxt = jnp.where(rows == TM - 1, nrow, pltpu.roll(main, TM - 1, 0))
        return main + mu[0:1, :] * (prev - main) + mu[1:2, :] * (nxt - main)

    for c, out in enumerate((r_o, k_o, v_o)):
        sl = slice(c * RWKV_WIDTH, (c + 1) * RWKV_WIDTH)
        out[...] = shifted(p_ref[:, sl], pp_ref[:, sl], pn_ref[:, sl], mu_ref[:, sl])

    lo = shifted(l_ref[...], lp_ref[...], ln_ref[...], mul_ref[...])
    wl = jnp.tanh(lo[:, 0:LANES]).astype(BF16)
    al = lo[:, LANES:2 * LANES].astype(BF16)
    gl = jax.nn.sigmoid(lo[:, 2 * LANES:4 * LANES]).astype(BF16)
    w_raw = w0_ref[...] + _dot(wl, wup_ref[...])
    lw_o[...] = -math.exp(-0.5) * jax.nn.sigmoid(w_raw)
    a_o[...] = jax.nn.sigmoid(a0_ref[...] + _dot(al, aup_ref[...]))
    g_o[...] = _dot(gl, gup_ref[...])


def _prep_call(p_rkv, p_lora, mu_rkv, mu_lora, wup, aup, gup, w0, a0, tpb):
    nt = p_rkv.shape[0]
    last8 = nt // 8 - 1
    sub = TM // 8

    def halo(cols):
        return [
            pl.BlockSpec((TM, cols), lambda i: (i, 0)),
            pl.BlockSpec((8, cols), lambda i: (jnp.maximum(i * sub - 1, 0), 0)),
            pl.BlockSpec((8, cols), lambda i: (jnp.minimum((i + 1) * sub, last8), 0)),
        ]

    w2 = 2 * RWKV_WIDTH
    return pl.pallas_call(
        functools.partial(_prep_kernel, tpb),
        grid=(nt // TM,),
        in_specs=halo(COLS_RKV) + halo(COLS_LORA) + [
            _resident((2, COLS_RKV), lambda i: (0, 0)),
            _resident((2, COLS_LORA), lambda i: (0, 0)),
            _resident((LANES, w2), lambda i: (0, 0)),
            _resident((LANES, w2), lambda i: (0, 0)),
            _resident((2 * LANES, RWKV_WIDTH), lambda i: (0, 0)),
            _resident((1, w2), lambda i: (0, 0)),
            _resident((1, w2), lambda i: (0, 0)),
        ],
        out_specs=[
            pl.BlockSpec((TM, RWKV_WIDTH), lambda i: (i, 0)),
            pl.BlockSpec((TM, RWKV_WIDTH), lambda i: (i, 0)),
            pl.BlockSpec((TM, RWKV_WIDTH), lambda i: (i, 0)),
            pl.BlockSpec((TM, w2), lambda i: (i, 0)),
            pl.BlockSpec((TM, w2), lambda i: (i, 0)),
            pl.BlockSpec((TM, RWKV_WIDTH), lambda i: (i, 0)),
        ],
        out_shape=[
            jax.ShapeDtypeStruct((nt, RWKV_WIDTH), F32),
            jax.ShapeDtypeStruct((nt, RWKV_WIDTH), F32),
            jax.ShapeDtypeStruct((nt, RWKV_WIDTH), F32),
            jax.ShapeDtypeStruct((nt, w2), F32),
            jax.ShapeDtypeStruct((nt, w2), F32),
            jax.ShapeDtypeStruct((nt, RWKV_WIDTH), F32),
        ],
        compiler_params=_cparams(("arbitrary",)),
        name="prep",
    )(p_rkv, p_rkv, p_rkv, p_lora, p_lora, p_lora, mu_rkv, mu_lora, wup, aup, gup, w0, a0)


def _stack_heads(x):
    lane = lax.broadcasted_iota(jnp.int32, x.shape, 1)
    zero = jnp.zeros_like(x)
    return jnp.concatenate([jnp.where(lane < RWKV_HEAD_DIM, x, zero),
                            jnp.where(lane >= RWKV_HEAD_DIM, x, zero)], axis=0)


def _unstack_heads(z):
    half = z.shape[0] // 2
    return z[:half] + z[half:]


def _scan_kernel(rev, r_ref, k_ref, v_ref, lw_ref, a_ref, kkey_ref, akey_ref, y_ref, s_scr):
    @pl.when(pl.program_id(2) == 0)
    def _():
        s_scr[...] = jnp.zeros_like(s_scr)

    c2 = 2 * CHUNK
    rows = SCAN_SUB * CHUNK
    tb = lax.broadcasted_iota(jnp.int32, (rows, rows), 0)
    ib = lax.broadcasted_iota(jnp.int32, (rows, rows), 1)
    upto = (ib >= tb) if rev else (ib <= tb)
    cum_mat = jnp.where((tb // CHUNK == ib // CHUNK) & upto, 1.0, 0.0).astype(BF16)

    row = lax.broadcasted_iota(jnp.int32, (c2, LANES), 0)
    col = lax.broadcasted_iota(jnp.int32, (c2, LANES), 1)
    t_idx = row % CHUNK
    i_idx = col % CHUNK
    before = (i_idx > t_idx) if rev else (i_idx < t_idx)
    keep = before | ((i_idx == t_idx) & (row >= CHUNK))
    same_head = (row // RWKV_HEAD_DIM) == (col // RWKV_HEAD_DIM)
    eye_f = jnp.where(row == col, 1.0, 0.0)

    zero = jnp.zeros((c2, LANES), F32)
    pairs = range(PAIRS_PER_STEP)
    units = [(c, p) for c in range(SCAN_SUB) for p in pairs]

    def unit(x, u):
        c, p = u
        return x[c * CHUNK:(c + 1) * CHUNK, p * PAIR:(p + 1) * PAIR]

    r = r_ref[...]
    k = k_ref[...]
    lw = lw_ref[...]
    lr = a_ref[...]
    vb = v_ref[...].astype(BF16)
    kraw = k * kkey_ref[...]
    w_hi, w_mid, w_lo = _split3(lw)
    lp = _dot(cum_mat, w_hi) + _dot(cum_mat, w_mid) + _dot(cum_mat, w_lo)
    kk = kraw * lax.rsqrt(_head_sum(kraw * kraw, _head_ones()) + 1e-12)
    b = kk * lr
    kd = k * (1.0 + (lr - 1.0) * akey_ref[...])
    last = 0 if rev else CHUNK - 1
    ltot_rows = [lp[c * CHUNK + last:c * CHUNK + last + 1, :] for c in range(SCAN_SUB)]
    ltot = jnp.concatenate([jnp.broadcast_to(t, (CHUNK, t.shape[1])) for t in ltot_rows], axis=0)
    e_neg = jnp.exp(-lp)
    e_rest = jnp.exp(ltot - lp)
    e_tot = [jnp.exp(t) for t in ltot_rows]
    at = -kk * jnp.exp(lp - lw)
    rt = r * jnp.exp(lp)
    at_b = at.astype(BF16)
    rt_b = rt.astype(BF16)
    bt_b = (b * e_neg).astype(BF16)
    kt_b = (kd * e_neg).astype(BF16)
    bh = (b * e_rest).astype(BF16)
    kh = (kd * e_rest).astype(BF16)

    sv = {u: _stack_heads(unit(vb, u)) for u in units}
    ar = {u: jnp.concatenate([unit(at_b, u), unit(rt_b, u)], axis=0) for u in units}
    ab = {u: jnp.where(keep, _dot_nt(ar[u], _stack_heads(unit(bt_b, u))), zero) for u in units}
    ak = {u: jnp.where(keep, _dot_nt(ar[u], _stack_heads(unit(kt_b, u))), zero) for u in units}
    a_rb = {u: ab[u][CHUNK:].astype(BF16) for u in units}
    akv = {u: _dot(ak[u].astype(BF16), sv[u]) for u in units}

    pw = {u: _stack_heads(ab[u][:CHUNK]) for u in units}
    tm = {u: eye_f + pw[u] for u in units}
    pw = {u: _dot(pw[u].astype(BF16), pw[u].astype(BF16)) for u in units}
    for _ in range(int(math.log2(CHUNK)) - 2):
        both = {u: _dot(jnp.concatenate([tm[u], pw[u]], axis=0).astype(BF16), pw[u].astype(BF16)) for u in units}
        tm = {u: tm[u] + both[u][:c2] for u in units}
        pw = {u: both[u][c2:] for u in units}
    tm = {u: tm[u] + _dot(tm[u].astype(BF16), pw[u].astype(BF16)) for u in units}
    t_p = {u: _unstack_heads(tm[u]).astype(BF16) for u in units}

    wg = {u: _dot(t_p[u], jnp.concatenate([_stack_heads(akv[u][:CHUNK].astype(BF16)),
                                           _stack_heads(unit(at_b, u))], axis=1)) for u in units}
    w_b = {u: wg[u][:, :LANES].astype(BF16) for u in units}
    g_b = {u: wg[u][:, LANES:].astype(BF16) for u in units}
    qz = {u: _dot(a_rb[u], jnp.concatenate([_stack_heads(g_b[u]), _stack_heads(w_b[u])], axis=1)) for u in units}
    gz = jnp.zeros((CHUNK, LANES), BF16)
    mn = {u: _dot_tn(jnp.concatenate([jnp.concatenate([w_b[u], g_b[u]], axis=1),
                                      jnp.concatenate([unit(vb, u), gz], axis=1)], axis=0),
                     jnp.concatenate([unit(bh, u), unit(kh, u)], axis=0)) for u in units}
    n_st = {u: jnp.where(same_head, mn[u][:c2], zero) for u in units}
    m_bd = {u: jnp.where(same_head, mn[u][c2:], zero).astype(BF16) for u in units}
    q_b = {u: (unit(rt, u) + qz[u][:, :LANES]).astype(BF16) for u in units}
    z = {u: qz[u][:, LANES:] + akv[u][CHUNK:] for u in units}

    state = [s_scr[p] for p in pairs]
    for c in (reversed(range(SCAN_SUB)) if rev else range(SCAN_SUB)):
        s_b = [state[p].astype(BF16) for p in pairs]
        for p in pairs:
            y_ref[c * CHUNK:(c + 1) * CHUNK, p * PAIR:(p + 1) * PAIR] = (
                _dot_nt(q_b[(c, p)], _stack_heads(s_b[p])) + z[(c, p)])
        state = [state[p] * e_tot[c][:, p * PAIR:(p + 1) * PAIR] + _dot(s_b[p], m_bd[(c, p)])
                 + _unstack_heads(n_st[(c, p)]) for p in pairs]
    for p in pairs:
        s_scr[p] = state[p]


def _scan_call(rev, r, k, v, lw, lr, key_k, key_a, n_batch, bpb, ctx_blocks):
    nt = r.shape[0]
    groups = N_PAIRS // PAIRS_PER_STEP
    gw = PAIRS_PER_STEP * PAIR
    rows = SCAN_SUB * CHUNK
    d = 1 if rev else 0

    def block_row(b, j):
        if rev:
            j = jnp.where(j < ctx_blocks, ctx_blocks - 1 - j, bpb + ctx_blocks - 1 - j)
        return b * bpb + j

    shared = pl.BlockSpec((rows, gw), lambda b, g, j: (block_row(b, j), g))
    per_dir = pl.BlockSpec((rows, gw), lambda b, g, j: (block_row(b, j), d * groups + g))
    keys = pl.BlockSpec((1, gw), lambda b, g, j: (0, g))
    return pl.pallas_call(
        functools.partial(_scan_kernel, rev),
        grid=(n_batch, groups, bpb),
        in_specs=[shared, shared, shared, per_dir, per_dir, keys, keys],
        out_specs=shared,
        out_shape=jax.ShapeDtypeStruct((nt, RWKV_WIDTH), F32),
        scratch_shapes=[pltpu.VMEM((PAIRS_PER_STEP, RWKV_HEAD_DIM, PAIR), F32)],
        compiler_params=_cparams(("arbitrary", "arbitrary", "arbitrary")),
        name="scan_bwd" if rev else "scan_fwd",
    )(r, k, v, lw, lr, key_k, key_a)


def _mla_prep_kernel(p_ref, ck_ref, sk_ref, cq_ref, sq_ref, qg_ref, kvg_ref, wa_ref, wb_ref, wkv_ref, wvt_ref,
                     q_o, k_o, v_o):
    cq = p_ref[:, 0:Q_LORA_RANK]
    cqn = (cq * lax.rsqrt(jnp.mean(cq * cq, axis=-1, keepdims=True) + NORM_EPS) * qg_ref[...]).astype(BF16)
    ckv = p_ref[:, Q_LORA_RANK:Q_LORA_RANK + KV_LORA_RANK]
    ckvn = (ckv * lax.rsqrt(jnp.mean(ckv * ckv, axis=-1, keepdims=True) + NORM_EPS) * kvg_ref[...]).astype(BF16)
    kr_a = p_ref[:, 768:896]
    kr_b = p_ref[:, 896:1024]
    k_rot = (kr_a * ck_ref[...] + kr_b * sk_ref[...]).astype(BF16)
    cos_q = cq_ref[...]
    sin_q = sq_ref[...]
    for h in range(MLA_HEADS):
        hs = slice(h * QK_PAD_DIM, (h + 1) * QK_PAD_DIM)
        q_o[:, hs] = (_dot(cqn, wa_ref[:, hs]) * cos_q + _dot(cqn, wb_ref[:, hs]) * sin_q).astype(BF16)
        k_o[:, h * QK_PAD_DIM:h * QK_PAD_DIM + QK_NOPE_DIM] = _dot(
            ckvn, wkv_ref[:, h * QK_NOPE_DIM:(h + 1) * QK_NOPE_DIM]).astype(BF16)
        k_o[:, h * QK_PAD_DIM + QK_NOPE_DIM:(h + 1) * QK_PAD_DIM] = k_rot
    v_t = _dot_nt(wvt_ref[...], ckvn)
    for h in range(MLA_HEADS):
        v_o[h, 0:V_HEAD_DIM, :] = v_t[h * V_HEAD_DIM:(h + 1) * V_HEAD_DIM, :].astype(BF16)
        v_o[h, V_HEAD_DIM:VT_ROWS, :] = jnp.ones((VT_ROWS - V_HEAD_DIM, TM), BF16)


def _mla_prep_call(p_mla, tabs, q_norm_g, kv_norm_g, wa, wb, wk, wvt, n_batch, tpb):
    nt = p_mla.shape[0]
    ck, sk, cq, sq = tabs
    qw = MLA_HEADS * QK_PAD_DIM
    return pl.pallas_call(
        _mla_prep_kernel,
        grid=(nt // TM,),
        in_specs=[
            pl.BlockSpec((TM, COLS_MLA), lambda i: (i, 0)),
            pl.BlockSpec((TM, LANES), lambda i: (i % tpb, 0)),
            pl.BlockSpec((TM, LANES), lambda i: (i % tpb, 0)),
            pl.BlockSpec((TM, QK_PAD_DIM), lambda i: (i % tpb, 0)),
            pl.BlockSpec((TM, QK_PAD_DIM), lambda i: (i % tpb, 0)),
            _resident((1, Q_LORA_RANK), lambda i: (0, 0)),
            _resident((1, KV_LORA_RANK), lambda i: (0, 0)),
            _resident((Q_LORA_RANK, qw), lambda i: (0, 0)),
            _resident((Q_LORA_RANK, qw), lambda i: (0, 0)),
            _resident((KV_LORA_RANK, MLA_WIDTH), lambda i: (0, 0)),
            _resident((MLA_WIDTH, KV_LORA_RANK), lambda i: (0, 0)),
        ],
        out_specs=[
            pl.BlockSpec((TM, qw), lambda i: (i, 0)),
            pl.BlockSpec((TM, qw), lambda i: (i, 0)),
            pl.BlockSpec((None, MLA_HEADS, VT_ROWS, TM), lambda i: (i // tpb, 0, 0, i % tpb)),
        ],
        out_shape=[
            jax.ShapeDtypeStruct((nt, qw), BF16),
            jax.ShapeDtypeStruct((nt, qw), BF16),
            jax.ShapeDtypeStruct((n_batch, MLA_HEADS, VT_ROWS, tpb * TM), BF16),
        ],
        compiler_params=_cparams(("arbitrary",)),
        name="mla_prep",
    )(p_mla, ck, sk, cq, sq, q_norm_g.reshape(1, -1), kv_norm_g.reshape(1, -1), wa, wb, wk, wvt)


def _attn_kernel(n_kv, *refs):
    q_refs, (k_ref, v_ref, o_ref) = refs[:-3], refs[-3:]
    qs = [q_ref[...] for q_ref in q_refs]
    chains = range(len(qs))

    m = [jnp.full((1, TM), -jnp.inf, F32) for _ in chains]
    acc = [jnp.zeros((VT_ROWS, TM), F32) for _ in chains]
    def scores(j):
        kj = k_ref[j * ATTN_TK:(j + 1) * ATTN_TK, :]
        return [_dot_nt(kj, qs[c]) for c in chains]

    s_next = scores(0)
    for j in range(n_kv):
        vj = v_ref[:, j * ATTN_TK:(j + 1) * ATTN_TK]
        s = s_next
        if j + 1 < n_kv:
            s_next = scores(j + 1)
        for c in chains:
            m_new = jnp.maximum(m[c], jnp.max(s[c], axis=0, keepdims=True))
            alpha = jnp.exp2(m[c] - m_new)
            p = jnp.exp2((s[c] - m_new).astype(BF16))
            acc[c] = alpha * acc[c] + _dot(vj, p)
            m[c] = m_new
    for c in chains:
        out = acc[c][0:V_HEAD_DIM, :] / acc[c][V_HEAD_DIM:V_HEAD_DIM + 1, :]
        o_ref[c * TM:(c + 1) * TM, :] = jnp.transpose(out).astype(o_ref.dtype)


def _attn_call(q, k, v, n_batch, t_len, tpb):
    rows_b = tpb * TM
    assert rows_b % ATTN_TK == 0 and t_len % ATTN_TQ == 0 and ATTN_TQ % TM == 0
    n_q = t_len // ATTN_TQ
    sub = ATTN_TQ // TM
    k3 = k.reshape(n_batch, rows_b, MLA_HEADS * QK_PAD_DIM)

    def q_spec(u):
        return pl.BlockSpec((TM, QK_PAD_DIM), lambda b, h, i: (b * tpb + 1 + i * sub + u, h))

    return pl.pallas_call(
        functools.partial(_attn_kernel, rows_b // ATTN_TK),
        grid=(n_batch, MLA_HEADS, n_q),
        in_specs=[q_spec(u) for u in range(sub)] + [
            pl.BlockSpec((None, rows_b, QK_PAD_DIM), lambda b, h, i: (b, 0, h)),
            pl.BlockSpec((None, None, VT_ROWS, rows_b), lambda b, h, i: (b, h, 0, 0)),
        ],
        out_specs=pl.BlockSpec((ATTN_TQ, V_HEAD_DIM), lambda b, h, i: (b * n_q + i, h)),
        out_shape=jax.ShapeDtypeStruct((n_batch * t_len, MLA_WIDTH), BF16),
        compiler_params=_cparams(("arbitrary", "arbitrary", "arbitrary")),
        name="attention",
    )(*([q] * sub), k3, v)


def _slot_rank(idx, run_ref):
    lane = lax.broadcasted_iota(jnp.int32, idx.shape, 1)
    oh0 = lane == idx[:, 0:1]
    oh1 = lane == idx[:, 1:2]
    both = jnp.where(oh0 | oh1, 1.0, 0.0)
    t_row = lax.broadcasted_iota(jnp.int32, (TM, TM), 0)
    t_col = lax.broadcasted_iota(jnp.int32, (TM, TM), 1)
    earlier = jnp.where(t_col < t_row, 1.0, 0.0).astype(BF16)
    seen = _dot(earlier, both.astype(BF16)) + run_ref[...]
    r0 = jnp.sum(jnp.where(oh0, seen, 0.0), axis=-1, keepdims=True)
    r1 = jnp.sum(jnp.where(oh1, seen, 0.0), axis=-1, keepdims=True)
    run_ref[...] = run_ref[...] + jnp.sum(both, axis=0, keepdims=True)
    return jnp.where(lane == 0, r0, jnp.where(lane == 1, r1, 0.0)).astype(jnp.int32)


def _route(logits):
    lane = lax.broadcasted_iota(jnp.int32, logits.shape, 1)
    neg = jnp.full_like(logits, -jnp.inf)
    big = jnp.full_like(lane, 2 ** 30)
    is_grp = lane < N_GROUPS
    gl = jnp.where(is_grp, logits, neg)
    ge = jnp.exp(gl - jnp.max(gl, axis=-1, keepdims=True))
    gp = ge / jnp.sum(ge, axis=-1, keepdims=True)
    g_val = jnp.max(gp, axis=-1, keepdims=True)
    g_idx = jnp.min(jnp.where(is_grp & (gp == g_val), lane, big), axis=-1, keepdims=True)
    e_lane = lane - N_GROUPS
    in_grp = (e_lane >= g_idx * EXPERTS_PER_GROUP) & (e_lane < (g_idx + 1) * EXPERTS_PER_GROUP)
    el = jnp.where(in_grp, logits, neg)
    ee = jnp.exp(el - jnp.max(el, axis=-1, keepdims=True))
    ep = ee / jnp.sum(ee, axis=-1, keepdims=True)
    v1 = jnp.max(ep, axis=-1, keepdims=True)
    i1 = jnp.min(jnp.where(in_grp & (ep == v1), lane, big), axis=-1, keepdims=True)
    rest = in_grp & (lane != i1)
    v2 = jnp.max(jnp.where(rest, ep, neg), axis=-1, keepdims=True)
    i2 = jnp.min(jnp.where(rest & (ep == v2), lane, big), axis=-1, keepdims=True)
    denom = v1 + v2
    idx = jnp.where(lane == 0, i1 - N_GROUPS, jnp.where(lane == 1, i2 - N_GROUPS, 0))
    gate = jnp.where(lane == 0, g_val * v1 / denom, jnp.where(lane == 1, g_val * v2 / denom, 0.0))
    return idx, gate


def _mix_kernel(x_ref, attn_ref, yf_ref, yb_ref, r_ref, k_ref, v_ref, af_ref, ab_ref, g_ref,
                g1_ref, sh2_ref, sc2_ref, akey_ref, rk_ref, lng_ref, lnb_ref, ng_ref,
                wo_ref, wr_ref, br_ref,
                x1_o, h2_o, idx_o, gate_o, rank_o, cnt_o, run_scr):
    @pl.when(pl.program_id(0) == 0)
    def _():
        run_scr[...] = jnp.zeros_like(run_scr)

    subs = range(MIX_SUB)
    rows = TM // MIX_SUB

    def part(ref, u):
        return ref[u * rows:(u + 1) * rows, :]

    ones_bd = _head_ones()
    inv = 1.0 / RWKV_HEAD_DIM
    y = [part(yf_ref, u) + part(yb_ref, u) for u in subs]
    mu = [_head_sum(y[u], ones_bd) * inv for u in subs]
    dy = [y[u] - mu[u] for u in subs]
    var = [_head_sum(dy[u] * dy[u], ones_bd) * inv for u in subs]
    k_sum = [part(k_ref, u) * (2.0 + (part(af_ref, u) + part(ab_ref, u) - 2.0) * akey_ref[...]) for u in subs]
    bonus = [_head_sum(part(r_ref, u) * k_sum[u] * rk_ref[...], ones_bd) * part(v_ref, u) for u in subs]
    yn = [dy[u] * lax.rsqrt(var[u] + LNX_EPS) * lng_ref[...] + lnb_ref[...] for u in subs]
    rw = [((yn[u] + bonus[u]) * part(g_ref, u)).astype(BF16) for u in subs]
    o = [_dot(part(attn_ref, u), wo_ref[0:MLA_WIDTH, :]) + _dot(rw[u], wo_ref[MLA_WIDTH:D_MODEL, :])
         for u in subs]
    x1 = [part(x_ref, u) + g1_ref[...] * o[u] for u in subs]
    h2 = [x1[u] * lax.rsqrt(jnp.mean(x1[u] * x1[u], axis=-1, keepdims=True) + NORM_EPS) * ng_ref[...]
          * (1.0 + sc2_ref[...]) + sh2_ref[...] for u in subs]
    h_hl = [_split2(h2[u]) for u in subs]
    both = [_dot(h_hl[u][0], wr_ref[...]) for u in subs]
    lo_hi = [_dot(h_hl[u][1], wr_ref[:, 0:ROUTER_COLS]) for u in subs]
    routed = [_route(both[u][:, 0:ROUTER_COLS] + both[u][:, ROUTER_COLS:2 * ROUTER_COLS] + lo_hi[u] + br_ref[...])
              for u in subs]
    for u in subs:
        sl = slice(u * rows, (u + 1) * rows)
        x1_o[sl, :] = x1[u]
        h2_o[sl, :] = h2[u]
        idx_o[sl, :] = routed[u][0]
        gate_o[sl, :] = routed[u][1]
    rank_o[...] = _slot_rank(jnp.concatenate([routed[u][0] for u in subs], axis=0), run_scr)
    cnt_o[...] = run_scr[...]


def _mix_call(x, attn, yscan, r, k, v, lr, g, mod_tab, key_a, bonus_rk, lnx_g, lnx_b, norm_g,
              w_out_b, w_router, b_router, tpb):
    n_batch, t_len, _ = x.shape
    tpl = t_len // TM
    n = n_batch * t_len

    def lat(i):
        return (i // tpl) * tpb + 1 + i % tpl

    def tok(cols, col_blk=0):
        return pl.BlockSpec((TM, cols), lambda i: (lat(i), col_blk))

    def mod_spec(kk):
        return pl.BlockSpec((None, 1, D_MODEL), lambda i: ((i // tpl) * 6 + kk, 0, 0))

    def vec(cols):
        return _resident((1, cols), lambda i: (0, 0))

    tile_out = pl.BlockSpec((TM, ROUTER_COLS), lambda i: (i, 0))
    return pl.pallas_call(
        _mix_kernel,
        grid=(n // TM,),
        in_specs=[
            pl.BlockSpec((None, TM, D_MODEL), lambda i: (i // tpl, i % tpl, 0)),
            pl.BlockSpec((TM, MLA_WIDTH), lambda i: (i, 0)),
            tok(RWKV_WIDTH), tok(RWKV_WIDTH),
            tok(RWKV_WIDTH), tok(RWKV_WIDTH), tok(RWKV_WIDTH),
            tok(RWKV_WIDTH, 0), tok(RWKV_WIDTH, 1), tok(RWKV_WIDTH),
            mod_spec(2), mod_spec(3), mod_spec(4),
            vec(RWKV_WIDTH), vec(RWKV_WIDTH), vec(RWKV_WIDTH), vec(RWKV_WIDTH), vec(D_MODEL),
            _resident((D_MODEL, D_MODEL), lambda i: (0, 0)),
            _resident((D_MODEL, 2 * ROUTER_COLS), lambda i: (0, 0)),
            vec(ROUTER_COLS),
        ],
        out_specs=[
            pl.BlockSpec((TM, D_MODEL), lambda i: (i, 0)),
            pl.BlockSpec((TM, D_MODEL), lambda i: (i, 0)),
            tile_out, tile_out, tile_out,
            pl.BlockSpec((1, ROUTER_COLS), lambda i: (0, 0)),
        ],
        out_shape=[
            jax.ShapeDtypeStruct((n, D_MODEL), F32),
            jax.ShapeDtypeStruct((n, D_MODEL), F32),
            jax.ShapeDtypeStruct((n, ROUTER_COLS), jnp.int32),
            jax.ShapeDtypeStruct((n, ROUTER_COLS), F32),
            jax.ShapeDtypeStruct((n, ROUTER_COLS), jnp.int32),
            jax.ShapeDtypeStruct((1, ROUTER_COLS), F32),
        ],
        scratch_shapes=[pltpu.VMEM((1, ROUTER_COLS), F32)],
        compiler_params=_cparams(("arbitrary",)),
        name="mix",
    )(x, attn, yscan[0], yscan[1], r, k, v, lr, lr, g, mod_tab, mod_tab, mod_tab,
      key_a, bonus_rk, lnx_g, lnx_b, norm_g.reshape(1, D_MODEL), w_out_b, w_router, b_router)


def _scatter_kernel(dest_ref, h_ref, init_hbm, xs_hbm, sem):
    del init_hbm
    base = pl.program_id(0) * (TM * TOP_K)

    def row(t, slot):
        return pltpu.make_async_copy(h_ref.at[pl.ds(t, 1)], xs_hbm.at[pl.ds(slot, 1)], sem)

    def start(t, c):
        for kk in range(TOP_K):
            row(t, dest_ref[base + t * TOP_K + kk]).start()
        return c
    lax.fori_loop(0, TM, start, 0, unroll=DMA_UNROLL)

    def wait(t, c):
        for _ in range(TOP_K):
            row(t, 0).wait()
        return c
    lax.fori_loop(0, TM, wait, 0, unroll=DMA_UNROLL)


def _scatter_call(dest, h2, n_slots):
    n = h2.shape[0]
    return pl.pallas_call(
        _scatter_kernel,
        grid_spec=pltpu.PrefetchScalarGridSpec(
            num_scalar_prefetch=1,
            grid=(n // TM,),
            in_specs=[
                pl.BlockSpec((TM, D_MODEL), lambda i, dest: (i, 0)),
                pl.BlockSpec(memory_space=pl.ANY),
            ],
            out_specs=pl.BlockSpec(memory_space=pl.ANY),
            scratch_shapes=[pltpu.SemaphoreType.DMA],
        ),
        out_shape=jax.ShapeDtypeStruct((n_slots, D_MODEL), F32),
        input_output_aliases={2: 0},
        compiler_params=_cparams(("arbitrary",)),
        name="scatter",
    )(dest, h2, jnp.zeros((n_slots, D_MODEL), F32))


def _moe_kernel(be_ref, used_ref, x_ref, w1_ref, w3_ref, w2_ref, y_ref, w1b, w3b, w2b):
    i = pl.program_id(0)

    @pl.when(i < used_ref[0])
    def _():
        @pl.when((i == 0) | (be_ref[i] != be_ref[jnp.maximum(i - 1, 0)]))
        def _():
            w1b[...] = w1_ref[...].astype(BF16)
            w3b[...] = w3_ref[...].astype(BF16)
            w2b[...] = w2_ref[...].astype(BF16)

        x = x_ref[...].astype(BF16)
        a1 = _dot(x, w1b[...])
        a3 = _dot(x, w3b[...])
        hm = (a1 * jax.nn.sigmoid(a1) * a3).astype(BF16)
        y_ref[...] = _dot(hm, w2b[...])

    @pl.when(i >= used_ref[0])
    def _():
        y_ref[...] = jnp.zeros_like(y_ref)


def _moe_call(block_expert, n_used, xs, w1, w3, w2):
    n_blocks = block_expert.shape[0]

    def wspec(shape):
        return pl.BlockSpec((None,) + shape, lambda i, be, used: (be[i], 0, 0))

    return pl.pallas_call(
        _moe_kernel,
        grid_spec=pltpu.PrefetchScalarGridSpec(
            num_scalar_prefetch=2,
            grid=(n_blocks,),
            in_specs=[
                pl.BlockSpec((MOE_BLOCK, D_MODEL), lambda i, be, used: (jnp.minimum(i, used[0] - 1), 0)),
                wspec((D_MODEL, D_EXPERT)),
                wspec((D_MODEL, D_EXPERT)),
                wspec((D_EXPERT, D_MODEL)),
            ],
            out_specs=pl.BlockSpec((MOE_BLOCK, D_MODEL), lambda i, be, used: (i, 0)),
            scratch_shapes=[
                pltpu.VMEM((D_MODEL, D_EXPERT), BF16),
                pltpu.VMEM((D_MODEL, D_EXPERT), BF16),
                pltpu.VMEM((D_EXPERT, D_MODEL), BF16),
            ],
        ),
        out_shape=jax.ShapeDtypeStruct(xs.shape, F32),
        compiler_params=_cparams(("arbitrary",)),
        name="moe",
    )(block_expert, n_used, xs, w1, w3, w2)


def _final_kernel(dest_ref, x1_ref, gate_ref, g2_ref, ng_ref, ys_hbm, o_ref, ybuf, sem):
    base = pl.program_id(0) * (TM * TOP_K)

    def row(t, kk, slot):
        return pltpu.make_async_copy(ys_hbm.at[pl.ds(slot, 1)], ybuf.at[kk, pl.ds(t, 1)], sem)

    def start(t, c):
        for kk in range(TOP_K):
            row(t, kk, dest_ref[base + t * TOP_K + kk]).start()
        return c
    lax.fori_loop(0, TM, start, 0, unroll=DMA_UNROLL)

    def wait(t, c):
        for kk in range(TOP_K):
            row(t, kk, 0).wait()
        return c
    lax.fori_loop(0, TM, wait, 0, unroll=DMA_UNROLL)

    gate = gate_ref[...]
    y = ybuf[0] * gate[:, 0:1] + ybuf[1] * gate[:, 1:2]
    x2 = x1_ref[...] + g2_ref[...] * y
    o_ref[...] = x2 * lax.rsqrt(jnp.mean(x2 * x2, axis=-1, keepdims=True) + NORM_EPS) * ng_ref[...]


def _final_call(dest, x1, ys, gates, mod_tab, final_g, t_len):
    n = x1.shape[0]
    tpl = t_len // TM
    return pl.pallas_call(
        _final_kernel,
        grid_spec=pltpu.PrefetchScalarGridSpec(
            num_scalar_prefetch=1,
            grid=(n // TM,),
            in_specs=[
                pl.BlockSpec((TM, D_MODEL), lambda i, dest: (i, 0)),
                pl.BlockSpec((TM, ROUTER_COLS), lambda i, dest: (i, 0)),
                pl.BlockSpec((None, 1, D_MODEL), lambda i, dest: ((i // tpl) * 6 + 5, 0, 0)),
                _resident((1, D_MODEL), lambda i, dest: (0, 0)),
                pl.BlockSpec(memory_space=pl.ANY),
            ],
            out_specs=pl.BlockSpec((TM, D_MODEL), lambda i, dest: (i, 0)),
            scratch_shapes=[pltpu.VMEM((TOP_K, TM, D_MODEL), F32), pltpu.SemaphoreType.DMA],
        ),
        out_shape=jax.ShapeDtypeStruct((n, D_MODEL), F32),
        compiler_params=_cparams(("arbitrary",)),
        name="final",
    )(dest, x1, gates, mod_tab, final_g.reshape(1, D_MODEL), ys)


def _pad_cols(w, width):
    return jnp.pad(w, ((0, 0), (0, width - w.shape[1])))


_ROPE_SWAP = np.concatenate([np.arange(16, 32), np.arange(0, 16), np.arange(48, 64), np.arange(32, 48)])


def _rope_tables(t_len):
    pos = jnp.arange(t_len)
    inv_freq = ROPE_THETA ** (-jnp.arange(0, ROPE_AXIS_DIM, 2, dtype=F32) / ROPE_AXIS_DIM)
    ang_r = (pos // GRID_W)[:, None].astype(F32) * inv_freq
    ang_c = (pos % GRID_W)[:, None].astype(F32) * inv_freq
    cos = jnp.concatenate([jnp.cos(ang_r)] * 2 + [jnp.cos(ang_c)] * 2, axis=1)
    sin = jnp.concatenate([-jnp.sin(ang_r), jnp.sin(ang_r), -jnp.sin(ang_c), jnp.sin(ang_c)], axis=1)
    cos = jnp.concatenate([jnp.ones((CTX_LEN, QK_ROPE_DIM), F32), cos], axis=0)
    sin = jnp.concatenate([jnp.zeros((CTX_LEN, QK_ROPE_DIM), F32), sin], axis=0)
    rows = cos.shape[0]
    z64 = jnp.zeros((rows, QK_ROPE_DIM), F32)
    ck = jnp.concatenate([cos, z64], axis=1)
    sk = jnp.concatenate([sin, z64], axis=1)
    q_scale = MLA_SCALE * math.log2(math.e)
    cq = q_scale * jnp.concatenate([jnp.ones((rows, QK_NOPE_DIM), F32), cos, z64], axis=1)
    sq = q_scale * jnp.concatenate([jnp.zeros((rows, QK_NOPE_DIM), F32), sin, z64], axis=1)
    return ck, sk, cq, sq


def _slot_tables(idx2, rank2, counts, n_tokens):
    n_blocks = (n_tokens * TOP_K + N_EXPERTS * (MOE_BLOCK - 1) + MOE_BLOCK - 1) // MOE_BLOCK
    padded = (counts + MOE_BLOCK - 1) // MOE_BLOCK * MOE_BLOCK
    pad_end = jnp.cumsum(padded)
    pad_start = pad_end - padded
    experts = jnp.arange(N_EXPERTS, dtype=jnp.int32)
    first = jnp.sum(jnp.where(idx2[..., None] == experts, pad_start, 0), axis=-1)
    dest = (first + rank2).reshape(-1).astype(jnp.int32)
    block_start = jnp.arange(n_blocks, dtype=jnp.int32) * MOE_BLOCK
    block_expert = jnp.minimum(jnp.sum(block_start[:, None] >= pad_end[None, :], axis=1), N_EXPERTS - 1)
    n_used = (pad_end[-1] // MOE_BLOCK).reshape(1)
    return dest, block_expert.astype(jnp.int32), n_used.astype(jnp.int32), n_blocks * MOE_BLOCK


def kernel(x, c, ctx, c_ctx, w_mod, b_mod, norm_attn_g, norm_ffn_g, w_in, shift_mu, q_norm_g, w_uq, kv_norm_g, w_ukv, decay_w0, decay_up, iclr_a0, iclr_up, gate_up, key_k, key_a, bonus_r_k, lnx_g, lnx_b, w_out, w_grp, b_grp, w_exp, b_exp, w1, w3, w2, final_norm_g):
    n_batch, t_len, _ = x.shape
    assert ctx.shape[1] == CTX_LEN == TM and t_len % TM == 0 and w_mod.shape[0] == 1
    tpb = (CTX_LEN + t_len) // TM
    n = n_batch * t_len

    c_rows = jnp.zeros((8, D_MODEL), F32).at[:n_batch].set(c).at[n_batch].set(c_ctx)
    mod_tab = _mod_call(c_rows, w_mod[0], b_mod[0]).reshape(8 * 6, 1, D_MODEL)

    wi = w_in[0]
    w_kr = wi[:, 768:MLA_IN]
    o = MLA_IN
    w_in_p = jnp.concatenate([
        wi[:, 0:768], _pad_cols(w_kr, LANES), _pad_cols(w_kr[:, _ROPE_SWAP], LANES),
        wi[:, o:o + COLS_RKV],
        _pad_cols(wi[:, o + COLS_RKV:o + COLS_RKV + DECAY_LORA], LANES),
        _pad_cols(wi[:, o + COLS_RKV + DECAY_LORA:o + COLS_RKV + DECAY_LORA + ICLR_LORA], LANES),
        _pad_cols(wi[:, o + COLS_RKV + DECAY_LORA + ICLR_LORA:], 2 * LANES),
    ], axis=1).astype(BF16)
    p_mla, p_rkv, p_lora = _project_call(x, ctx, mod_tab, norm_attn_g[0], w_in_p, tpb)

    mu = shift_mu[0]
    mu_rkv = mu[:, 0:COLS_RKV]
    mu_lora = jnp.concatenate([
        _pad_cols(mu[:, COLS_RKV:COLS_RKV + DECAY_LORA], LANES),
        _pad_cols(mu[:, COLS_RKV + DECAY_LORA:COLS_RKV + DECAY_LORA + ICLR_LORA], LANES),
        _pad_cols(mu[:, COLS_RKV + DECAY_LORA + ICLR_LORA:], 2 * LANES)], axis=1)

    def lora_up(w):
        both = jnp.concatenate([w[0], w[1]], axis=1)
        return jnp.pad(both, ((0, LANES - both.shape[0]), (0, 0))).astype(BF16)

    gup = jnp.pad(gate_up[0], ((0, 2 * LANES - GATE_LORA), (0, 0))).astype(BF16)
    r, k, v, lw, lr, g = _prep_call(
        p_rkv, p_lora, mu_rkv, mu_lora, lora_up(decay_up[0]), lora_up(iclr_up[0]), gup,
        decay_w0[0].reshape(1, -1), iclr_a0[0].reshape(1, -1), tpb)
    key_k2 = key_k[0].reshape(1, -1)
    key_a2 = key_a[0].reshape(1, -1)
    scan_rows = SCAN_SUB * CHUNK
    assert CTX_LEN % scan_rows == 0 and t_len % scan_rows == 0
    y_dirs = [_scan_call(rev, r, k, v, lw, lr, key_k2, key_a2, n_batch,
                         (CTX_LEN + t_len) // scan_rows, CTX_LEN // scan_rows) for rev in (False, True)]

    hd = QK_NOPE_DIM + QK_ROPE_DIM
    wq = w_uq[0].reshape(Q_LORA_RANK, MLA_HEADS, hd)
    zq = jnp.zeros((Q_LORA_RANK, MLA_HEADS, QK_ROPE_DIM), F32)
    wa = jnp.concatenate([wq, zq], axis=2).reshape(Q_LORA_RANK, -1).astype(BF16)
    wb = jnp.concatenate([jnp.zeros((Q_LORA_RANK, MLA_HEADS, QK_NOPE_DIM), F32),
                          wq[:, :, QK_NOPE_DIM:][:, :, _ROPE_SWAP], zq], axis=2
                         ).reshape(Q_LORA_RANK, -1).astype(BF16)
    wkv3 = w_ukv[0].reshape(KV_LORA_RANK, MLA_HEADS, QK_NOPE_DIM + V_HEAD_DIM)
    wk = wkv3[:, :, :QK_NOPE_DIM].reshape(KV_LORA_RANK, -1).astype(BF16)
    wvt = wkv3[:, :, QK_NOPE_DIM:].reshape(KV_LORA_RANK, -1).T.astype(BF16)
    q, kmat, vmat = _mla_prep_call(p_mla, _rope_tables(t_len), q_norm_g[0], kv_norm_g[0], wa, wb, wk, wvt,
                                   n_batch, tpb)
    attn = _attn_call(q, kmat, vmat, n_batch, t_len, tpb)

    w_router = _pad_cols(jnp.concatenate([w_grp[0], w_exp[0]], axis=1), ROUTER_COLS)
    w_router_hi = w_router.astype(BF16)
    w_router2 = jnp.concatenate([w_router_hi, (w_router - w_router_hi.astype(F32)).astype(BF16)], axis=1)
    b_router = _pad_cols(jnp.concatenate([b_grp[0], b_exp[0]]).reshape(1, -1), ROUTER_COLS)
    x1, h2, idx, gates, rank, counts = _mix_call(
        x, attn, y_dirs, r, k, v, lr, g, mod_tab, key_a2, bonus_r_k[0].reshape(1, -1),
        lnx_g[0].reshape(1, -1), lnx_b[0].reshape(1, -1), norm_ffn_g[0],
        w_out[0].astype(BF16), w_router2, b_router, tpb)

    dest, block_expert, n_used, n_slots = _slot_tables(
        idx[:, :TOP_K], rank[:, :TOP_K], counts[0, :N_EXPERTS].astype(jnp.int32), n)
    xs = _scatter_call(dest, h2, n_slots)
    ys = _moe_call(block_expert, n_used, xs, w1[0], w3[0], w2[0])
    out = _final_call(dest, x1, ys, gates, mod_tab, final_norm_g, t_len)
    return out.reshape(n_batch, t_len, D_MODEL)
```

```python
import functools
import math

import jax
import jax.numpy as jnp
import numpy as np
from jax import lax
from jax.experimental import pallas as pl
from jax.experimental.pallas import tpu as pltpu

F32 = jnp.float32
BF16 = jnp.bfloat16
HIGHEST = lax.Precision.HIGHEST

D_MODEL = 2048
CTX_LEN = 256
GRID_W = 64
NORM_EPS = 1e-6

MLA_HEADS = 8
QK_NOPE_DIM = 128
QK_ROPE_DIM = 64
V_HEAD_DIM = 128
Q_LORA_RANK = 512
KV_LORA_RANK = 256
MLA_WIDTH = MLA_HEADS * V_HEAD_DIM
MLA_SCALE = (QK_NOPE_DIM + QK_ROPE_DIM) ** -0.5
ROPE_THETA = 10000.0
ROPE_AXIS_DIM = QK_ROPE_DIM // 2
QK_PAD_DIM = 256
VT_ROWS = V_HEAD_DIM + 16

RWKV_HEAD_DIM = 64
RWKV_WIDTH = D_MODEL - MLA_WIDTH
RWKV_HEADS = RWKV_WIDTH // RWKV_HEAD_DIM
DECAY_LORA = 64
ICLR_LORA = 64
GATE_LORA = 160
LNX_EPS = 64e-5

N_GROUPS = 4
EXPERTS_PER_GROUP = 8
N_EXPERTS = N_GROUPS * EXPERTS_PER_GROUP
TOP_K = 2
D_EXPERT = 512
MOE_BLOCK = 256

MLA_IN = Q_LORA_RANK + KV_LORA_RANK + QK_ROPE_DIM
LANES = 128
TM = 256
CHUNK = 64
PAIR = 2 * RWKV_HEAD_DIM
N_PAIRS = RWKV_WIDTH // PAIR
HEAD_GROUP = 256
PAIRS_PER_STEP = 8
SCAN_SUB = 4
ATTN_TQ = 1024
ATTN_TK = 768
MIX_SUB = 2
DMA_UNROLL = 8
VMEM_LIMIT = 56 * 1024 * 1024

COLS_MLA = 1024
COLS_RKV = 3 * RWKV_WIDTH
COLS_LORA = 512
COLS_IN = COLS_MLA + COLS_RKV + COLS_LORA
ROUTER_COLS = 128


def _cparams(sem):
    return pltpu.CompilerParams(dimension_semantics=sem, vmem_limit_bytes=VMEM_LIMIT)


def _resident(shape, index_map):
    return pl.BlockSpec(shape, index_map, pipeline_mode=pl.Buffered(1))


def _dot(a, b):
    return jnp.dot(a, b, preferred_element_type=F32)


def _dot_nt(a, b):
    return lax.dot_general(a, b, (((1,), (1,)), ((), ())), preferred_element_type=F32)


def _dot_tn(a, b):
    return lax.dot_general(a, b, (((0,), (0,)), ((), ())), preferred_element_type=F32)


def _split2(x):
    hi = x.astype(BF16)
    lo = (x - hi.astype(F32)).astype(BF16)
    return hi, lo


def _head_ones():
    row = lax.broadcasted_iota(jnp.int32, (HEAD_GROUP, HEAD_GROUP), 0)
    col = lax.broadcasted_iota(jnp.int32, (HEAD_GROUP, HEAD_GROUP), 1)
    return jnp.where(row // RWKV_HEAD_DIM == col // RWKV_HEAD_DIM, 1.0, 0.0).astype(BF16)


def _head_sum(x, ones_bd):
    rows = x.shape[0]
    n = x.shape[1] // HEAD_GROUP
    parts = [half[:, c * HEAD_GROUP:(c + 1) * HEAD_GROUP] for half in _split2(x) for c in range(n)]
    res = _dot(jnp.concatenate(parts, axis=0), ones_bd)
    return jnp.concatenate([res[c * rows:(c + 1) * rows] + res[(n + c) * rows:(n + c + 1) * rows]
                            for c in range(n)], axis=1)


def _split3(x):
    hi = x.astype(BF16)
    r1 = x - hi.astype(F32)
    mid = r1.astype(BF16)
    lo = (r1 - mid.astype(F32)).astype(BF16)
    return hi, mid, lo


def _mod_kernel(c_ref, w_ref, b_ref, o_ref):
    c = c_ref[...]
    s = c * jax.nn.sigmoid(c)
    o_ref[...] = jnp.dot(s, w_ref[...], preferred_element_type=F32, precision=HIGHEST) + b_ref[...]


def _mod_call(c_rows, w_mod, b_mod):
    n = w_mod.shape[1]
    tn = 1024
    return pl.pallas_call(
        _mod_kernel,
        grid=(n // tn,),
        in_specs=[
            pl.BlockSpec((8, D_MODEL), lambda i: (0, 0)),
            pl.BlockSpec((D_MODEL, tn), lambda i: (0, i)),
            pl.BlockSpec((1, tn), lambda i: (0, i)),
        ],
        out_specs=pl.BlockSpec((8, tn), lambda i: (0, i)),
        out_shape=jax.ShapeDtypeStruct((8, n), F32),
        compiler_params=_cparams(("arbitrary",)),
        name="mod",
    )(c_rows, w_mod, b_mod.reshape(1, n))


def _project_kernel(tpb, x_ref, ctx_ref, sh_ref, sc_ref, g_ref, w_ref, o_mla, o_rkv, o_lora):
    is_ctx = (pl.program_id(0) % tpb) == 0
    xin = jnp.where(is_ctx, ctx_ref[...], x_ref[...])
    ms = jnp.mean(xin * xin, axis=-1, keepdims=True)
    h = xin * lax.rsqrt(ms + NORM_EPS) * g_ref[...]
    hb = (h * (1.0 + sc_ref[...]) + sh_ref[...]).astype(BF16)
    o_mla[...] = _dot(hb, w_ref[:, 0:COLS_MLA])
    o_rkv[...] = _dot(hb, w_ref[:, COLS_MLA:COLS_MLA + COLS_RKV])
    o_lora[...] = _dot(hb, w_ref[:, COLS_MLA + COLS_RKV:COLS_IN])


def _mod_row(i, tpb, n_batch):
    return jnp.where(i % tpb == 0, n_batch, i // tpb)


def _project_call(x, ctx, mod_tab, norm_g, w_in_p, tpb):
    n_batch, t_len, _ = x.shape
    nt = n_batch * tpb * TM

    def mod_spec(k):
        return pl.BlockSpec((None, 1, D_MODEL), lambda i: (_mod_row(i, tpb, n_batch) * 6 + k, 0, 0))

    return pl.pallas_call(
        functools.partial(_project_kernel, tpb),
        grid=(n_batch * tpb,),
        in_specs=[
            pl.BlockSpec((None, TM, D_MODEL), lambda i: (i // tpb, jnp.maximum(i % tpb - 1, 0), 0)),
            pl.BlockSpec((None, TM, D_MODEL), lambda i: (i // tpb, 0, 0)),
            mod_spec(0),
            mod_spec(1),
            _resident((1, D_MODEL), lambda i: (0, 0)),
            _resident((D_MODEL, COLS_IN), lambda i: (0, 0)),
        ],
        out_specs=[
            pl.BlockSpec((TM, COLS_MLA), lambda i: (i, 0)),
            pl.BlockSpec((TM, COLS_RKV), lambda i: (i, 0)),
            pl.BlockSpec((TM, COLS_LORA), lambda i: (i, 0)),
        ],
        out_shape=[
            jax.ShapeDtypeStruct((nt, COLS_MLA), F32),
            jax.ShapeDtypeStruct((nt, COLS_RKV), F32),
            jax.ShapeDtypeStruct((nt, COLS_LORA), F32),
        ],
        compiler_params=_cparams(("arbitrary",)),
        name="project",
    )(x, ctx, mod_tab, mod_tab, norm_g.reshape(1, D_MODEL), w_in_p)


def _prep_kernel(tpb, p_ref, pp_ref, pn_ref, l_ref, lp_ref, ln_ref, mu_ref, mul_ref,
                 wup_ref, aup_ref, gup_ref, w0_ref, a0_ref,
                 r_o, k_o, v_o, lw_o, a_o, g_o):
    j = pl.program_id(0) % tpb
    no_prev = j <= 1
    no_next = (j == 0) | (j == tpb - 1)

    def shifted(main, prev_blk, next_blk, mu):
        rows = lax.broadcasted_iota(jnp.int32, main.shape, 0)
        prow = jnp.where(no_prev, 0.0, prev_blk[7:8, :])
        nrow = jnp.where(no_next, 0.0, next_blk[0:1, :])
        prev = jnp.where(rows == 0, prow, pltpu.roll(main, 1, 0))
        nxt = jnp.where(rows == TM - 1, nrow, pltpu.roll(main, TM - 1, 0))
        return main + mu[0:1, :] * (prev - main) + mu[1:2, :] * (nxt - main)

    for c, out in enumerate((r_o, k_o, v_o)):
        sl = slice(c * RWKV_WIDTH, (c + 1) * RWKV_WIDTH)
        out[...] = shifted(p_ref[:, sl], pp_ref[:, sl], pn_ref[:, sl], mu_ref[:, sl])

    lo = shifted(l_ref[...], lp_ref[...], ln_ref[...], mul_ref[...])
    wl = jnp.tanh(lo[:, 0:LANES]).astype(BF16)
    al = lo[:, LANES:2 * LANES].astype(BF16)
    gl = jax.nn.sigmoid(lo[:, 2 * LANES:4 * LANES]).astype(BF16)
    w_raw = w0_ref[...] + _dot(wl, wup_ref[...])
    lw_o[...] = -math.exp(-0.5) * jax.nn.sigmoid(w_raw)
    a_o[...] = jax.nn.sigmoid(a0_ref[...] + _dot(al, aup_ref[...]))
    g_o[...] = _dot(gl, gup_ref[...])


def _prep_call(p_rkv, p_lora, mu_rkv, mu_lora, wup, aup, gup, w0, a0, tpb):
    nt = p_rkv.shape[0]
    last8 = nt // 8 - 1
    sub = TM // 8

    def halo(cols):
        return [
            pl.BlockSpec((TM, cols), lambda i: (i, 0)),
            pl.BlockSpec((8, cols), lambda i: (jnp.maximum(i * sub - 1, 0), 0)),
            pl.BlockSpec((8, cols), lambda i: (jnp.minimum((i + 1) * sub, last8), 0)),
        ]

    w2 = 2 * RWKV_WIDTH
    return pl.pallas_call(
        functools.partial(_prep_kernel, tpb),
        grid=(nt // TM,),
        in_specs=halo(COLS_RKV) + halo(COLS_LORA) + [
            _resident((2, COLS_RKV), lambda i: (0, 0)),
            _resident((2, COLS_LORA), lambda i: (0, 0)),
            _resident((LANES, w2), lambda i: (0, 0)),
            _resident((LANES, w2), lambda i: (0, 0)),
            _resident((2 * LANES, RWKV_WIDTH), lambda i: (0, 0)),
            _resident((1, w2), lambda i: (0, 0)),
            _resident((1, w2), lambda i: (0, 0)),
        ],
        out_specs=[
            pl.BlockSpec((TM, RWKV_WIDTH), lambda i: (i, 0)),
            pl.BlockSpec((TM, RWKV_WIDTH), lambda i: (i, 0)),
            pl.BlockSpec((TM, RWKV_WIDTH), lambda i: (i, 0)),
            pl.BlockSpec((TM, w2), lambda i: (i, 0)),
            pl.BlockSpec((TM, w2), lambda i: (i, 0)),
            pl.BlockSpec((TM, RWKV_WIDTH), lambda i: (i, 0)),
        ],
        out_shape=[
            jax.ShapeDtypeStruct((nt, RWKV_WIDTH), F32),
            jax.ShapeDtypeStruct((nt, RWKV_WIDTH), F32),
            jax.ShapeDtypeStruct((nt, RWKV_WIDTH), F32),
            jax.ShapeDtypeStruct((nt, w2), F32),
            jax.ShapeDtypeStruct((nt, w2), F32),
            jax.ShapeDtypeStruct((nt, RWKV_WIDTH), F32),
        ],
        compiler_params=_cparams(("arbitrary",)),
        name="prep",
    )(p_rkv, p_rkv, p_rkv, p_lora, p_lora, p_lora, mu_rkv, mu_lora, wup, aup, gup, w0, a0)


def _stack_heads(x):
    lane = lax.broadcasted_iota(jnp.int32, x.shape, 1)
    zero = jnp.zeros_like(x)
    return jnp.concatenate([jnp.where(lane < RWKV_HEAD_DIM, x, zero),
                            jnp.where(lane >= RWKV_HEAD_DIM, x, zero)], axis=0)


def _unstack_heads(z):
    half = z.shape[0] // 2
    return z[:half] + z[half:]


def _scan_kernel(rev, r_ref, k_ref, v_ref, lw_ref, a_ref, kkey_ref, akey_ref, y_ref, s_scr):
    @pl.when(pl.program_id(2) == 0)
    def _():
        s_scr[...] = jnp.zeros_like(s_scr)

    c2 = 2 * CHUNK
    rows = SCAN_SUB * CHUNK
    tb = lax.broadcasted_iota(jnp.int32, (rows, rows), 0)
    ib = lax.broadcasted_iota(jnp.int32, (rows, rows), 1)
    upto = (ib >= tb) if rev else (ib <= tb)
    cum_mat = jnp.where((tb // CHUNK == ib // CHUNK) & upto, 1.0, 0.0).astype(BF16)

    row = lax.broadcasted_iota(jnp.int32, (c2, LANES), 0)
    col = lax.broadcasted_iota(jnp.int32, (c2, LANES), 1)
    t_idx = row % CHUNK
    i_idx = col % CHUNK
    before = (i_idx > t_idx) if rev else (i_idx < t_idx)
    keep = before | ((i_idx == t_idx) & (row >= CHUNK))
    same_head = (row // RWKV_HEAD_DIM) == (col // RWKV_HEAD_DIM)
    eye_f = jnp.where(row == col, 1.0, 0.0)

    zero = jnp.zeros((c2, LANES), F32)
    pairs = range(PAIRS_PER_STEP)
    units = [(c, p) for c in range(SCAN_SUB) for p in pairs]

    def unit(x, u):
        c, p = u
        return x[c * CHUNK:(c + 1) * CHUNK, p * PAIR:(p + 1) * PAIR]

    r = r_ref[...]
    k = k_ref[...]
    lw = lw_ref[...]
    lr = a_ref[...]
    vb = v_ref[...].astype(BF16)
    kraw = k * kkey_ref[...]
    w_hi, w_mid, w_lo = _split3(lw)
    lp = _dot(cum_mat, w_hi) + _dot(cum_mat, w_mid) + _dot(cum_mat, w_lo)
    kk = kraw * lax.rsqrt(_head_sum(kraw * kraw, _head_ones()) + 1e-12)
    b = kk * lr
    kd = k * (1.0 + (lr - 1.0) * akey_ref[...])
    last = 0 if rev else CHUNK - 1
    ltot_rows = [lp[c * CHUNK + last:c * CHUNK + last + 1, :] for c in range(SCAN_SUB)]
    ltot = jnp.concatenate([jnp.broadcast_to(t, (CHUNK, t.shape[1])) for t in ltot_rows], axis=0)
    e_neg = jnp.exp(-lp)
    e_rest = jnp.exp(ltot - lp)
    e_tot = [jnp.exp(t) for t in ltot_rows]
    at = -kk * jnp.exp(lp - lw)
    rt = r * jnp.exp(lp)
    at_b = at.astype(BF16)
    rt_b = rt.astype(BF16)
    bt_b = (b * e_neg).astype(BF16)
    kt_b = (kd * e_neg).astype(BF16)
    bh = (b * e_rest).astype(BF16)
    kh = (kd * e_rest).astype(BF16)

    sv = {u: _stack_heads(unit(vb, u)) for u in units}
    ar = {u: jnp.concatenate([unit(at_b, u), unit(rt_b, u)], axis=0) for u in units}
    ab = {u: jnp.where(keep, _dot_nt(ar[u], _stack_heads(unit(bt_b, u))), zero) for u in units}
    ak = {u: jnp.where(keep, _dot_nt(ar[u], _stack_heads(unit(kt_b, u))), zero) for u in units}
    a_rb = {u: ab[u][CHUNK:].astype(BF16) for u in units}
    akv = {u: _dot(ak[u].astype(BF16), sv[u]) for u in units}

    pw = {u: _stack_heads(ab[u][:CHUNK]) for u in units}
    tm = {u: eye_f + pw[u] for u in units}
    pw = {u: _dot(pw[u].astype(BF16), pw[u].astype(BF16)) for u in units}
    for _ in range(int(math.log2(CHUNK)) - 2):
        both = {u: _dot(jnp.concatenate([tm[u], pw[u]], axis=0).astype(BF16), pw[u].astype(BF16)) for u in units}
        tm = {u: tm[u] + both[u][:c2] for u in units}
        pw = {u: both[u][c2:] for u in units}
    tm = {u: tm[u] + _dot(tm[u].astype(BF16), pw[u].astype(BF16)) for u in units}
    t_p = {u: _unstack_heads(tm[u]).astype(BF16) for u in units}

    wg = {u: _dot(t_p[u], jnp.concatenate([_stack_heads(akv[u][:CHUNK].astype(BF16)),
                                           _stack_heads(unit(at_b, u))], axis=1)) for u in units}
    w_b = {u: wg[u][:, :LANES].astype(BF16) for u in units}
    g_b = {u: wg[u][:, LANES:].astype(BF16) for u in units}
    qz = {u: _dot(a_rb[u], jnp.concatenate([_stack_heads(g_b[u]), _stack_heads(w_b[u])], axis=1)) for u in units}
    gz = jnp.zeros((CHUNK, LANES), BF16)
    mn = {u: _dot_tn(jnp.concatenate([jnp.concatenate([w_b[u], g_b[u]], axis=1),
                                      jnp.concatenate([unit(vb, u), gz], axis=1)], axis=0),
                     jnp.concatenate([unit(bh, u), unit(kh, u)], axis=0)) for u in units}
    n_st = {u: jnp.where(same_head, mn[u][:c2], zero) for u in units}
    m_bd = {u: jnp.where(same_head, mn[u][c2:], zero).astype(BF16) for u in units}
    q_b = {u: (unit(rt, u) + qz[u][:, :LANES]).astype(BF16) for u in units}
    z = {u: qz[u][:, LANES:] + akv[u][CHUNK:] for u in units}

    state = [s_scr[p] for p in pairs]
    for c in (reversed(range(SCAN_SUB)) if rev else range(SCAN_SUB)):
        s_b = [state[p].astype(BF16) for p in pairs]
        for p in pairs:
            y_ref[c * CHUNK:(c + 1) * CHUNK, p * PAIR:(p + 1) * PAIR] = (
                _dot_nt(q_b[(c, p)], _stack_heads(s_b[p])) + z[(c, p)])
        state = [state[p] * e_tot[c][:, p * PAIR:(p + 1) * PAIR] + _dot(s_b[p], m_bd[(c, p)])
                 + _unstack_heads(n_st[(c, p)]) for p in pairs]
    for p in pairs:
        s_scr[p] = state[p]


def _scan_call(rev, r, k, v, lw, lr, key_k, key_a, n_batch, bpb, ctx_blocks):
    nt = r.shape[0]
    groups = N_PAIRS // PAIRS_PER_STEP
    gw = PAIRS_PER_STEP * PAIR
    rows = SCAN_SUB * CHUNK
    d = 1 if rev else 0

    def block_row(b, j):
        if rev:
            j = jnp.where(j < ctx_blocks, ctx_blocks - 1 - j, bpb + ctx_blocks - 1 - j)
        return b * bpb + j

    shared = pl.BlockSpec((rows, gw), lambda b, g, j: (block_row(b, j), g))
    per_dir = pl.BlockSpec((rows, gw), lambda b, g, j: (block_row(b, j), d * groups + g))
    keys = pl.BlockSpec((1, gw), lambda b, g, j: (0, g))
    return pl.pallas_call(
        functools.partial(_scan_kernel, rev),
        grid=(n_batch, groups, bpb),
        in_specs=[shared, shared, shared, per_dir, per_dir, keys, keys],
        out_specs=shared,
        out_shape=jax.ShapeDtypeStruct((nt, RWKV_WIDTH), F32),
        scratch_shapes=[pltpu.VMEM((PAIRS_PER_STEP, RWKV_HEAD_DIM, PAIR), F32)],
        compiler_params=_cparams(("arbitrary", "arbitrary", "arbitrary")),
        name="scan_bwd" if rev else "scan_fwd",
    )(r, k, v, lw, lr, key_k, key_a)


def _mla_prep_kernel(p_ref, ck_ref, sk_ref, cq_ref, sq_ref, qg_ref, kvg_ref, wa_ref, wb_ref, wkv_ref, wvt_ref,
                     q_o, k_o, v_o):
    cq = p_ref[:, 0:Q_LORA_RANK]
    cqn = (cq * lax.rsqrt(jnp.mean(cq * cq, axis=-1, keepdims=True) + NORM_EPS) * qg_ref[...]).astype(BF16)
    ckv = p_ref[:, Q_LORA_RANK:Q_LORA_RANK + KV_LORA_RANK]
    ckvn = (ckv * lax.rsqrt(jnp.mean(ckv * ckv, axis=-1, keepdims=True) + NORM_EPS) * kvg_ref[...]).astype(BF16)
    kr_a = p_ref[:, 768:896]
    kr_b = p_ref[:, 896:1024]
    k_rot = (kr_a * ck_ref[...] + kr_b * sk_ref[...]).astype(BF16)
    cos_q = cq_ref[...]
    sin_q = sq_ref[...]
    for h in range(MLA_HEADS):
        hs = slice(h * QK_PAD_DIM, (h + 1) * QK_PAD_DIM)
        q_o[:, hs] = (_dot(cqn, wa_ref[:, hs]) * cos_q + _dot(cqn, wb_ref[:, hs]) * sin_q).astype(BF16)
        k_o[:, h * QK_PAD_DIM:h * QK_PAD_DIM + QK_NOPE_DIM] = _dot(
            ckvn, wkv_ref[:, h * QK_NOPE_DIM:(h + 1) * QK_NOPE_DIM]).astype(BF16)
        k_o[:, h * QK_PAD_DIM + QK_NOPE_DIM:(h + 1) * QK_PAD_DIM] = k_rot
    v_t = _dot_nt(wvt_ref[...], ckvn)
    for h in range(MLA_HEADS):
        v_o[h, 0:V_HEAD_DIM, :] = v_t[h * V_HEAD_DIM:(h + 1) * V_HEAD_DIM, :].astype(BF16)
        v_o[h, V_HEAD_DIM:VT_ROWS, :] = jnp.ones((VT_ROWS - V_HEAD_DIM, TM), BF16)


def _mla_prep_call(p_mla, tabs, q_norm_g, kv_norm_g, wa, wb, wk, wvt, n_batch, tpb):
    nt = p_mla.shape[0]
    ck, sk, cq, sq = tabs
    qw = MLA_HEADS * QK_PAD_DIM
    return pl.pallas_call(
        _mla_prep_kernel,
        grid=(nt // TM,),
        in_specs=[
            pl.BlockSpec((TM, COLS_MLA), lambda i: (i, 0)),
            pl.BlockSpec((TM, LANES), lambda i: (i % tpb, 0)),
            pl.BlockSpec((TM, LANES), lambda i: (i % tpb, 0)),
            pl.BlockSpec((TM, QK_PAD_DIM), lambda i: (i % tpb, 0)),
            pl.BlockSpec((TM, QK_PAD_DIM), lambda i: (i % tpb, 0)),
            _resident((1, Q_LORA_RANK), lambda i: (0, 0)),
            _resident((1, KV_LORA_RANK), lambda i: (0, 0)),
            _resident((Q_LORA_RANK, qw), lambda i: (0, 0)),
            _resident((Q_LORA_RANK, qw), lambda i: (0, 0)),
            _resident((KV_LORA_RANK, MLA_WIDTH), lambda i: (0, 0)),
            _resident((MLA_WIDTH, KV_LORA_RANK), lambda i: (0, 0)),
        ],
        out_specs=[
            pl.BlockSpec((TM, qw), lambda i: (i, 0)),
            pl.BlockSpec((TM, qw), lambda i: (i, 0)),
            pl.BlockSpec((None, MLA_HEADS, VT_ROWS, TM), lambda i: (i // tpb, 0, 0, i % tpb)),
        ],
        out_shape=[
            jax.ShapeDtypeStruct((nt, qw), BF16),
            jax.ShapeDtypeStruct((nt, qw), BF16),
            jax.ShapeDtypeStruct((n_batch, MLA_HEADS, VT_ROWS, tpb * TM), BF16),
        ],
        compiler_params=_cparams(("arbitrary",)),
        name="mla_prep",
    )(p_mla, ck, sk, cq, sq, q_norm_g.reshape(1, -1), kv_norm_g.reshape(1, -1), wa, wb, wk, wvt)


def _attn_kernel(n_kv, *refs):
    q_refs, (k_ref, v_ref, o_ref) = refs[:-3], refs[-3:]
    qs = [q_ref[...] for q_ref in q_refs]
    chains = range(len(qs))

    m = [jnp.full((1, TM), -jnp.inf, F32) for _ in chains]
    acc = [jnp.zeros((VT_ROWS, TM), F32) for _ in chains]
    def scores(j):
        kj = k_ref[j * ATTN_TK:(j + 1) * ATTN_TK, :]
        return [_dot_nt(kj, qs[c]) for c in chains]

    s_next = scores(0)
    for j in range(n_kv):
        vj = v_ref[:, j * ATTN_TK:(j + 1) * ATTN_TK]
        s = s_next
        if j + 1 < n_kv:
            s_next = scores(j + 1)
        for c in chains:
            m_new = jnp.maximum(m[c], jnp.max(s[c], axis=0, keepdims=True))
            alpha = jnp.exp2(m[c] - m_new)
            p = jnp.exp2((s[c] - m_new).astype(BF16))
            acc[c] = alpha * acc[c] + _dot(vj, p)
            m[c] = m_new
    for c in chains:
        out = acc[c][0:V_HEAD_DIM, :] / acc[c][V_HEAD_DIM:V_HEAD_DIM + 1, :]
        o_ref[c * TM:(c + 1) * TM, :] = jnp.transpose(out).astype(o_ref.dtype)


def _attn_call(q, k, v, n_batch, t_len, tpb):
    rows_b = tpb * TM
    assert rows_b % ATTN_TK == 0 and t_len % ATTN_TQ == 0 and ATTN_TQ % TM == 0
    n_q = t_len // ATTN_TQ
    sub = ATTN_TQ // TM
    k3 = k.reshape(n_batch, rows_b, MLA_HEADS * QK_PAD_DIM)

    def q_spec(u):
        return pl.BlockSpec((TM, QK_PAD_DIM), lambda b, h, i: (b * tpb + 1 + i * sub + u, h))

    return pl.pallas_call(
        functools.partial(_attn_kernel, rows_b // ATTN_TK),
        grid=(n_batch, MLA_HEADS, n_q),
        in_specs=[q_spec(u) for u in range(sub)] + [
            pl.BlockSpec((None, rows_b, QK_PAD_DIM), lambda b, h, i: (b, 0, h)),
            pl.BlockSpec((None, None, VT_ROWS, rows_b), lambda b, h, i: (b, h, 0, 0)),
        ],
        out_specs=pl.BlockSpec((ATTN_TQ, V_HEAD_DIM), lambda b, h, i: (b * n_q + i, h)),
        out_shape=jax.ShapeDtypeStruct((n_batch * t_len, MLA_WIDTH), BF16),
        compiler_params=_cparams(("arbitrary", "arbitrary", "arbitrary")),
        name="attention",
    )(*([q] * sub), k3, v)


def _slot_rank(idx, run_ref):
    lane = lax.broadcasted_iota(jnp.int32, idx.shape, 1)
    oh0 = lane == idx[:, 0:1]
    oh1 = lane == idx[:, 1:2]
    both = jnp.where(oh0 | oh1, 1.0, 0.0)
    t_row = lax.broadcasted_iota(jnp.int32, (TM, TM), 0)
    t_col = lax.broadcasted_iota(jnp.int32, (TM, TM), 1)
    earlier = jnp.where(t_col < t_row, 1.0, 0.0).astype(BF16)
    seen = _dot(earlier, both.astype(BF16)) + run_ref[...]
    r0 = jnp.sum(jnp.where(oh0, seen, 0.0), axis=-1, keepdims=True)
    r1 = jnp.sum(jnp.where(oh1, seen, 0.0), axis=-1, keepdims=True)
    run_ref[...] = run_ref[...] + jnp.sum(both, axis=0, keepdims=True)
    return jnp.where(lane == 0, r0, jnp.where(lane == 1, r1, 0.0)).astype(jnp.int32)


def _route(logits):
    lane = lax.broadcasted_iota(jnp.int32, logits.shape, 1)
    neg = jnp.full_like(logits, -jnp.inf)
    big = jnp.full_like(lane, 2 ** 30)
    is_grp = lane < N_GROUPS
    gl = jnp.where(is_grp, logits, neg)
    ge = jnp.exp(gl - jnp.max(gl, axis=-1, keepdims=True))
    gp = ge / jnp.sum(ge, axis=-1, keepdims=True)
    g_val = jnp.max(gp, axis=-1, keepdims=True)
    g_idx = jnp.min(jnp.where(is_grp & (gp == g_val), lane, big), axis=-1, keepdims=True)
    e_lane = lane - N_GROUPS
    in_grp = (e_lane >= g_idx * EXPERTS_PER_GROUP) & (e_lane < (g_idx + 1) * EXPERTS_PER_GROUP)
    el = jnp.where(in_grp, logits, neg)
    ee = jnp.exp(el - jnp.max(el, axis=-1, keepdims=True))
    ep = ee / jnp.sum(ee, axis=-1, keepdims=True)
    v1 = jnp.max(ep, axis=-1, keepdims=True)
    i1 = jnp.min(jnp.where(in_grp & (ep == v1), lane, big), axis=-1, keepdims=True)
    rest = in_grp & (lane != i1)
    v2 = jnp.max(jnp.where(rest, ep, neg), axis=-1, keepdims=True)
    i2 = jnp.min(jnp.where(rest & (ep == v2), lane, big), axis=-1, keepdims=True)
    denom = v1 + v2
    idx = jnp.where(lane == 0, i1 - N_GROUPS, jnp.where(lane == 1, i2 - N_GROUPS, 0))
    gate = jnp.where(lane == 0, g_val * v1 / denom, jnp.where(lane == 1, g_val * v2 / denom, 0.0))
    return idx, gate


def _mix_kernel(x_ref, attn_ref, yf_ref, yb_ref, r_ref, k_ref, v_ref, af_ref, ab_ref, g_ref,
                g1_ref, sh2_ref, sc2_ref, akey_ref, rk_ref, lng_ref, lnb_ref, ng_ref,
                wo_ref, wr_ref, br_ref,
                x1_o, h2_o, idx_o, gate_o, rank_o, cnt_o, run_scr):
    @pl.when(pl.program_id(0) == 0)
    def _():
        run_scr[...] = jnp.zeros_like(run_scr)

    subs = range(MIX_SUB)
    rows = TM // MIX_SUB

    def part(ref, u):
        return ref[u * rows:(u + 1) * rows, :]

    ones_bd = _head_ones()
    inv = 1.0 / RWKV_HEAD_DIM
    y = [part(yf_ref, u) + part(yb_ref, u) for u in subs]
    mu = [_head_sum(y[u], ones_bd) * inv for u in subs]
    dy = [y[u] - mu[u] for u in subs]
    var = [_head_sum(dy[u] * dy[u], ones_bd) * inv for u in subs]
    k_sum = [part(k_ref, u) * (2.0 + (part(af_ref, u) + part(ab_ref, u) - 2.0) * akey_ref[...]) for u in subs]
    bonus = [_head_sum(part(r_ref, u) * k_sum[u] * rk_ref[...], ones_bd) * part(v_ref, u) for u in subs]
    yn = [dy[u] * lax.rsqrt(var[u] + LNX_EPS) * lng_ref[...] + lnb_ref[...] for u in subs]
    rw = [((yn[u] + bonus[u]) * part(g_ref, u)).astype(BF16) for u in subs]
    o = [_dot(part(attn_ref, u), wo_ref[0:MLA_WIDTH, :]) + _dot(rw[u], wo_ref[MLA_WIDTH:D_MODEL, :])
         for u in subs]
    x1 = [part(x_ref, u) + g1_ref[...] * o[u] for u in subs]
    h2 = [x1[u] * lax.rsqrt(jnp.mean(x1[u] * x1[u], axis=-1, keepdims=True) + NORM_EPS) * ng_ref[...]
          * (1.0 + sc2_ref[...]) + sh2_ref[...] for u in subs]
    h_hl = [_split2(h2[u]) for u in subs]
    both = [_dot(h_hl[u][0], wr_ref[...]) for u in subs]
    lo_hi = [_dot(h_hl[u][1], wr_ref[:, 0:ROUTER_COLS]) for u in subs]
    routed = [_route(both[u][:, 0:ROUTER_COLS] + both[u][:, ROUTER_COLS:2 * ROUTER_COLS] + lo_hi[u] + br_ref[...])
              for u in subs]
    for u in subs:
        sl = slice(u * rows, (u + 1) * rows)
        x1_o[sl, :] = x1[u]
        h2_o[sl, :] = h2[u]
        idx_o[sl, :] = routed[u][0]
        gate_o[sl, :] = routed[u][1]
    rank_o[...] = _slot_rank(jnp.concatenate([routed[u][0] for u in subs], axis=0), run_scr)
    cnt_o[...] = run_scr[...]


def _mix_call(x, attn, yscan, r, k, v, lr, g, mod_tab, key_a, bonus_rk, lnx_g, lnx_b, norm_g,
              w_out_b, w_router, b_router, tpb):
    n_batch, t_len, _ = x.shape
    tpl = t_len // TM
    n = n_batch * t_len

    def lat(i):
        return (i // tpl) * tpb + 1 + i % tpl

    def tok(cols, col_blk=0):
        return pl.BlockSpec((TM, cols), lambda i: (lat(i), col_blk))

    def mod_spec(kk):
        return pl.BlockSpec((None, 1, D_MODEL), lambda i: ((i // tpl) * 6 + kk, 0, 0))

    def vec(cols):
        return _resident((1, cols), lambda i: (0, 0))

    tile_out = pl.BlockSpec((TM, ROUTER_COLS), lambda i: (i, 0))
    return pl.pallas_call(
        _mix_kernel,
        grid=(n // TM,),
        in_specs=[
            pl.BlockSpec((None, TM, D_MODEL), lambda i: (i // tpl, i % tpl, 0)),
            pl.BlockSpec((TM, MLA_WIDTH), lambda i: (i, 0)),
            tok(RWKV_WIDTH), tok(RWKV_WIDTH),
            tok(RWKV_WIDTH), tok(RWKV_WIDTH), tok(RWKV_WIDTH),
            tok(RWKV_WIDTH, 0), tok(RWKV_WIDTH, 1), tok(RWKV_WIDTH),
            mod_spec(2), mod_spec(3), mod_spec(4),
            vec(RWKV_WIDTH), vec(RWKV_WIDTH), vec(RWKV_WIDTH), vec(RWKV_WIDTH), vec(D_MODEL),
            _resident((D_MODEL, D_MODEL), lambda i: (0, 0)),
            _resident((D_MODEL, 2 * ROUTER_COLS), lambda i: (0, 0)),
            vec(ROUTER_COLS),
        ],
        out_specs=[
            pl.BlockSpec((TM, D_MODEL), lambda i: (i, 0)),
            pl.BlockSpec((TM, D_MODEL), lambda i: (i, 0)),
            tile_out, tile_out, tile_out,
            pl.BlockSpec((1, ROUTER_COLS), lambda i: (0, 0)),
        ],
        out_shape=[
            jax.ShapeDtypeStruct((n, D_MODEL), F32),
            jax.ShapeDtypeStruct((n, D_MODEL), F32),
            jax.ShapeDtypeStruct((n, ROUTER_COLS), jnp.int32),
            jax.ShapeDtypeStruct((n, ROUTER_COLS), F32),
            jax.ShapeDtypeStruct((n, ROUTER_COLS), jnp.int32),
            jax.ShapeDtypeStruct((1, ROUTER_COLS), F32),
        ],
        scratch_shapes=[pltpu.VMEM((1, ROUTER_COLS), F32)],
        compiler_params=_cparams(("arbitrary",)),
        name="mix",
    )(x, attn, yscan[0], yscan[1], r, k, v, lr, lr, g, mod_tab, mod_tab, mod_tab,
      key_a, bonus_rk, lnx_g, lnx_b, norm_g.reshape(1, D_MODEL), w_out_b, w_router, b_router)


def _scatter_kernel(dest_ref, h_ref, init_hbm, xs_hbm, sem):
    del init_hbm
    base = pl.program_id(0) * (TM * TOP_K)

    def row(t, slot):
        return pltpu.make_async_copy(h_ref.at[pl.ds(t, 1)], xs_hbm.at[pl.ds(slot, 1)], sem)

    def start(t, c):
        for kk in range(TOP_K):
            row(t, dest_ref[base + t * TOP_K + kk]).start()
        return c
    lax.fori_loop(0, TM, start, 0, unroll=DMA_UNROLL)

    def wait(t, c):
        for _ in range(TOP_K):
            row(t, 0).wait()
        return c
    lax.fori_loop(0, TM, wait, 0, unroll=DMA_UNROLL)


def _scatter_call(dest, h2, n_slots):
    n = h2.shape[0]
    return pl.pallas_call(
        _scatter_kernel,
        grid_spec=pltpu.PrefetchScalarGridSpec(
            num_scalar_prefetch=1,
            grid=(n // TM,),
            in_specs=[
                pl.BlockSpec((TM, D_MODEL), lambda i, dest: (i, 0)),
                pl.BlockSpec(memory_space=pl.ANY),
            ],
            out_specs=pl.BlockSpec(memory_space=pl.ANY),
            scratch_shapes=[pltpu.SemaphoreType.DMA],
        ),
        out_shape=jax.ShapeDtypeStruct((n_slots, D_MODEL), F32),
        input_output_aliases={2: 0},
        compiler_params=_cparams(("arbitrary",)),
        name="scatter",
    )(dest, h2, jnp.zeros((n_slots, D_MODEL), F32))


def _moe_kernel(be_ref, used_ref, x_ref, w1_ref, w3_ref, w2_ref, y_ref, w1b, w3b, w2b):
    i = pl.program_id(0)

    @pl.when(i < used_ref[0])
    def _():
        @pl.when((i == 0) | (be_ref[i] != be_ref[jnp.maximum(i - 1, 0)]))
        def _():
            w1b[...] = w1_ref[...].astype(BF16)
            w3b[...] = w3_ref[...].astype(BF16)
            w2b[...] = w2_ref[...].astype(BF16)

        x = x_ref[...].astype(BF16)
        a1 = _dot(x, w1b[...])
        a3 = _dot(x, w3b[...])
        hm = (a1 * jax.nn.sigmoid(a1) * a3).astype(BF16)
        y_ref[...] = _dot(hm, w2b[...])

    @pl.when(i >= used_ref[0])
    def _():
        y_ref[...] = jnp.zeros_like(y_ref)


def _moe_call(block_expert, n_used, xs, w1, w3, w2):
    n_blocks = block_expert.shape[0]

    def wspec(shape):
        return pl.BlockSpec((None,) + shape, lambda i, be, used: (be[i], 0, 0))

    return pl.pallas_call(
        _moe_kernel,
        grid_spec=pltpu.PrefetchScalarGridSpec(
            num_scalar_prefetch=2,
            grid=(n_blocks,),
            in_specs=[
                pl.BlockSpec((MOE_BLOCK, D_MODEL), lambda i, be, used: (jnp.minimum(i, used[0] - 1), 0)),
                wspec((D_MODEL, D_EXPERT)),
                wspec((D_MODEL, D_EXPERT)),
                wspec((D_EXPERT, D_MODEL)),
            ],
            out_specs=pl.BlockSpec((MOE_BLOCK, D_MODEL), lambda i, be, used: (i, 0)),
            scratch_shapes=[
                pltpu.VMEM((D_MODEL, D_EXPERT), BF16),
                pltpu.VMEM((D_MODEL, D_EXPERT), BF16),
                pltpu.VMEM((D_EXPERT, D_MODEL), BF16),
            ],
        ),
        out_shape=jax.ShapeDtypeStruct(xs.shape, F32),
        compiler_params=_cparams(("arbitrary",)),
        name="moe",
    )(block_expert, n_used, xs, w1, w3, w2)


def _final_kernel(dest_ref, x1_ref, gate_ref, g2_ref, ng_ref, ys_hbm, o_ref, ybuf, sems):
    i = pl.program_id(0)
    last = pl.num_programs(0) - 1
    cur = i % 2

    def row(buf, t, kk, slot):
        return pltpu.make_async_copy(ys_hbm.at[pl.ds(slot, 1)], ybuf.at[buf, kk, pl.ds(t, 1)], sems.at[buf])

    def fetch(buf, tile, t):
        for kk in range(TOP_K):
            row(buf, t, kk, dest_ref[(tile * TM + t) * TOP_K + kk]).start()

    def wait_all(buf):
        def wait(t, c):
            for kk in range(TOP_K):
                row(buf, t, kk, 0).wait()
            return c
        lax.fori_loop(0, TM, wait, 0, unroll=DMA_UNROLL)

    @pl.when(i == 0)
    def _():
        def start(t, c):
            fetch(0, 0, t)
            return c
        lax.fori_loop(0, TM, start, 0, unroll=DMA_UNROLL)

    wait_all(cur)
    nxt = jnp.minimum(i + 1, last)
    for t in range(TM):
        fetch(1 - cur, nxt, t)

    gate = gate_ref[...]
    y = ybuf[cur, 0] * gate[:, 0:1] + ybuf[cur, 1] * gate[:, 1:2]
    x2 = x1_ref[...] + g2_ref[...] * y
    o_ref[...] = x2 * lax.rsqrt(jnp.mean(x2 * x2, axis=-1, keepdims=True) + NORM_EPS) * ng_ref[...]

    @pl.when(i == last)
    def _():
        wait_all(1 - cur)


def _final_call(dest, x1, ys, gates, mod_tab, final_g, t_len):
    n = x1.shape[0]
    tpl = t_len // TM
    return pl.pallas_call(
        _final_kernel,
        grid_spec=pltpu.PrefetchScalarGridSpec(
            num_scalar_prefetch=1,
            grid=(n // TM,),
            in_specs=[
                pl.BlockSpec((TM, D_MODEL), lambda i, dest: (i, 0)),
                pl.BlockSpec((TM, ROUTER_COLS), lambda i, dest: (i, 0)),
                pl.BlockSpec((None, 1, D_MODEL), lambda i, dest: ((i // tpl) * 6 + 5, 0, 0)),
                _resident((1, D_MODEL), lambda i, dest: (0, 0)),
                pl.BlockSpec(memory_space=pl.ANY),
            ],
            out_specs=pl.BlockSpec((TM, D_MODEL), lambda i, dest: (i, 0)),
            scratch_shapes=[pltpu.VMEM((2, TOP_K, TM, D_MODEL), F32), pltpu.SemaphoreType.DMA((2,))],
        ),
        out_shape=jax.ShapeDtypeStruct((n, D_MODEL), F32),
        compiler_params=_cparams(("arbitrary",)),
        name="final",
    )(dest, x1, gates, mod_tab, final_g.reshape(1, D_MODEL), ys)


def _pad_cols(w, width):
    return jnp.pad(w, ((0, 0), (0, width - w.shape[1])))


_ROPE_SWAP = np.concatenate([np.arange(16, 32), np.arange(0, 16), np.arange(48, 64), np.arange(32, 48)])


def _rope_tables(t_len):
    pos = jnp.arange(t_len)
    inv_freq = ROPE_THETA ** (-jnp.arange(0, ROPE_AXIS_DIM, 2, dtype=F32) / ROPE_AXIS_DIM)
    ang_r = (pos // GRID_W)[:, None].astype(F32) * inv_freq
    ang_c = (pos % GRID_W)[:, None].astype(F32) * inv_freq
    cos = jnp.concatenate([jnp.cos(ang_r)] * 2 + [jnp.cos(ang_c)] * 2, axis=1)
    sin = jnp.concatenate([-jnp.sin(ang_r), jnp.sin(ang_r), -jnp.sin(ang_c), jnp.sin(ang_c)], axis=1)
    cos = jnp.concatenate([jnp.ones((CTX_LEN, QK_ROPE_DIM), F32), cos], axis=0)
    sin = jnp.concatenate([jnp.zeros((CTX_LEN, QK_ROPE_DIM), F32), sin], axis=0)
    rows = cos.shape[0]
    z64 = jnp.zeros((rows, QK_ROPE_DIM), F32)
    ck = jnp.concatenate([cos, z64], axis=1)
    sk = jnp.concatenate([sin, z64], axis=1)
    q_scale = MLA_SCALE * math.log2(math.e)
    cq = q_scale * jnp.concatenate([jnp.ones((rows, QK_NOPE_DIM), F32), cos, z64], axis=1)
    sq = q_scale * jnp.concatenate([jnp.zeros((rows, QK_NOPE_DIM), F32), sin, z64], axis=1)
    return ck, sk, cq, sq


def _slot_tables(idx2, rank2, counts, n_tokens):
    n_blocks = (n_tokens * TOP_K + N_EXPERTS * (MOE_BLOCK - 1) + MOE_BLOCK - 1) // MOE_BLOCK
    padded = (counts + MOE_BLOCK - 1) // MOE_BLOCK * MOE_BLOCK
    pad_end = jnp.cumsum(padded)
    pad_start = pad_end - padded
    experts = jnp.arange(N_EXPERTS, dtype=jnp.int32)
    first = jnp.sum(jnp.where(idx2[..., None] == experts, pad_start, 0), axis=-1)
    dest = (first + rank2).reshape(-1).astype(jnp.int32)
    block_start = jnp.arange(n_blocks, dtype=jnp.int32) * MOE_BLOCK
    block_expert = jnp.minimum(jnp.sum(block_start[:, None] >= pad_end[None, :], axis=1), N_EXPERTS - 1)
    n_used = (pad_end[-1] // MOE_BLOCK).reshape(1)
    return dest, block_expert.astype(jnp.int32), n_used.astype(jnp.int32), n_blocks * MOE_BLOCK


def kernel(x, c, ctx, c_ctx, w_mod, b_mod, norm_attn_g, norm_ffn_g, w_in, shift_mu, q_norm_g, w_uq, kv_norm_g, w_ukv, decay_w0, decay_up, iclr_a0, iclr_up, gate_up, key_k, key_a, bonus_r_k, lnx_g, lnx_b, w_out, w_grp, b_grp, w_exp, b_exp, w1, w3, w2, final_norm_g):
    n_batch, t_len, _ = x.shape
    assert ctx.shape[1] == CTX_LEN == TM and t_len % TM == 0 and w_mod.shape[0] == 1
    tpb = (CTX_LEN + t_len) // TM
    n = n_batch * t_len

    c_rows = jnp.zeros((8, D_MODEL), F32).at[:n_batch].set(c).at[n_batch].set(c_ctx)
    mod_tab = _mod_call(c_rows, w_mod[0], b_mod[0]).reshape(8 * 6, 1, D_MODEL)

    wi = w_in[0]
    w_kr = wi[:, 768:MLA_IN]
    o = MLA_IN
    w_in_p = jnp.concatenate([
        wi[:, 0:768], _pad_cols(w_kr, LANES), _pad_cols(w_kr[:, _ROPE_SWAP], LANES),
        wi[:, o:o + COLS_RKV],
        _pad_cols(wi[:, o + COLS_RKV:o + COLS_RKV + DECAY_LORA], LANES),
        _pad_cols(wi[:, o + COLS_RKV + DECAY_LORA:o + COLS_RKV + DECAY_LORA + ICLR_LORA], LANES),
        _pad_cols(wi[:, o + COLS_RKV + DECAY_LORA + ICLR_LORA:], 2 * LANES),
    ], axis=1).astype(BF16)
    p_mla, p_rkv, p_lora = _project_call(x, ctx, mod_tab, norm_attn_g[0], w_in_p, tpb)

    mu = shift_mu[0]
    mu_rkv = mu[:, 0:COLS_RKV]
    mu_lora = jnp.concatenate([
        _pad_cols(mu[:, COLS_RKV:COLS_RKV + DECAY_LORA], LANES),
        _pad_cols(mu[:, COLS_RKV + DECAY_LORA:COLS_RKV + DECAY_LORA + ICLR_LORA], LANES),
        _pad_cols(mu[:, COLS_RKV + DECAY_LORA + ICLR_LORA:], 2 * LANES)], axis=1)

    def lora_up(w):
        both = jnp.concatenate([w[0], w[1]], axis=1)
        return jnp.pad(both, ((0, LANES - both.shape[0]), (0, 0))).astype(BF16)

    gup = jnp.pad(gate_up[0], ((0, 2 * LANES - GATE_LORA), (0, 0))).astype(BF16)
    r, k, v, lw, lr, g = _prep_call(
        p_rkv, p_lora, mu_rkv, mu_lora, lora_up(decay_up[0]), lora_up(iclr_up[0]), gup,
        decay_w0[0].reshape(1, -1), iclr_a0[0].reshape(1, -1), tpb)
    key_k2 = key_k[0].reshape(1, -1)
    key_a2 = key_a[0].reshape(1, -1)
    scan_rows = SCAN_SUB * CHUNK
    assert CTX_LEN % scan_rows == 0 and t_len % scan_rows == 0
    y_dirs = [_scan_call(rev, r, k, v, lw, lr, key_k2, key_a2, n_batch,
                         (CTX_LEN + t_len) // scan_rows, CTX_LEN // scan_rows) for rev in (False, True)]

    hd = QK_NOPE_DIM + QK_ROPE_DIM
    wq = w_uq[0].reshape(Q_LORA_RANK, MLA_HEADS, hd)
    zq = jnp.zeros((Q_LORA_RANK, MLA_HEADS, QK_ROPE_DIM), F32)
    wa = jnp.concatenate([wq, zq], axis=2).reshape(Q_LORA_RANK, -1).astype(BF16)
    wb = jnp.concatenate([jnp.zeros((Q_LORA_RANK, MLA_HEADS, QK_NOPE_DIM), F32),
                          wq[:, :, QK_NOPE_DIM:][:, :, _ROPE_SWAP], zq], axis=2
                         ).reshape(Q_LORA_RANK, -1).astype(BF16)
    wkv3 = w_ukv[0].reshape(KV_LORA_RANK, MLA_HEADS, QK_NOPE_DIM + V_HEAD_DIM)
    wk = wkv3[:, :, :QK_NOPE_DIM].reshape(KV_LORA_RANK, -1).astype(BF16)
    wvt = wkv3[:, :, QK_NOPE_DIM:].reshape(KV_LORA_RANK, -1).T.astype(BF16)
    q, kmat, vmat = _mla_prep_call(p_mla, _rope_tables(t_len), q_norm_g[0], kv_norm_g[0], wa, wb, wk, wvt,
                                   n_batch, tpb)
    attn = _attn_call(q, kmat, vmat, n_batch, t_len, tpb)

    w_router = _pad_cols(jnp.concatenate([w_grp[0], w_exp[0]], axis=1), ROUTER_COLS)
    w_router_hi = w_router.astype(BF16)
    w_router2 = jnp.concatenate([w_router_hi, (w_router - w_router_hi.astype(F32)).astype(BF16)], axis=1)
    b_router = _pad_cols(jnp.concatenate([b_grp[0], b_exp[0]]).reshape(1, -1), ROUTER_COLS)
    x1, h2, idx, gates, rank, counts = _mix_call(
        x, attn, y_dirs, r, k, v, lr, g, mod_tab, key_a2, bonus_r_k[0].reshape(1, -1),
        lnx_g[0].reshape(1, -1), lnx_b[0].reshape(1, -1), norm_ffn_g[0],
        w_out[0].astype(BF16), w_router2, b_router, tpb)

    dest, block_expert, n_used, n_slots = _slot_tables(
        idx[:, :TOP_K], rank[:, :TOP_K], counts[0, :N_EXPERTS].astype(jnp.int32), n)
    xs = _scatter_call(dest, h2, n_slots)
    ys = _moe_call(block_expert, n_used, xs, w1[0], w3[0], w2[0])
    out = _final_call(dest, x1, ys, gates, mod_tab, final_norm_g, t_len)
    return out.reshape(n_batch, t_len, D_MODEL)
```

```python
import functools
import math

import jax
import jax.numpy as jnp
import numpy as np
from jax import lax
from jax.experimental import pallas as pl
from jax.experimental.pallas import tpu as pltpu

F32 = jnp.float32
BF16 = jnp.bfloat16
HIGHEST = lax.Precision.HIGHEST

D_MODEL = 2048
CTX_LEN = 256
GRID_W = 64
NORM_EPS = 1e-6

MLA_HEADS = 8
QK_NOPE_DIM = 128
QK_ROPE_DIM = 64
V_HEAD_DIM = 128
Q_LORA_RANK = 512
KV_LORA_RANK = 256
MLA_WIDTH = MLA_HEADS * V_HEAD_DIM
MLA_SCALE = (QK_NOPE_DIM + QK_ROPE_DIM) ** -0.5
ROPE_THETA = 10000.0
ROPE_AXIS_DIM = QK_ROPE_DIM // 2
QK_PAD_DIM = 256
VT_ROWS = V_HEAD_DIM + 16

RWKV_HEAD_DIM = 64
RWKV_WIDTH = D_MODEL - MLA_WIDTH
RWKV_HEADS = RWKV_WIDTH // RWKV_HEAD_DIM
DECAY_LORA = 64
ICLR_LORA = 64
GATE_LORA = 160
LNX_EPS = 64e-5

N_GROUPS = 4
EXPERTS_PER_GROUP = 8
N_EXPERTS = N_GROUPS * EXPERTS_PER_GROUP
TOP_K = 2
D_EXPERT = 512
MOE_BLOCK = 256

MLA_IN = Q_LORA_RANK + KV_LORA_RANK + QK_ROPE_DIM
LANES = 128
TM = 256
HALO = 16
CHUNK = 64
PAIR = 2 * RWKV_HEAD_DIM
N_PAIRS = RWKV_WIDTH // PAIR
HEAD_GROUP = 256
PAIRS_PER_STEP = 8
SCAN_SUB = 4
ATTN_TQ = 1024
ATTN_TK = 768
MIX_SUB = 2
DMA_UNROLL = 8
VMEM_LIMIT = 56 * 1024 * 1024

COLS_MLA = 1024
COLS_RKV = 3 * RWKV_WIDTH
COLS_LORA = 512
COLS_IN = COLS_MLA + COLS_RKV + COLS_LORA
ROUTER_COLS = 128


def _cparams(sem):
    return pltpu.CompilerParams(dimension_semantics=sem, vmem_limit_bytes=VMEM_LIMIT)


def _resident(shape, index_map):
    return pl.BlockSpec(shape, index_map, pipeline_mode=pl.Buffered(1))


def _dot(a, b):
    return jnp.dot(a, b, preferred_element_type=F32)


def _dot_nt(a, b):
    return lax.dot_general(a, b, (((1,), (1,)), ((), ())), preferred_element_type=F32)


def _dot_tn(a, b):
    return lax.dot_general(a, b, (((0,), (0,)), ((), ())), preferred_element_type=F32)


def _split2(x):
    hi = x.astype(BF16)
    lo = (x - hi.astype(F32)).astype(BF16)
    return hi, lo


def _head_ones():
    row = lax.broadcasted_iota(jnp.int32, (HEAD_GROUP, HEAD_GROUP), 0)
    col = lax.broadcasted_iota(jnp.int32, (HEAD_GROUP, HEAD_GROUP), 1)
    return jnp.where(row // RWKV_HEAD_DIM == col // RWKV_HEAD_DIM, 1.0, 0.0).astype(BF16)


def _head_sum(x, ones_bd):
    rows = x.shape[0]
    n = x.shape[1] // HEAD_GROUP
    parts = [half[:, c * HEAD_GROUP:(c + 1) * HEAD_GROUP] for half in _split2(x) for c in range(n)]
    res = _dot(jnp.concatenate(parts, axis=0), ones_bd)
    return jnp.concatenate([res[c * rows:(c + 1) * rows] + res[(n + c) * rows:(n + c + 1) * rows]
                            for c in range(n)], axis=1)


def _split3(x):
    hi = x.astype(BF16)
    r1 = x - hi.astype(F32)
    mid = r1.astype(BF16)
    lo = (r1 - mid.astype(F32)).astype(BF16)
    return hi, mid, lo


def _mod_kernel(c_ref, w_ref, b_ref, o_ref):
    c = c_ref[...]
    s = c * jax.nn.sigmoid(c)
    o_ref[...] = jnp.dot(s, w_ref[...], preferred_element_type=F32, precision=HIGHEST) + b_ref[...]


def _mod_call(c_rows, w_mod, b_mod):
    n = w_mod.shape[1]
    tn = 1024
    return pl.pallas_call(
        _mod_kernel,
        grid=(n // tn,),
        in_specs=[
            pl.BlockSpec((8, D_MODEL), lambda i: (0, 0)),
            pl.BlockSpec((D_MODEL, tn), lambda i: (0, i)),
            pl.BlockSpec((1, tn), lambda i: (0, i)),
        ],
        out_specs=pl.BlockSpec((8, tn), lambda i: (0, i)),
        out_shape=jax.ShapeDtypeStruct((8, n), F32),
        compiler_params=_cparams(("arbitrary",)),
        name="mod",
    )(c_rows, w_mod, b_mod.reshape(1, n))


def _project_kernel(tpb, x_ref, xp_ref, xn_ref, ctx_ref, sh_ref, sc_ref, g_ref, w_ref, mu_ref, mul_ref,
                    wup_ref, aup_ref, gup_ref, w0_ref, a0_ref,
                    o_mla, r_o, k_o, v_o, lw_o, a_o, g_o):
    j = pl.program_id(0) % tpb
    is_ctx = j == 0
    zero_prev_row = jnp.where(j <= 1, 0, -1)
    zero_next_row = jnp.where(is_ctx | (j == tpb - 1), TM - 1, -1)

    xin = jnp.where(is_ctx, ctx_ref[...], x_ref[...])
    xe = jnp.concatenate([xp_ref[...], xin, xn_ref[...]], axis=0)
    ms = jnp.mean(xe * xe, axis=-1, keepdims=True)
    h = xe * lax.rsqrt(ms + NORM_EPS) * g_ref[...]
    h = h * (1.0 + sc_ref[...]) + sh_ref[...]
    o_mla[...] = _dot(h[HALO:HALO + TM].astype(BF16), w_ref[:, 0:COLS_MLA])
    hb = h.astype(BF16)
    ext = TM + 2 * HALO

    def shifted(c0, c1, mu):
        pe = _dot(hb, w_ref[:, c0:c1])
        main = pe[HALO:HALO + TM]
        row = lax.broadcasted_iota(jnp.int32, main.shape, 0)
        prev = jnp.where(row == zero_prev_row, 0.0, pltpu.roll(pe, 1, 0)[HALO:HALO + TM])
        nxt = jnp.where(row == zero_next_row, 0.0, pltpu.roll(pe, ext - 1, 0)[HALO:HALO + TM])
        return main + mu[0:1, :] * (prev - main) + mu[1:2, :] * (nxt - main)

    for c, out in enumerate((r_o, k_o, v_o)):
        c0 = COLS_MLA + c * RWKV_WIDTH
        out[...] = shifted(c0, c0 + RWKV_WIDTH, mu_ref[:, c * RWKV_WIDTH:(c + 1) * RWKV_WIDTH])

    lo = shifted(COLS_MLA + COLS_RKV, COLS_IN, mul_ref[...])
    wl = jnp.tanh(lo[:, 0:LANES]).astype(BF16)
    al = lo[:, LANES:2 * LANES].astype(BF16)
    gl = jax.nn.sigmoid(lo[:, 2 * LANES:4 * LANES]).astype(BF16)
    w_raw = w0_ref[...] + _dot(wl, wup_ref[...])
    lw_o[...] = -math.exp(-0.5) * jax.nn.sigmoid(w_raw)
    a_o[...] = jax.nn.sigmoid(a0_ref[...] + _dot(al, aup_ref[...]))
    g_o[...] = _dot(gl, gup_ref[...])


def _mod_row(i, tpb, n_batch):
    return jnp.where(i % tpb == 0, n_batch, i // tpb)


def _project_call(x, ctx, mod_tab, norm_g, w_in_p, mu_rkv, mu_lora, wup, aup, gup, w0, a0, tpb):
    n_batch, t_len, _ = x.shape
    nt = n_batch * tpb * TM
    sub = TM // HALO
    last_halo = t_len // HALO - 1
    w2 = 2 * RWKV_WIDTH

    def mod_spec(k):
        return pl.BlockSpec((None, 1, D_MODEL), lambda i: (_mod_row(i, tpb, n_batch) * 6 + k, 0, 0))

    def lat(i):
        return jnp.maximum(i % tpb - 1, 0)

    def tile_out(cols):
        return pl.BlockSpec((TM, cols), lambda i: (i, 0))

    return pl.pallas_call(
        functools.partial(_project_kernel, tpb),
        grid=(n_batch * tpb,),
        in_specs=[
            pl.BlockSpec((None, TM, D_MODEL), lambda i: (i // tpb, lat(i), 0)),
            pl.BlockSpec((None, HALO, D_MODEL), lambda i: (i // tpb, jnp.maximum(lat(i) * sub - 1, 0), 0)),
            pl.BlockSpec((None, HALO, D_MODEL), lambda i: (i // tpb, jnp.minimum((lat(i) + 1) * sub, last_halo), 0)),
            pl.BlockSpec((None, TM, D_MODEL), lambda i: (i // tpb, 0, 0)),
            mod_spec(0),
            mod_spec(1),
            _resident((1, D_MODEL), lambda i: (0, 0)),
            _resident((D_MODEL, COLS_IN), lambda i: (0, 0)),
            _resident((2, COLS_RKV), lambda i: (0, 0)),
            _resident((2, COLS_LORA), lambda i: (0, 0)),
            _resident((LANES, w2), lambda i: (0, 0)),
            _resident((LANES, w2), lambda i: (0, 0)),
            _resident((2 * LANES, RWKV_WIDTH), lambda i: (0, 0)),
            _resident((1, w2), lambda i: (0, 0)),
            _resident((1, w2), lambda i: (0, 0)),
        ],
        out_specs=[tile_out(COLS_MLA), tile_out(RWKV_WIDTH), tile_out(RWKV_WIDTH), tile_out(RWKV_WIDTH),
                   tile_out(w2), tile_out(w2), tile_out(RWKV_WIDTH)],
        out_shape=[jax.ShapeDtypeStruct((nt, cols), F32)
                   for cols in (COLS_MLA, RWKV_WIDTH, RWKV_WIDTH, RWKV_WIDTH, w2, w2, RWKV_WIDTH)],
        compiler_params=_cparams(("arbitrary",)),
        name="project",
    )(x, x, x, ctx, mod_tab, mod_tab, norm_g.reshape(1, D_MODEL), w_in_p,
      mu_rkv, mu_lora, wup, aup, gup, w0, a0)


def _stack_heads(x):
    lane = lax.broadcasted_iota(jnp.int32, x.shape, 1)
    zero = jnp.zeros_like(x)
    return jnp.concatenate([jnp.where(lane < RWKV_HEAD_DIM, x, zero),
                            jnp.where(lane >= RWKV_HEAD_DIM, x, zero)], axis=0)


def _unstack_heads(z):
    half = z.shape[0] // 2
    return z[:half] + z[half:]


def _scan_kernel(rev, r_ref, k_ref, v_ref, lw_ref, a_ref, kkey_ref, akey_ref, y_ref, s_scr):
    @pl.when(pl.program_id(2) == 0)
    def _():
        s_scr[...] = jnp.zeros_like(s_scr)

    c2 = 2 * CHUNK
    rows = SCAN_SUB * CHUNK
    tb = lax.broadcasted_iota(jnp.int32, (rows, rows), 0)
    ib = lax.broadcasted_iota(jnp.int32, (rows, rows), 1)
    upto = (ib >= tb) if rev else (ib <= tb)
    cum_mat = jnp.where((tb // CHUNK == ib // CHUNK) & upto, 1.0, 0.0).astype(BF16)

    row = lax.broadcasted_iota(jnp.int32, (c2, LANES), 0)
    col = lax.broadcasted_iota(jnp.int32, (c2, LANES), 1)
    t_idx = row % CHUNK
    i_idx = col % CHUNK
    before = (i_idx > t_idx) if rev else (i_idx < t_idx)
    keep = before | ((i_idx == t_idx) & (row >= CHUNK))
    same_head = (row // RWKV_HEAD_DIM) == (col // RWKV_HEAD_DIM)
    eye_f = jnp.where(row == col, 1.0, 0.0)

    zero = jnp.zeros((c2, LANES), F32)
    pairs = range(PAIRS_PER_STEP)
    units = [(c, p) for c in range(SCAN_SUB) for p in pairs]

    def unit(x, u):
        c, p = u
        return x[c * CHUNK:(c + 1) * CHUNK, p * PAIR:(p + 1) * PAIR]

    r = r_ref[...]
    k = k_ref[...]
    lw = lw_ref[...]
    lr = a_ref[...]
    vb = v_ref[...].astype(BF16)
    kraw = k * kkey_ref[...]
    w_hi, w_mid, w_lo = _split3(lw)
    lp = _dot(cum_mat, w_hi) + _dot(cum_mat, w_mid) + _dot(cum_mat, w_lo)
    kk = kraw * lax.rsqrt(_head_sum(kraw * kraw, _head_ones()) + 1e-12)
    b = kk * lr
    kd = k * (1.0 + (lr - 1.0) * akey_ref[...])
    last = 0 if rev else CHUNK - 1
    ltot_rows = [lp[c * CHUNK + last:c * CHUNK + last + 1, :] for c in range(SCAN_SUB)]
    ltot = jnp.concatenate([jnp.broadcast_to(t, (CHUNK, t.shape[1])) for t in ltot_rows], axis=0)
    e_neg = jnp.exp(-lp)
    e_rest = jnp.exp(ltot - lp)
    e_tot = [jnp.exp(t) for t in ltot_rows]
    at = -kk * jnp.exp(lp - lw)
    rt = r * jnp.exp(lp)
    at_b = at.astype(BF16)
    rt_b = rt.astype(BF16)
    bt_b = (b * e_neg).astype(BF16)
    kt_b = (kd * e_neg).astype(BF16)
    bh = (b * e_rest).astype(BF16)
    kh = (kd * e_rest).astype(BF16)

    sv = {u: _stack_heads(unit(vb, u)) for u in units}
    ar = {u: jnp.concatenate([unit(at_b, u), unit(rt_b, u)], axis=0) for u in units}
    ab = {u: jnp.where(keep, _dot_nt(ar[u], _stack_heads(unit(bt_b, u))), zero) for u in units}
    ak = {u: jnp.where(keep, _dot_nt(ar[u], _stack_heads(unit(kt_b, u))), zero) for u in units}
    a_rb = {u: ab[u][CHUNK:].astype(BF16) for u in units}
    akv = {u: _dot(ak[u].astype(BF16), sv[u]) for u in units}

    pw = {u: _stack_heads(ab[u][:CHUNK]) for u in units}
    tm = {u: eye_f + pw[u] for u in units}
    pw = {u: _dot(pw[u].astype(BF16), pw[u].astype(BF16)) for u in units}
    for _ in range(int(math.log2(CHUNK)) - 2):
        both = {u: _dot(jnp.concatenate([tm[u], pw[u]], axis=0).astype(BF16), pw[u].astype(BF16)) for u in units}
        tm = {u: tm[u] + both[u][:c2] for u in units}
        pw = {u: both[u][c2:] for u in units}
    tm = {u: tm[u] + _dot(tm[u].astype(BF16), pw[u].astype(BF16)) for u in units}
    t_p = {u: _unstack_heads(tm[u]).astype(BF16) for u in units}

    wg = {u: _dot(t_p[u], jnp.concatenate([_stack_heads(akv[u][:CHUNK].astype(BF16)),
                                           _stack_heads(unit(at_b, u))], axis=1)) for u in units}
    w_b = {u: wg[u][:, :LANES].astype(BF16) for u in units}
    g_b = {u: wg[u][:, LANES:].astype(BF16) for u in units}
    qz = {u: _dot(a_rb[u], jnp.concatenate([_stack_heads(g_b[u]), _stack_heads(w_b[u])], axis=1)) for u in units}
    gz = jnp.zeros((CHUNK, LANES), BF16)
    mn = {u: _dot_tn(jnp.concatenate([jnp.concatenate([w_b[u], g_b[u]], axis=1),
                                      jnp.concatenate([unit(vb, u), gz], axis=1)], axis=0),
                     jnp.concatenate([unit(bh, u), unit(kh, u)], axis=0)) for u in units}
    n_st = {u: jnp.where(same_head, mn[u][:c2], zero) for u in units}
    m_bd = {u: jnp.where(same_head, mn[u][c2:], zero).astype(BF16) for u in units}
    q_b = {u: (unit(rt, u) + qz[u][:, :LANES]).astype(BF16) for u in units}
    z = {u: qz[u][:, LANES:] + akv[u][CHUNK:] for u in units}

    state = [s_scr[p] for p in pairs]
    for c in (reversed(range(SCAN_SUB)) if rev else range(SCAN_SUB)):
        s_b = [state[p].astype(BF16) for p in pairs]
        for p in pairs:
            y_ref[c * CHUNK:(c + 1) * CHUNK, p * PAIR:(p + 1) * PAIR] = (
                _dot_nt(q_b[(c, p)], _stack_heads(s_b[p])) + z[(c, p)])
        state = [state[p] * e_tot[c][:, p * PAIR:(p + 1) * PAIR] + _dot(s_b[p], m_bd[(c, p)])
                 + _unstack_heads(n_st[(c, p)]) for p in pairs]
    for p in pairs:
        s_scr[p] = state[p]


def _scan_call(rev, r, k, v, lw, lr, key_k, key_a, n_batch, bpb, ctx_blocks):
    nt = r.shape[0]
    groups = N_PAIRS // PAIRS_PER_STEP
    gw = PAIRS_PER_STEP * PAIR
    rows = SCAN_SUB * CHUNK
    d = 1 if rev else 0

    def block_row(b, j):
        if rev:
            j = jnp.where(j < ctx_blocks, ctx_blocks - 1 - j, bpb + ctx_blocks - 1 - j)
        return b * bpb + j

    shared = pl.BlockSpec((rows, gw), lambda b, g, j: (block_row(b, j), g))
    per_dir = pl.BlockSpec((rows, gw), lambda b, g, j: (block_row(b, j), d * groups + g))
    keys = pl.BlockSpec((1, gw), lambda b, g, j: (0, g))
    return pl.pallas_call(
        functools.partial(_scan_kernel, rev),
        grid=(n_batch, groups, bpb),
        in_specs=[shared, shared, shared, per_dir, per_dir, keys, keys],
        out_specs=shared,
        out_shape=jax.ShapeDtypeStruct((nt, RWKV_WIDTH), F32),
        scratch_shapes=[pltpu.VMEM((PAIRS_PER_STEP, RWKV_HEAD_DIM, PAIR), F32)],
        compiler_params=_cparams(("arbitrary", "arbitrary", "arbitrary")),
        name="scan_bwd" if rev else "scan_fwd",
    )(r, k, v, lw, lr, key_k, key_a)


def _mla_prep_kernel(p_ref, ck_ref, sk_ref, cq_ref, sq_ref, qg_ref, kvg_ref, wa_ref, wb_ref, wkv_ref, wvt_ref,
                     q_o, k_o, v_o):
    cq = p_ref[:, 0:Q_LORA_RANK]
    cqn = (cq * lax.rsqrt(jnp.mean(cq * cq, axis=-1, keepdims=True) + NORM_EPS) * qg_ref[...]).astype(BF16)
    ckv = p_ref[:, Q_LORA_RANK:Q_LORA_RANK + KV_LORA_RANK]
    ckvn = (ckv * lax.rsqrt(jnp.mean(ckv * ckv, axis=-1, keepdims=True) + NORM_EPS) * kvg_ref[...]).astype(BF16)
    kr_a = p_ref[:, 768:896]
    kr_b = p_ref[:, 896:1024]
    k_rot = (kr_a * ck_ref[...] + kr_b * sk_ref[...]).astype(BF16)
    cos_q = cq_ref[...]
    sin_q = sq_ref[...]
    for h in range(MLA_HEADS):
        hs = slice(h * QK_PAD_DIM, (h + 1) * QK_PAD_DIM)
        q_o[:, hs] = (_dot(cqn, wa_ref[:, hs]) * cos_q + _dot(cqn, wb_ref[:, hs]) * sin_q).astype(BF16)
        k_o[:, h * QK_PAD_DIM:h * QK_PAD_DIM + QK_NOPE_DIM] = _dot(
            ckvn, wkv_ref[:, h * QK_NOPE_DIM:(h + 1) * QK_NOPE_DIM]).astype(BF16)
        k_o[:, h * QK_PAD_DIM + QK_NOPE_DIM:(h + 1) * QK_PAD_DIM] = k_rot
    v_t = _dot_nt(wvt_ref[...], ckvn)
    for h in range(MLA_HEADS):
        v_o[h, 0:V_HEAD_DIM, :] = v_t[h * V_HEAD_DIM:(h + 1) * V_HEAD_DIM, :].astype(BF16)
        v_o[h, V_HEAD_DIM:VT_ROWS, :] = jnp.ones((VT_ROWS - V_HEAD_DIM, TM), BF16)


def _mla_prep_call(p_mla, tabs, q_norm_g, kv_norm_g, wa, wb, wk, wvt, n_batch, tpb):
    nt = p_mla.shape[0]
    ck, sk, cq, sq = tabs
    qw = MLA_HEADS * QK_PAD_DIM
    return pl.pallas_call(
        _mla_prep_kernel,
        grid=(nt // TM,),
        in_specs=[
            pl.BlockSpec((TM, COLS_MLA), lambda i: (i, 0)),
            pl.BlockSpec((TM, LANES), lambda i: (i % tpb, 0)),
            pl.BlockSpec((TM, LANES), lambda i: (i % tpb, 0)),
            pl.BlockSpec((TM, QK_PAD_DIM), lambda i: (i % tpb, 0)),
            pl.BlockSpec((TM, QK_PAD_DIM), lambda i: (i % tpb, 0)),
            _resident((1, Q_LORA_RANK), lambda i: (0, 0)),
            _resident((1, KV_LORA_RANK), lambda i: (0, 0)),
            _resident((Q_LORA_RANK, qw), lambda i: (0, 0)),
            _resident((Q_LORA_RANK, qw), lambda i: (0, 0)),
            _resident((KV_LORA_RANK, MLA_WIDTH), lambda i: (0, 0)),
            _resident((MLA_WIDTH, KV_LORA_RANK), lambda i: (0, 0)),
        ],
        out_specs=[
            pl.BlockSpec((TM, qw), lambda i: (i, 0)),
            pl.BlockSpec((TM, qw), lambda i: (i, 0)),
            pl.BlockSpec((None, MLA_HEADS, VT_ROWS, TM), lambda i: (i // tpb, 0, 0, i % tpb)),
        ],
        out_shape=[
            jax.ShapeDtypeStruct((nt, qw), BF16),
            jax.ShapeDtypeStruct((nt, qw), BF16),
            jax.ShapeDtypeStruct((n_batch, MLA_HEADS, VT_ROWS, tpb * TM), BF16),
        ],
        compiler_params=_cparams(("arbitrary",)),
        name="mla_prep",
    )(p_mla, ck, sk, cq, sq, q_norm_g.reshape(1, -1), kv_norm_g.reshape(1, -1), wa, wb, wk, wvt)


def _attn_kernel(n_kv, *refs):
    q_refs, (k_ref, v_ref, o_ref) = refs[:-3], refs[-3:]
    qs = [q_ref[...] for q_ref in q_refs]
    chains = range(len(qs))

    m = [jnp.full((1, TM), -jnp.inf, F32) for _ in chains]
    acc = [jnp.zeros((VT_ROWS, TM), F32) for _ in chains]
    def scores(j):
        kj = k_ref[j * ATTN_TK:(j + 1) * ATTN_TK, :]
        return [_dot_nt(kj, qs[c]) for c in chains]

    s_next = scores(0)
    for j in range(n_kv):
        vj = v_ref[:, j * ATTN_TK:(j + 1) * ATTN_TK]
        s = s_next
        if j + 1 < n_kv:
            s_next = scores(j + 1)
        for c in chains:
            m_new = jnp.maximum(m[c], jnp.max(s[c], axis=0, keepdims=True))
            alpha = jnp.exp2(m[c] - m_new)
            p = jnp.exp2((s[c] - m_new).astype(BF16))
            acc[c] = alpha * acc[c] + _dot(vj, p)
            m[c] = m_new
    for c in chains:
        out = acc[c][0:V_HEAD_DIM, :] / acc[c][V_HEAD_DIM:V_HEAD_DIM + 1, :]
        o_ref[c * TM:(c + 1) * TM, :] = jnp.transpose(out).astype(o_ref.dtype)


def _attn_call(q, k, v, n_batch, t_len, tpb):
    rows_b = tpb * TM
    assert rows_b % ATTN_TK == 0 and t_len % ATTN_TQ == 0 and ATTN_TQ % TM == 0
    n_q = t_len // ATTN_TQ
    sub = ATTN_TQ // TM
    k3 = k.reshape(n_batch, rows_b, MLA_HEADS * QK_PAD_DIM)

    def q_spec(u):
        return pl.BlockSpec((TM, QK_PAD_DIM), lambda b, h, i: (b * tpb + 1 + i * sub + u, h))

    return pl.pallas_call(
        functools.partial(_attn_kernel, rows_b // ATTN_TK),
        grid=(n_batch, MLA_HEADS, n_q),
        in_specs=[q_spec(u) for u in range(sub)] + [
            pl.BlockSpec((None, rows_b, QK_PAD_DIM), lambda b, h, i: (b, 0, h)),
            pl.BlockSpec((None, None, VT_ROWS, rows_b), lambda b, h, i: (b, h, 0, 0)),
        ],
        out_specs=pl.BlockSpec((ATTN_TQ, V_HEAD_DIM), lambda b, h, i: (b * n_q + i, h)),
        out_shape=jax.ShapeDtypeStruct((n_batch * t_len, MLA_WIDTH), BF16),
        compiler_params=_cparams(("arbitrary", "arbitrary", "arbitrary")),
        name="attention",
    )(*([q] * sub), k3, v)


def _slot_rank(idx, run_ref):
    lane = lax.broadcasted_iota(jnp.int32, idx.shape, 1)
    oh0 = lane == idx[:, 0:1]
    oh1 = lane == idx[:, 1:2]
    both = jnp.where(oh0 | oh1, 1.0, 0.0)
    t_row = lax.broadcasted_iota(jnp.int32, (TM, TM), 0)
    t_col = lax.broadcasted_iota(jnp.int32, (TM, TM), 1)
    earlier = jnp.where(t_col < t_row, 1.0, 0.0).astype(BF16)
    seen = _dot(earlier, both.astype(BF16)) + run_ref[...]
    r0 = jnp.sum(jnp.where(oh0, seen, 0.0), axis=-1, keepdims=True)
    r1 = jnp.sum(jnp.where(oh1, seen, 0.0), axis=-1, keepdims=True)
    run_ref[...] = run_ref[...] + jnp.sum(both, axis=0, keepdims=True)
    return jnp.where(lane == 0, r0, jnp.where(lane == 1, r1, 0.0)).astype(jnp.int32)


def _route(logits):
    lane = lax.broadcasted_iota(jnp.int32, logits.shape, 1)
    neg = jnp.full_like(logits, -jnp.inf)
    big = jnp.full_like(lane, 2 ** 30)
    is_grp = lane < N_GROUPS
    gl = jnp.where(is_grp, logits, neg)
    ge = jnp.exp(gl - jnp.max(gl, axis=-1, keepdims=True))
    gp = ge / jnp.sum(ge, axis=-1, keepdims=True)
    g_val = jnp.max(gp, axis=-1, keepdims=True)
    g_idx = jnp.min(jnp.where(is_grp & (gp == g_val), lane, big), axis=-1, keepdims=True)
    e_lane = lane - N_GROUPS
    in_grp = (e_lane >= g_idx * EXPERTS_PER_GROUP) & (e_lane < (g_idx + 1) * EXPERTS_PER_GROUP)
    el = jnp.where(in_grp, logits, neg)
    ee = jnp.exp(el - jnp.max(el, axis=-1, keepdims=True))
    ep = ee / jnp.sum(ee, axis=-1, keepdims=True)
    v1 = jnp.max(ep, axis=-1, keepdims=True)
    i1 = jnp.min(jnp.where(in_grp & (ep == v1), lane, big), axis=-1, keepdims=True)
    rest = in_grp & (lane != i1)
    v2 = jnp.max(jnp.where(rest, ep, neg), axis=-1, keepdims=True)
    i2 = jnp.min(jnp.where(rest & (ep == v2), lane, big), axis=-1, keepdims=True)
    denom = v1 + v2
    idx = jnp.where(lane == 0, i1 - N_GROUPS, jnp.where(lane == 1, i2 - N_GROUPS, 0))
    gate = jnp.where(lane == 0, g_val * v1 / denom, jnp.where(lane == 1, g_val * v2 / denom, 0.0))
    return idx, gate


def _mix_kernel(x_ref, attn_ref, yf_ref, yb_ref, r_ref, k_ref, v_ref, af_ref, ab_ref, g_ref,
                g1_ref, sh2_ref, sc2_ref, akey_ref, rk_ref, lng_ref, lnb_ref, ng_ref,
                wo_ref, wr_ref, br_ref,
                x1_o, h2_o, idx_o, gate_o, rank_o, cnt_o, run_scr):
    @pl.when(pl.program_id(0) == 0)
    def _():
        run_scr[...] = jnp.zeros_like(run_scr)

    subs = range(MIX_SUB)
    rows = TM // MIX_SUB

    def part(ref, u):
        return ref[u * rows:(u + 1) * rows, :]

    ones_bd = _head_ones()
    inv = 1.0 / RWKV_HEAD_DIM
    y = [part(yf_ref, u) + part(yb_ref, u) for u in subs]
    mu = [_head_sum(y[u], ones_bd) * inv for u in subs]
    dy = [y[u] - mu[u] for u in subs]
    var = [_head_sum(dy[u] * dy[u], ones_bd) * inv for u in subs]
    k_sum = [part(k_ref, u) * (2.0 + (part(af_ref, u) + part(ab_ref, u) - 2.0) * akey_ref[...]) for u in subs]
    bonus = [_head_sum(part(r_ref, u) * k_sum[u] * rk_ref[...], ones_bd) * part(v_ref, u) for u in subs]
    yn = [dy[u] * lax.rsqrt(var[u] + LNX_EPS) * lng_ref[...] + lnb_ref[...] for u in subs]
    rw = [((yn[u] + bonus[u]) * part(g_ref, u)).astype(BF16) for u in subs]
    o = [_dot(part(attn_ref, u), wo_ref[0:MLA_WIDTH, :]) + _dot(rw[u], wo_ref[MLA_WIDTH:D_MODEL, :])
         for u in subs]
    x1 = [part(x_ref, u) + g1_ref[...] * o[u] for u in subs]
    h2 = [x1[u] * lax.rsqrt(jnp.mean(x1[u] * x1[u], axis=-1, keepdims=True) + NORM_EPS) * ng_ref[...]
          * (1.0 + sc2_ref[...]) + sh2_ref[...] for u in subs]
    h_hl = [_split2(h2[u]) for u in subs]
    both = [_dot(h_hl[u][0], wr_ref[...]) for u in subs]
    lo_hi = [_dot(h_hl[u][1], wr_ref[:, 0:ROUTER_COLS]) for u in subs]
    routed = [_route(both[u][:, 0:ROUTER_COLS] + both[u][:, ROUTER_COLS:2 * ROUTER_COLS] + lo_hi[u] + br_ref[...])
              for u in subs]
    for u in subs:
        sl = slice(u * rows, (u + 1) * rows)
        x1_o[sl, :] = x1[u]
        h2_o[sl, :] = h2[u]
        idx_o[sl, :] = routed[u][0]
        gate_o[sl, :] = routed[u][1]
    rank_o[...] = _slot_rank(jnp.concatenate([routed[u][0] for u in subs], axis=0), run_scr)
    cnt_o[...] = run_scr[...]


def _mix_call(x, attn, yscan, r, k, v, lr, g, mod_tab, key_a, bonus_rk, lnx_g, lnx_b, norm_g,
              w_out_b, w_router, b_router, tpb):
    n_batch, t_len, _ = x.shape
    tpl = t_len // TM
    n = n_batch * t_len

    def lat(i):
        return (i // tpl) * tpb + 1 + i % tpl

    def tok(cols, col_blk=0):
        return pl.BlockSpec((TM, cols), lambda i: (lat(i), col_blk))

    def mod_spec(kk):
        return pl.BlockSpec((None, 1, D_MODEL), lambda i: ((i // tpl) * 6 + kk, 0, 0))

    def vec(cols):
        return _resident((1, cols), lambda i: (0, 0))

    tile_out = pl.BlockSpec((TM, ROUTER_COLS), lambda i: (i, 0))
    return pl.pallas_call(
        _mix_kernel,
        grid=(n // TM,),
        in_specs=[
            pl.BlockSpec((None, TM, D_MODEL), lambda i: (i // tpl, i % tpl, 0)),
            pl.BlockSpec((TM, MLA_WIDTH), lambda i: (i, 0)),
            tok(RWKV_WIDTH), tok(RWKV_WIDTH),
            tok(RWKV_WIDTH), tok(RWKV_WIDTH), tok(RWKV_WIDTH),
            tok(RWKV_WIDTH, 0), tok(RWKV_WIDTH, 1), tok(RWKV_WIDTH),
            mod_spec(2), mod_spec(3), mod_spec(4),
            vec(RWKV_WIDTH), vec(RWKV_WIDTH), vec(RWKV_WIDTH), vec(RWKV_WIDTH), vec(D_MODEL),
            _resident((D_MODEL, D_MODEL), lambda i: (0, 0)),
            _resident((D_MODEL, 2 * ROUTER_COLS), lambda i: (0, 0)),
            vec(ROUTER_COLS),
        ],
        out_specs=[
            pl.BlockSpec((TM, D_MODEL), lambda i: (i, 0)),
            pl.BlockSpec((TM, D_MODEL), lambda i: (i, 0)),
            tile_out, tile_out, tile_out,
            pl.BlockSpec((1, ROUTER_COLS), lambda i: (0, 0)),
        ],
        out_shape=[
            jax.ShapeDtypeStruct((n, D_MODEL), F32),
            jax.ShapeDtypeStruct((n, D_MODEL), F32),
            jax.ShapeDtypeStruct((n, ROUTER_COLS), jnp.int32),
            jax.ShapeDtypeStruct((n, ROUTER_COLS), F32),
            jax.ShapeDtypeStruct((n, ROUTER_COLS), jnp.int32),
            jax.ShapeDtypeStruct((1, ROUTER_COLS), F32),
        ],
        scratch_shapes=[pltpu.VMEM((1, ROUTER_COLS), F32)],
        compiler_params=_cparams(("arbitrary",)),
        name="mix",
    )(x, attn, yscan[0], yscan[1], r, k, v, lr, lr, g, mod_tab, mod_tab, mod_tab,
      key_a, bonus_rk, lnx_g, lnx_b, norm_g.reshape(1, D_MODEL), w_out_b, w_router, b_router)


def _scatter_kernel(dest_ref, h_ref, init_hbm, xs_hbm, sem):
    del init_hbm
    base = pl.program_id(0) * (TM * TOP_K)

    def row(t, slot):
        return pltpu.make_async_copy(h_ref.at[pl.ds(t, 1)], xs_hbm.at[pl.ds(slot, 1)], sem)

    def start(t, c):
        for kk in range(TOP_K):
            row(t, dest_ref[base + t * TOP_K + kk]).start()
        return c
    lax.fori_loop(0, TM, start, 0, unroll=DMA_UNROLL)

    def wait(t, c):
        for _ in range(TOP_K):
            row(t, 0).wait()
        return c
    lax.fori_loop(0, TM, wait, 0, unroll=DMA_UNROLL)


def _scatter_call(dest, h2, n_slots):
    n = h2.shape[0]
    return pl.pallas_call(
        _scatter_kernel,
        grid_spec=pltpu.PrefetchScalarGridSpec(
            num_scalar_prefetch=1,
            grid=(n // TM,),
            in_specs=[
                pl.BlockSpec((TM, D_MODEL), lambda i, dest: (i, 0)),
                pl.BlockSpec(memory_space=pl.ANY),
            ],
            out_specs=pl.BlockSpec(memory_space=pl.ANY),
            scratch_shapes=[pltpu.SemaphoreType.DMA],
        ),
        out_shape=jax.ShapeDtypeStruct((n_slots, D_MODEL), F32),
        input_output_aliases={2: 0},
        compiler_params=_cparams(("arbitrary",)),
        name="scatter",
    )(dest, h2, jnp.zeros((n_slots, D_MODEL), F32))


def _moe_kernel(be_ref, used_ref, x_ref, w1_ref, w3_ref, w2_ref, y_ref, w1b, w3b, w2b):
    i = pl.program_id(0)

    @pl.when(i < used_ref[0])
    def _():
        @pl.when((i == 0) | (be_ref[i] != be_ref[jnp.maximum(i - 1, 0)]))
        def _():
            w1b[...] = w1_ref[...].astype(BF16)
            w3b[...] = w3_ref[...].astype(BF16)
            w2b[...] = w2_ref[...].astype(BF16)

        x = x_ref[...].astype(BF16)
        a1 = _dot(x, w1b[...])
        a3 = _dot(x, w3b[...])
        hm = (a1 * jax.nn.sigmoid(a1) * a3).astype(BF16)
        y_ref[...] = _dot(hm, w2b[...])

    @pl.when(i >= used_ref[0])
    def _():
        y_ref[...] = jnp.zeros_like(y_ref)


def _moe_call(block_expert, n_used, xs, w1, w3, w2):
    n_blocks = block_expert.shape[0]

    def wspec(shape):
        return pl.BlockSpec((None,) + shape, lambda i, be, used: (be[i], 0, 0))

    return pl.pallas_call(
        _moe_kernel,
        grid_spec=pltpu.PrefetchScalarGridSpec(
            num_scalar_prefetch=2,
            grid=(n_blocks,),
            in_specs=[
                pl.BlockSpec((MOE_BLOCK, D_MODEL), lambda i, be, used: (jnp.minimum(i, used[0] - 1), 0)),
                wspec((D_MODEL, D_EXPERT)),
                wspec((D_MODEL, D_EXPERT)),
                wspec((D_EXPERT, D_MODEL)),
            ],
            out_specs=pl.BlockSpec((MOE_BLOCK, D_MODEL), lambda i, be, used: (i, 0)),
            scratch_shapes=[
                pltpu.VMEM((D_MODEL, D_EXPERT), BF16),
                pltpu.VMEM((D_MODEL, D_EXPERT), BF16),
                pltpu.VMEM((D_EXPERT, D_MODEL), BF16),
            ],
        ),
        out_shape=jax.ShapeDtypeStruct(xs.shape, F32),
        compiler_params=_cparams(("arbitrary",)),
        name="moe",
    )(block_expert, n_used, xs, w1, w3, w2)


def _final_kernel(dest_ref, x1_ref, gate_ref, g2_ref, ng_ref, ys_hbm, o_ref, ybuf, sems):
    i = pl.program_id(0)
    last = pl.num_programs(0) - 1
    cur = i % 2

    def row(buf, t, kk, slot):
        return pltpu.make_async_copy(ys_hbm.at[pl.ds(slot, 1)], ybuf.at[buf, kk, pl.ds(t, 1)], sems.at[buf])

    def fetch(buf, tile, t):
        for kk in range(TOP_K):
            row(buf, t, kk, dest_ref[(tile * TM + t) * TOP_K + kk]).start()

    def wait_all(buf):
        def wait(t, c):
            for kk in range(TOP_K):
                row(buf, t, kk, 0).wait()
            return c
        lax.fori_loop(0, TM, wait, 0, unroll=DMA_UNROLL)

    @pl.when(i == 0)
    def _():
        def start(t, c):
            fetch(0, 0, t)
            return c
        lax.fori_loop(0, TM, start, 0, unroll=DMA_UNROLL)

    wait_all(cur)
    nxt = jnp.minimum(i + 1, last)
    for t in range(TM):
        fetch(1 - cur, nxt, t)

    gate = gate_ref[...]
    y = ybuf[cur, 0] * gate[:, 0:1] + ybuf[cur, 1] * gate[:, 1:2]
    x2 = x1_ref[...] + g2_ref[...] * y
    o_ref[...] = x2 * lax.rsqrt(jnp.mean(x2 * x2, axis=-1, keepdims=True) + NORM_EPS) * ng_ref[...]

    @pl.when(i == last)
    def _():
        wait_all(1 - cur)


def _final_call(dest, x1, ys, gates, mod_tab, final_g, t_len):
    n = x1.shape[0]
    tpl = t_len // TM
    return pl.pallas_call(
        _final_kernel,
        grid_spec=pltpu.PrefetchScalarGridSpec(
            num_scalar_prefetch=1,
            grid=(n // TM,),
            in_specs=[
                pl.BlockSpec((TM, D_MODEL), lambda i, dest: (i, 0)),
                pl.BlockSpec((TM, ROUTER_COLS), lambda i, dest: (i, 0)),
                pl.BlockSpec((None, 1, D_MODEL), lambda i, dest: ((i // tpl) * 6 + 5, 0, 0)),
                _resident((1, D_MODEL), lambda i, dest: (0, 0)),
                pl.BlockSpec(memory_space=pl.ANY),
            ],
            out_specs=pl.BlockSpec((TM, D_MODEL), lambda i, dest: (i, 0)),
            scratch_shapes=[pltpu.VMEM((2, TOP_K, TM, D_MODEL), F32), pltpu.SemaphoreType.DMA((2,))],
        ),
        out_shape=jax.ShapeDtypeStruct((n, D_MODEL), F32),
        compiler_params=_cparams(("arbitrary",)),
        name="final",
    )(dest, x1, gates, mod_tab, final_g.reshape(1, D_MODEL), ys)


def _pad_cols(w, width):
    return jnp.pad(w, ((0, 0), (0, width - w.shape[1])))


_ROPE_SWAP = np.concatenate([np.arange(16, 32), np.arange(0, 16), np.arange(48, 64), np.arange(32, 48)])


def _rope_tables(t_len):
    pos = jnp.arange(t_len)
    inv_freq = ROPE_THETA ** (-jnp.arange(0, ROPE_AXIS_DIM, 2, dtype=F32) / ROPE_AXIS_DIM)
    ang_r = (pos // GRID_W)[:, None].astype(F32) * inv_freq
    ang_c = (pos % GRID_W)[:, None].astype(F32) * inv_freq
    cos = jnp.concatenate([jnp.cos(ang_r)] * 2 + [jnp.cos(ang_c)] * 2, axis=1)
    sin = jnp.concatenate([-jnp.sin(ang_r), jnp.sin(ang_r), -jnp.sin(ang_c), jnp.sin(ang_c)], axis=1)
    cos = jnp.concatenate([jnp.ones((CTX_LEN, QK_ROPE_DIM), F32), cos], axis=0)
    sin = jnp.concatenate([jnp.zeros((CTX_LEN, QK_ROPE_DIM), F32), sin], axis=0)
    rows = cos.shape[0]
    z64 = jnp.zeros((rows, QK_ROPE_DIM), F32)
    ck = jnp.concatenate([cos, z64], axis=1)
    sk = jnp.concatenate([sin, z64], axis=1)
    q_scale = MLA_SCALE * math.log2(math.e)
    cq = q_scale * jnp.concatenate([jnp.ones((rows, QK_NOPE_DIM), F32), cos, z64], axis=1)
    sq = q_scale * jnp.concatenate([jnp.zeros((rows, QK_NOPE_DIM), F32), sin, z64], axis=1)
    return ck, sk, cq, sq


def _slot_tables(idx2, rank2, counts, n_tokens):
    n_blocks = (n_tokens * TOP_K + N_EXPERTS * (MOE_BLOCK - 1) + MOE_BLOCK - 1) // MOE_BLOCK
    padded = (counts + MOE_BLOCK - 1) // MOE_BLOCK * MOE_BLOCK
    pad_end = jnp.cumsum(padded)
    pad_start = pad_end - padded
    experts = jnp.arange(N_EXPERTS, dtype=jnp.int32)
    first = jnp.sum(jnp.where(idx2[..., None] == experts, pad_start, 0), axis=-1)
    dest = (first + rank2).reshape(-1).astype(jnp.int32)
    block_start = jnp.arange(n_blocks, dtype=jnp.int32) * MOE_BLOCK
    block_expert = jnp.minimum(jnp.sum(block_start[:, None] >= pad_end[None, :], axis=1), N_EXPERTS - 1)
    n_used = (pad_end[-1] // MOE_BLOCK).reshape(1)
    return dest, block_expert.astype(jnp.int32), n_used.astype(jnp.int32), n_blocks * MOE_BLOCK


def kernel(x, c, ctx, c_ctx, w_mod, b_mod, norm_attn_g, norm_ffn_g, w_in, shift_mu, q_norm_g, w_uq, kv_norm_g, w_ukv, decay_w0, decay_up, iclr_a0, iclr_up, gate_up, key_k, key_a, bonus_r_k, lnx_g, lnx_b, w_out, w_grp, b_grp, w_exp, b_exp, w1, w3, w2, final_norm_g):
    n_batch, t_len, _ = x.shape
    assert ctx.shape[1] == CTX_LEN == TM and t_len % TM == 0 and w_mod.shape[0] == 1
    tpb = (CTX_LEN + t_len) // TM
    n = n_batch * t_len

    c_rows = jnp.zeros((8, D_MODEL), F32).at[:n_batch].set(c).at[n_batch].set(c_ctx)
    mod_tab = _mod_call(c_rows, w_mod[0], b_mod[0]).reshape(8 * 6, 1, D_MODEL)

    wi = w_in[0]
    w_kr = wi[:, 768:MLA_IN]
    o = MLA_IN
    w_in_p = jnp.concatenate([
        wi[:, 0:768], _pad_cols(w_kr, LANES), _pad_cols(w_kr[:, _ROPE_SWAP], LANES),
        wi[:, o:o + COLS_RKV],
        _pad_cols(wi[:, o + COLS_RKV:o + COLS_RKV + DECAY_LORA], LANES),
        _pad_cols(wi[:, o + COLS_RKV + DECAY_LORA:o + COLS_RKV + DECAY_LORA + ICLR_LORA], LANES),
        _pad_cols(wi[:, o + COLS_RKV + DECAY_LORA + ICLR_LORA:], 2 * LANES),
    ], axis=1).astype(BF16)
    mu = shift_mu[0]
    mu_rkv = mu[:, 0:COLS_RKV]
    mu_lora = jnp.concatenate([
        _pad_cols(mu[:, COLS_RKV:COLS_RKV + DECAY_LORA], LANES),
        _pad_cols(mu[:, COLS_RKV + DECAY_LORA:COLS_RKV + DECAY_LORA + ICLR_LORA], LANES),
        _pad_cols(mu[:, COLS_RKV + DECAY_LORA + ICLR_LORA:], 2 * LANES)], axis=1)

    def lora_up(w):
        both = jnp.concatenate([w[0], w[1]], axis=1)
        return jnp.pad(both, ((0, LANES - both.shape[0]), (0, 0))).astype(BF16)

    gup = jnp.pad(gate_up[0], ((0, 2 * LANES - GATE_LORA), (0, 0))).astype(BF16)
    p_mla, r, k, v, lw, lr, g = _project_call(
        x, ctx, mod_tab, norm_attn_g[0], w_in_p, mu_rkv, mu_lora, lora_up(decay_up[0]), lora_up(iclr_up[0]), gup,
        decay_w0[0].reshape(1, -1), iclr_a0[0].reshape(1, -1), tpb)
    key_k2 = key_k[0].reshape(1, -1)
    key_a2 = key_a[0].reshape(1, -1)
    scan_rows = SCAN_SUB * CHUNK
    assert CTX_LEN % scan_rows == 0 and t_len % scan_rows == 0
    y_dirs = [_scan_call(rev, r, k, v, lw, lr, key_k2, key_a2, n_batch,
                         (CTX_LEN + t_len) // scan_rows, CTX_LEN // scan_rows) for rev in (False, True)]

    hd = QK_NOPE_DIM + QK_ROPE_DIM
    wq = w_uq[0].reshape(Q_LORA_RANK, MLA_HEADS, hd)
    zq = jnp.zeros((Q_LORA_RANK, MLA_HEADS, QK_ROPE_DIM), F32)
    wa = jnp.concatenate([wq, zq], axis=2).reshape(Q_LORA_RANK, -1).astype(BF16)
    wb = jnp.concatenate([jnp.zeros((Q_LORA_RANK, MLA_HEADS, QK_NOPE_DIM), F32),
                          wq[:, :, QK_NOPE_DIM:][:, :, _ROPE_SWAP], zq], axis=2
                         ).reshape(Q_LORA_RANK, -1).astype(BF16)
    wkv3 = w_ukv[0].reshape(KV_LORA_RANK, MLA_HEADS, QK_NOPE_DIM + V_HEAD_DIM)
    wk = wkv3[:, :, :QK_NOPE_DIM].reshape(KV_LORA_RANK, -1).astype(BF16)
    wvt = wkv3[:, :, QK_NOPE_DIM:].reshape(KV_LORA_RANK, -1).T.astype(BF16)
    q, kmat, vmat = _mla_prep_call(p_mla, _rope_tables(t_len), q_norm_g[0], kv_norm_g[0], wa, wb, wk, wvt,
                                   n_batch, tpb)
    attn = _attn_call(q, kmat, vmat, n_batch, t_len, tpb)

    w_router = _pad_cols(jnp.concatenate([w_grp[0], w_exp[0]], axis=1), ROUTER_COLS)
    w_router_hi = w_router.astype(BF16)
    w_router2 = jnp.concatenate([w_router_hi, (w_router - w_router_hi.astype(F32)).astype(BF16)], axis=1)
    b_router = _pad_cols(jnp.concatenate([b_grp[0], b_exp[0]]).reshape(1, -1), ROUTER_COLS)
    x1, h2, idx, gates, rank, counts = _mix_call(
        x, attn, y_dirs, r, k, v, lr, g, mod_tab, key_a2, bonus_r_k[0].reshape(1, -1),
        lnx_g[0].reshape(1, -1), lnx_b[0].reshape(1, -1), norm_ffn_g[0],
        w_out[0].astype(BF16), w_router2, b_router, tpb)

    dest, block_expert, n_used, n_slots = _slot_tables(
        idx[:, :TOP_K], rank[:, :TOP_K], counts[0, :N_EXPERTS].astype(jnp.int32), n)
    xs = _scatter_call(dest, h2, n_slots)
    ys = _moe_call(block_expert, n_used, xs, w1[0], w3[0], w2[0])
    out = _final_call(dest, x1, ys, gates, mod_tab, final_norm_g, t_len)
    return out.reshape(n_batch, t_len, D_MODEL)
```

```python
import functools
import math

import jax
import jax.numpy as jnp
import numpy as np
from jax import lax
from jax.experimental import pallas as pl
from jax.experimental.pallas import tpu as pltpu

F32 = jnp.float32
BF16 = jnp.bfloat16
HIGHEST = lax.Precision.HIGHEST

D_MODEL = 2048
CTX_LEN = 256
GRID_W = 64
NORM_EPS = 1e-6

MLA_HEADS = 8
QK_NOPE_DIM = 128
QK_ROPE_DIM = 64
V_HEAD_DIM = 128
Q_LORA_RANK = 512
KV_LORA_RANK = 256
MLA_WIDTH = MLA_HEADS * V_HEAD_DIM
MLA_SCALE = (QK_NOPE_DIM + QK_ROPE_DIM) ** -0.5
ROPE_THETA = 10000.0
ROPE_AXIS_DIM = QK_ROPE_DIM // 2
QK_PAD_DIM = 256
VT_ROWS = V_HEAD_DIM + 16

RWKV_HEAD_DIM = 64
RWKV_WIDTH = D_MODEL - MLA_WIDTH
RWKV_HEADS = RWKV_WIDTH // RWKV_HEAD_DIM
DECAY_LORA = 64
ICLR_LORA = 64
GATE_LORA = 160
LNX_EPS = 64e-5

N_GROUPS = 4
EXPERTS_PER_GROUP = 8
N_EXPERTS = N_GROUPS * EXPERTS_PER_GROUP
TOP_K = 2
D_EXPERT = 512
MOE_BLOCK = 256

MLA_IN = Q_LORA_RANK + KV_LORA_RANK + QK_ROPE_DIM
LANES = 128
TM = 256
HALO = 8
CHUNK = 64
PAIR = 2 * RWKV_HEAD_DIM
N_PAIRS = RWKV_WIDTH // PAIR
HEAD_GROUP = 256
PAIRS_PER_STEP = 8
SCAN_SUB = 4
ATTN_TQ = 2048
ATTN_TK = 768
MIX_SUB = 2
DMA_UNROLL = 8
VMEM_LIMIT = 56 * 1024 * 1024

COLS_MLA = 1024
COLS_RKV = 3 * RWKV_WIDTH
COLS_LORA = 512
COLS_IN = COLS_MLA + COLS_RKV + COLS_LORA
ROUTER_COLS = 128


def _cparams(sem):
    return pltpu.CompilerParams(dimension_semantics=sem, vmem_limit_bytes=VMEM_LIMIT)


def _resident(shape, index_map):
    return pl.BlockSpec(shape, index_map, pipeline_mode=pl.Buffered(1))


def _dot(a, b):
    return jnp.dot(a, b, preferred_element_type=F32)


def _dot_nt(a, b):
    return lax.dot_general(a, b, (((1,), (1,)), ((), ())), preferred_element_type=F32)


def _dot_tn(a, b):
    return lax.dot_general(a, b, (((0,), (0,)), ((), ())), preferred_element_type=F32)


def _split2(x):
    hi = x.astype(BF16)
    lo = (x - hi.astype(F32)).astype(BF16)
    return hi, lo


def _head_ones():
    row = lax.broadcasted_iota(jnp.int32, (HEAD_GROUP, HEAD_GROUP), 0)
    col = lax.broadcasted_iota(jnp.int32, (HEAD_GROUP, HEAD_GROUP), 1)
    return jnp.where(row // RWKV_HEAD_DIM == col // RWKV_HEAD_DIM, 1.0, 0.0).astype(BF16)


def _head_sum(x, ones_bd):
    rows = x.shape[0]
    n = x.shape[1] // HEAD_GROUP
    parts = [half[:, c * HEAD_GROUP:(c + 1) * HEAD_GROUP] for half in _split2(x) for c in range(n)]
    res = _dot(jnp.concatenate(parts, axis=0), ones_bd)
    return jnp.concatenate([res[c * rows:(c + 1) * rows] + res[(n + c) * rows:(n + c + 1) * rows]
                            for c in range(n)], axis=1)


def _split3(x):
    hi = x.astype(BF16)
    r1 = x - hi.astype(F32)
    mid = r1.astype(BF16)
    lo = (r1 - mid.astype(F32)).astype(BF16)
    return hi, mid, lo


def _mod_kernel(c_ref, w_ref, b_ref, o_ref):
    c = c_ref[...]
    s = c * jax.nn.sigmoid(c)
    o_ref[...] = jnp.dot(s, w_ref[...], preferred_element_type=F32, precision=HIGHEST) + b_ref[...]


def _mod_call(c_rows, w_mod, b_mod):
    n = w_mod.shape[1]
    tn = 1024
    return pl.pallas_call(
        _mod_kernel,
        grid=(n // tn,),
        in_specs=[
            pl.BlockSpec((8, D_MODEL), lambda i: (0, 0)),
            pl.BlockSpec((D_MODEL, tn), lambda i: (0, i)),
            pl.BlockSpec((1, tn), lambda i: (0, i)),
        ],
        out_specs=pl.BlockSpec((8, tn), lambda i: (0, i)),
        out_shape=jax.ShapeDtypeStruct((8, n), F32),
        compiler_params=_cparams(("arbitrary",)),
        name="mod",
    )(c_rows, w_mod, b_mod.reshape(1, n))


def _project_kernel(tpb, x_ref, xp_ref, xn_ref, ctx_ref, sh_ref, sc_ref, g_ref, w_ref, mu_ref, mul_ref,
                    wup_ref, aup_ref, gup_ref, w0_ref, a0_ref,
                    o_mla, r_o, k_o, v_o, lw_o, a_o, g_o):
    j = pl.program_id(0) % tpb
    is_ctx = j == 0
    zero_prev_row = jnp.where(j <= 1, 0, -1)
    zero_next_row = jnp.where(is_ctx | (j == tpb - 1), TM - 1, -1)

    xin = jnp.where(is_ctx, ctx_ref[...], x_ref[...])
    xe = jnp.concatenate([xp_ref[...], xin, xn_ref[...]], axis=0)
    ms = jnp.mean(xe * xe, axis=-1, keepdims=True)
    h = xe * lax.rsqrt(ms + NORM_EPS) * g_ref[...]
    h = h * (1.0 + sc_ref[...]) + sh_ref[...]
    o_mla[...] = _dot(h[HALO:HALO + TM].astype(BF16), w_ref[:, 0:COLS_MLA])
    hb = h.astype(BF16)
    ext = TM + 2 * HALO

    def shifted(pe, mu):
        main = pe[HALO:HALO + TM]
        row = lax.broadcasted_iota(jnp.int32, main.shape, 0)
        prev = jnp.where(row == zero_prev_row, 0.0, pltpu.roll(pe, 1, 0)[HALO:HALO + TM])
        nxt = jnp.where(row == zero_next_row, 0.0, pltpu.roll(pe, ext - 1, 0)[HALO:HALO + TM])
        return main + mu[0:1, :] * (prev - main) + mu[1:2, :] * (nxt - main)

    bounds = [COLS_MLA + c * RWKV_WIDTH for c in range(4)] + [COLS_IN]
    mus = [mu_ref[:, c * RWKV_WIDTH:(c + 1) * RWKV_WIDTH] for c in range(3)] + [mul_ref[...]]
    outs = (r_o, k_o, v_o)
    pe_next = _dot(hb, w_ref[:, bounds[0]:bounds[1]])
    for c in range(4):
        pe = pe_next
        if c + 1 < 4:
            pe_next = _dot(hb, w_ref[:, bounds[c + 1]:bounds[c + 2]])
        if c < 3:
            outs[c][...] = shifted(pe, mus[c])
        else:
            lo = shifted(pe, mus[c])
    wl = jnp.tanh(lo[:, 0:LANES]).astype(BF16)
    al = lo[:, LANES:2 * LANES].astype(BF16)
    gl = jax.nn.sigmoid(lo[:, 2 * LANES:4 * LANES]).astype(BF16)
    w_raw = w0_ref[...] + _dot(wl, wup_ref[...])
    lw_o[...] = -math.exp(-0.5) * jax.nn.sigmoid(w_raw)
    a_o[...] = jax.nn.sigmoid(a0_ref[...] + _dot(al, aup_ref[...]))
    g_o[...] = _dot(gl, gup_ref[...])


def _mod_row(i, tpb, n_batch):
    return jnp.where(i % tpb == 0, n_batch, i // tpb)


def _project_call(x, ctx, mod_tab, norm_g, w_in_p, mu_rkv, mu_lora, wup, aup, gup, w0, a0, tpb):
    n_batch, t_len, _ = x.shape
    nt = n_batch * tpb * TM
    sub = TM // HALO
    last_halo = t_len // HALO - 1
    w2 = 2 * RWKV_WIDTH

    def mod_spec(k):
        return pl.BlockSpec((None, 1, D_MODEL), lambda i: (_mod_row(i, tpb, n_batch) * 6 + k, 0, 0))

    def lat(i):
        return jnp.maximum(i % tpb - 1, 0)

    def tile_out(cols):
        return pl.BlockSpec((TM, cols), lambda i: (i, 0))

    return pl.pallas_call(
        functools.partial(_project_kernel, tpb),
        grid=(n_batch * tpb,),
        in_specs=[
            pl.BlockSpec((None, TM, D_MODEL), lambda i: (i // tpb, lat(i), 0)),
            pl.BlockSpec((None, HALO, D_MODEL), lambda i: (i // tpb, jnp.maximum(lat(i) * sub - 1, 0), 0)),
            pl.BlockSpec((None, HALO, D_MODEL), lambda i: (i // tpb, jnp.minimum((lat(i) + 1) * sub, last_halo), 0)),
            pl.BlockSpec((None, TM, D_MODEL), lambda i: (i // tpb, 0, 0)),
            mod_spec(0),
            mod_spec(1),
            _resident((1, D_MODEL), lambda i: (0, 0)),
            _resident((D_MODEL, COLS_IN), lambda i: (0, 0)),
            _resident((2, COLS_RKV), lambda i: (0, 0)),
            _resident((2, COLS_LORA), lambda i: (0, 0)),
            _resident((LANES, w2), lambda i: (0, 0)),
            _resident((LANES, w2), lambda i: (0, 0)),
            _resident((2 * LANES, RWKV_WIDTH), lambda i: (0, 0)),
            _resident((1, w2), lambda i: (0, 0)),
            _resident((1, w2), lambda i: (0, 0)),
        ],
        out_specs=[tile_out(COLS_MLA), tile_out(RWKV_WIDTH), tile_out(RWKV_WIDTH), tile_out(RWKV_WIDTH),
                   tile_out(w2), tile_out(w2), tile_out(RWKV_WIDTH)],
        out_shape=[jax.ShapeDtypeStruct((nt, cols), F32)
                   for cols in (COLS_MLA, RWKV_WIDTH, RWKV_WIDTH, RWKV_WIDTH, w2, w2, RWKV_WIDTH)],
        compiler_params=_cparams(("arbitrary",)),
        name="project",
    )(x, x, x, ctx, mod_tab, mod_tab, norm_g.reshape(1, D_MODEL), w_in_p,
      mu_rkv, mu_lora, wup, aup, gup, w0, a0)


def _stack_heads(x):
    lane = lax.broadcasted_iota(jnp.int32, x.shape, 1)
    zero = jnp.zeros_like(x)
    return jnp.concatenate([jnp.where(lane < RWKV_HEAD_DIM, x, zero),
                            jnp.where(lane >= RWKV_HEAD_DIM, x, zero)], axis=0)


def _unstack_heads(z):
    half = z.shape[0] // 2
    return z[:half] + z[half:]


def _scan_kernel(rev, r_ref, k_ref, v_ref, lw_ref, a_ref, kkey_ref, akey_ref, y_ref, s_scr):
    @pl.when(pl.program_id(2) == 0)
    def _():
        s_scr[...] = jnp.zeros_like(s_scr)

    c2 = 2 * CHUNK
    rows = SCAN_SUB * CHUNK
    tb = lax.broadcasted_iota(jnp.int32, (rows, rows), 0)
    ib = lax.broadcasted_iota(jnp.int32, (rows, rows), 1)
    upto = (ib >= tb) if rev else (ib <= tb)
    cum_mat = jnp.where((tb // CHUNK == ib // CHUNK) & upto, 1.0, 0.0).astype(BF16)

    row = lax.broadcasted_iota(jnp.int32, (c2, LANES), 0)
    col = lax.broadcasted_iota(jnp.int32, (c2, LANES), 1)
    t_idx = row % CHUNK
    i_idx = col % CHUNK
    before = (i_idx > t_idx) if rev else (i_idx < t_idx)
    keep = before | ((i_idx == t_idx) & (row >= CHUNK))
    same_head = (row // RWKV_HEAD_DIM) == (col // RWKV_HEAD_DIM)
    eye_f = jnp.where(row == col, 1.0, 0.0)

    zero = jnp.zeros((c2, LANES), F32)
    pairs = range(PAIRS_PER_STEP)
    units = [(c, p) for c in range(SCAN_SUB) for p in pairs]

    def unit(x, u):
        c, p = u
        return x[c * CHUNK:(c + 1) * CHUNK, p * PAIR:(p + 1) * PAIR]

    r = r_ref[...]
    k = k_ref[...]
    lw = lw_ref[...]
    lr = a_ref[...]
    vb = v_ref[...].astype(BF16)
    kraw = k * kkey_ref[...]
    w_hi, w_mid, w_lo = _split3(lw)
    lp = _dot(cum_mat, w_hi) + _dot(cum_mat, w_mid) + _dot(cum_mat, w_lo)
    kk = kraw * lax.rsqrt(_head_sum(kraw * kraw, _head_ones()) + 1e-12)
    b = kk * lr
    kd = k * (1.0 + (lr - 1.0) * akey_ref[...])
    last = 0 if rev else CHUNK - 1
    ltot_rows = [lp[c * CHUNK + last:c * CHUNK + last + 1, :] for c in range(SCAN_SUB)]
    ltot = jnp.concatenate([jnp.broadcast_to(t, (CHUNK, t.shape[1])) for t in ltot_rows], axis=0)
    e_neg = jnp.exp(-lp)
    e_rest = jnp.exp(ltot - lp)
    e_tot = [jnp.exp(t) for t in ltot_rows]
    at = -kk * jnp.exp(lp - lw)
    rt = r * jnp.exp(lp)
    at_b = at.astype(BF16)
    rt_b = rt.astype(BF16)
    bt_b = (b * e_neg).astype(BF16)
    kt_b = (kd * e_neg).astype(BF16)
    bh = (b * e_rest).astype(BF16)
    kh = (kd * e_rest).astype(BF16)

    sv = {u: _stack_heads(unit(vb, u)) for u in units}
    ar = {u: jnp.concatenate([unit(at_b, u), unit(rt_b, u)], axis=0) for u in units}
    ab = {u: jnp.where(keep, _dot_nt(ar[u], _stack_heads(unit(bt_b, u))), zero) for u in units}
    ak = {u: jnp.where(keep, _dot_nt(ar[u], _stack_heads(unit(kt_b, u))), zero) for u in units}
    a_rb = {u: ab[u][CHUNK:].astype(BF16) for u in units}
    akv = {u: _dot(ak[u].astype(BF16), sv[u]) for u in units}

    pw = {u: _stack_heads(ab[u][:CHUNK]) for u in units}
    tm = {u: eye_f + pw[u] for u in units}
    pw = {u: _dot(pw[u].astype(BF16), pw[u].astype(BF16)) for u in units}
    for _ in range(int(math.log2(CHUNK)) - 2):
        both = {u: _dot(jnp.concatenate([tm[u], pw[u]], axis=0).astype(BF16), pw[u].astype(BF16)) for u in units}
        tm = {u: tm[u] + both[u][:c2] for u in units}
        pw = {u: both[u][c2:] for u in units}
    tm = {u: tm[u] + _dot(tm[u].astype(BF16), pw[u].astype(BF16)) for u in units}
    t_p = {u: _unstack_heads(tm[u]).astype(BF16) for u in units}

    wg = {u: _dot(t_p[u], jnp.concatenate([_stack_heads(akv[u][:CHUNK].astype(BF16)),
                                           _stack_heads(unit(at_b, u))], axis=1)) for u in units}
    w_b = {u: wg[u][:, :LANES].astype(BF16) for u in units}
    g_b = {u: wg[u][:, LANES:].astype(BF16) for u in units}
    qz = {u: _dot(a_rb[u], jnp.concatenate([_stack_heads(g_b[u]), _stack_heads(w_b[u])], axis=1)) for u in units}
    gz = jnp.zeros((CHUNK, LANES), BF16)
    mn = {u: _dot_tn(jnp.concatenate([jnp.concatenate([w_b[u], g_b[u]], axis=1),
                                      jnp.concatenate([unit(vb, u), gz], axis=1)], axis=0),
                     jnp.concatenate([unit(bh, u), unit(kh, u)], axis=0)) for u in units}
    n_st = {u: jnp.where(same_head, mn[u][:c2], zero) for u in units}
    m_bd = {u: jnp.where(same_head, mn[u][c2:], zero).astype(BF16) for u in units}
    q_b = {u: (unit(rt, u) + qz[u][:, :LANES]).astype(BF16) for u in units}
    z = {u: qz[u][:, LANES:] + akv[u][CHUNK:] for u in units}

    state = [s_scr[p] for p in pairs]
    for c in (reversed(range(SCAN_SUB)) if rev else range(SCAN_SUB)):
        s_b = [state[p].astype(BF16) for p in pairs]
        for p in pairs:
            y_ref[c * CHUNK:(c + 1) * CHUNK, p * PAIR:(p + 1) * PAIR] = (
                _dot_nt(q_b[(c, p)], _stack_heads(s_b[p])) + z[(c, p)])
        state = [state[p] * e_tot[c][:, p * PAIR:(p + 1) * PAIR] + _dot(s_b[p], m_bd[(c, p)])
                 + _unstack_heads(n_st[(c, p)]) for p in pairs]
    for p in pairs:
        s_scr[p] = state[p]


def _scan_call(rev, r, k, v, lw, lr, key_k, key_a, n_batch, bpb, ctx_blocks):
    nt = r.shape[0]
    groups = N_PAIRS // PAIRS_PER_STEP
    gw = PAIRS_PER_STEP * PAIR
    rows = SCAN_SUB * CHUNK
    d = 1 if rev else 0

    def block_row(b, j):
        if rev:
            j = jnp.where(j < ctx_blocks, ctx_blocks - 1 - j, bpb + ctx_blocks - 1 - j)
        return b * bpb + j

    shared = pl.BlockSpec((rows, gw), lambda b, g, j: (block_row(b, j), g))
    per_dir = pl.BlockSpec((rows, gw), lambda b, g, j: (block_row(b, j), d * groups + g))
    keys = pl.BlockSpec((1, gw), lambda b, g, j: (0, g))
    return pl.pallas_call(
        functools.partial(_scan_kernel, rev),
        grid=(n_batch, groups, bpb),
        in_specs=[shared, shared, shared, per_dir, per_dir, keys, keys],
        out_specs=shared,
        out_shape=jax.ShapeDtypeStruct((nt, RWKV_WIDTH), F32),
        scratch_shapes=[pltpu.VMEM((PAIRS_PER_STEP, RWKV_HEAD_DIM, PAIR), F32)],
        compiler_params=_cparams(("arbitrary", "arbitrary", "arbitrary")),
        name="scan_bwd" if rev else "scan_fwd",
    )(r, k, v, lw, lr, key_k, key_a)


def _mla_prep_kernel(p_ref, ck_ref, sk_ref, cq_ref, sq_ref, qg_ref, kvg_ref, wa_ref, wb_ref, wkv_ref, wvt_ref,
                     q_o, k_o, v_o):
    cq = p_ref[:, 0:Q_LORA_RANK]
    cqn = (cq * lax.rsqrt(jnp.mean(cq * cq, axis=-1, keepdims=True) + NORM_EPS) * qg_ref[...]).astype(BF16)
    ckv = p_ref[:, Q_LORA_RANK:Q_LORA_RANK + KV_LORA_RANK]
    ckvn = (ckv * lax.rsqrt(jnp.mean(ckv * ckv, axis=-1, keepdims=True) + NORM_EPS) * kvg_ref[...]).astype(BF16)
    kr_a = p_ref[:, 768:896]
    kr_b = p_ref[:, 896:1024]
    k_rot = (kr_a * ck_ref[...] + kr_b * sk_ref[...]).astype(BF16)
    cos_q = cq_ref[...]
    sin_q = sq_ref[...]
    for h in range(MLA_HEADS):
        hs = slice(h * QK_PAD_DIM, (h + 1) * QK_PAD_DIM)
        q_o[:, hs] = (_dot(cqn, wa_ref[:, hs]) * cos_q + _dot(cqn, wb_ref[:, hs]) * sin_q).astype(BF16)
        k_o[:, h * QK_PAD_DIM:h * QK_PAD_DIM + QK_NOPE_DIM] = _dot(
            ckvn, wkv_ref[:, h * QK_NOPE_DIM:(h + 1) * QK_NOPE_DIM]).astype(BF16)
        k_o[:, h * QK_PAD_DIM + QK_NOPE_DIM:(h + 1) * QK_PAD_DIM] = k_rot
    v_t = _dot_nt(wvt_ref[...], ckvn)
    for h in range(MLA_HEADS):
        v_o[h, 0:V_HEAD_DIM, :] = v_t[h * V_HEAD_DIM:(h + 1) * V_HEAD_DIM, :].astype(BF16)
        v_o[h, V_HEAD_DIM:VT_ROWS, :] = jnp.ones((VT_ROWS - V_HEAD_DIM, TM), BF16)


def _mla_prep_call(p_mla, tabs, q_norm_g, kv_norm_g, wa, wb, wk, wvt, n_batch, tpb):
    nt = p_mla.shape[0]
    ck, sk, cq, sq = tabs
    qw = MLA_HEADS * QK_PAD_DIM
    return pl.pallas_call(
        _mla_prep_kernel,
        grid=(nt // TM,),
        in_specs=[
            pl.BlockSpec((TM, COLS_MLA), lambda i: (i, 0)),
            pl.BlockSpec((TM, LANES), lambda i: (i % tpb, 0)),
            pl.BlockSpec((TM, LANES), lambda i: (i % tpb, 0)),
            pl.BlockSpec((TM, QK_PAD_DIM), lambda i: (i % tpb, 0)),
            pl.BlockSpec((TM, QK_PAD_DIM), lambda i: (i % tpb, 0)),
            _resident((1, Q_LORA_RANK), lambda i: (0, 0)),
            _resident((1, KV_LORA_RANK), lambda i: (0, 0)),
            _resident((Q_LORA_RANK, qw), lambda i: (0, 0)),
            _resident((Q_LORA_RANK, qw), lambda i: (0, 0)),
            _resident((KV_LORA_RANK, MLA_WIDTH), lambda i: (0, 0)),
            _resident((MLA_WIDTH, KV_LORA_RANK), lambda i: (0, 0)),
        ],
        out_specs=[
            pl.BlockSpec((TM, qw), lambda i: (i, 0)),
            pl.BlockSpec((TM, qw), lambda i: (i, 0)),
            pl.BlockSpec((None, MLA_HEADS, VT_ROWS, TM), lambda i: (i // tpb, 0, 0, i % tpb)),
        ],
        out_shape=[
            jax.ShapeDtypeStruct((nt, qw), BF16),
            jax.ShapeDtypeStruct((nt, qw), BF16),
            jax.ShapeDtypeStruct((n_batch, MLA_HEADS, VT_ROWS, tpb * TM), BF16),
        ],
        compiler_params=_cparams(("arbitrary",)),
        name="mla_prep",
    )(p_mla, ck, sk, cq, sq, q_norm_g.reshape(1, -1), kv_norm_g.reshape(1, -1), wa, wb, wk, wvt)


def _attn_kernel(n_kv, *refs):
    q_refs, (k_ref, v_ref, o_ref) = refs[:-3], refs[-3:]
    qs = [q_ref[...] for q_ref in q_refs]
    chains = range(len(qs))

    m = [jnp.full((1, TM), -jnp.inf, F32) for _ in chains]
    acc = [jnp.zeros((VT_ROWS, TM), F32) for _ in chains]
    def scores(j):
        kj = k_ref[j * ATTN_TK:(j + 1) * ATTN_TK, :]
        return [_dot_nt(kj, qs[c]) for c in chains]

    s_next = scores(0)
    for j in range(n_kv):
        vj = v_ref[:, j * ATTN_TK:(j + 1) * ATTN_TK]
        s = s_next
        if j + 1 < n_kv:
            s_next = scores(j + 1)
        for c in chains:
            m_new = jnp.maximum(m[c], jnp.max(s[c], axis=0, keepdims=True))
            alpha = jnp.exp2(m[c] - m_new)
            p = jnp.exp2((s[c] - m_new).astype(BF16))
            acc[c] = alpha * acc[c] + _dot(vj, p)
            m[c] = m_new
    for c in chains:
        out = acc[c][0:V_HEAD_DIM, :] / acc[c][V_HEAD_DIM:V_HEAD_DIM + 1, :]
        o_ref[c * TM:(c + 1) * TM, :] = jnp.transpose(out).astype(o_ref.dtype)


def _attn_call(q, k, v, n_batch, t_len, tpb):
    rows_b = tpb * TM
    assert rows_b % ATTN_TK == 0 and t_len % ATTN_TQ == 0 and ATTN_TQ % TM == 0
    n_q = t_len // ATTN_TQ
    sub = ATTN_TQ // TM
    k3 = k.reshape(n_batch, rows_b, MLA_HEADS * QK_PAD_DIM)

    def q_spec(u):
        return pl.BlockSpec((TM, QK_PAD_DIM), lambda b, h, i: (b * tpb + 1 + i * sub + u, h))

    return pl.pallas_call(
        functools.partial(_attn_kernel, rows_b // ATTN_TK),
        grid=(n_batch, MLA_HEADS, n_q),
        in_specs=[q_spec(u) for u in range(sub)] + [
            pl.BlockSpec((None, rows_b, QK_PAD_DIM), lambda b, h, i: (b, 0, h)),
            pl.BlockSpec((None, None, VT_ROWS, rows_b), lambda b, h, i: (b, h, 0, 0)),
        ],
        out_specs=pl.BlockSpec((ATTN_TQ, V_HEAD_DIM), lambda b, h, i: (b * n_q + i, h)),
        out_shape=jax.ShapeDtypeStruct((n_batch * t_len, MLA_WIDTH), BF16),
        compiler_params=_cparams(("arbitrary", "arbitrary", "arbitrary")),
        name="attention",
    )(*([q] * sub), k3, v)


def _slot_rank(idx, run_ref):
    lane = lax.broadcasted_iota(jnp.int32, idx.shape, 1)
    oh0 = lane == idx[:, 0:1]
    oh1 = lane == idx[:, 1:2]
    both = jnp.where(oh0 | oh1, 1.0, 0.0)
    t_row = lax.broadcasted_iota(jnp.int32, (TM, TM), 0)
    t_col = lax.broadcasted_iota(jnp.int32, (TM, TM), 1)
    earlier = jnp.where(t_col < t_row, 1.0, 0.0).astype(BF16)
    seen = _dot(earlier, both.astype(BF16)) + run_ref[...]
    r0 = jnp.sum(jnp.where(oh0, seen, 0.0), axis=-1, keepdims=True)
    r1 = jnp.sum(jnp.where(oh1, seen, 0.0), axis=-1, keepdims=True)
    run_ref[...] = run_ref[...] + jnp.sum(both, axis=0, keepdims=True)
    return jnp.where(lane == 0, r0, jnp.where(lane == 1, r1, 0.0)).astype(jnp.int32)


def _route(logits):
    lane = lax.broadcasted_iota(jnp.int32, logits.shape, 1)
    neg = jnp.full_like(logits, -jnp.inf)
    big = jnp.full_like(lane, 2 ** 30)
    is_grp = lane < N_GROUPS
    gl = jnp.where(is_grp, logits, neg)
    ge = jnp.exp(gl - jnp.max(gl, axis=-1, keepdims=True))
    gp = ge / jnp.sum(ge, axis=-1, keepdims=True)
    g_val = jnp.max(gp, axis=-1, keepdims=True)
    g_idx = jnp.min(jnp.where(is_grp & (gp == g_val), lane, big), axis=-1, keepdims=True)
    e_lane = lane - N_GROUPS
    in_grp = (e_lane >= g_idx * EXPERTS_PER_GROUP) & (e_lane < (g_idx + 1) * EXPERTS_PER_GROUP)
    el = jnp.where(in_grp, logits, neg)
    ee = jnp.exp(el - jnp.max(el, axis=-1, keepdims=True))
    ep = ee / jnp.sum(ee, axis=-1, keepdims=True)
    v1 = jnp.max(ep, axis=-1, keepdims=True)
    i1 = jnp.min(jnp.where(in_grp & (ep == v1), lane, big), axis=-1, keepdims=True)
    rest = in_grp & (lane != i1)
    v2 = jnp.max(jnp.where(rest, ep, neg), axis=-1, keepdims=True)
    i2 = jnp.min(jnp.where(rest & (ep == v2), lane, big), axis=-1, keepdims=True)
    denom = v1 + v2
    idx = jnp.where(lane == 0, i1 - N_GROUPS, jnp.where(lane == 1, i2 - N_GROUPS, 0))
    gate = jnp.where(lane == 0, g_val * v1 / denom, jnp.where(lane == 1, g_val * v2 / denom, 0.0))
    return idx, gate


def _mix_kernel(x_ref, attn_ref, yf_ref, yb_ref, r_ref, k_ref, v_ref, af_ref, ab_ref, g_ref,
                g1_ref, sh2_ref, sc2_ref, akey_ref, rk_ref, lng_ref, lnb_ref, ng_ref,
                wo_ref, wr_ref, br_ref,
                x1_o, h2_o, idx_o, gate_o, rank_o, cnt_o, run_scr):
    @pl.when(pl.program_id(0) == 0)
    def _():
        run_scr[...] = jnp.zeros_like(run_scr)

    subs = range(MIX_SUB)
    rows = TM // MIX_SUB

    def part(ref, u):
        return ref[u * rows:(u + 1) * rows, :]

    ones_bd = _head_ones()
    inv = 1.0 / RWKV_HEAD_DIM
    y = [part(yf_ref, u) + part(yb_ref, u) for u in subs]
    mu = [_head_sum(y[u], ones_bd) * inv for u in subs]
    dy = [y[u] - mu[u] for u in subs]
    var = [_head_sum(dy[u] * dy[u], ones_bd) * inv for u in subs]
    k_sum = [part(k_ref, u) * (2.0 + (part(af_ref, u) + part(ab_ref, u) - 2.0) * akey_ref[...]) for u in subs]
    bonus = [_head_sum(part(r_ref, u) * k_sum[u] * rk_ref[...], ones_bd) * part(v_ref, u) for u in subs]
    yn = [dy[u] * lax.rsqrt(var[u] + LNX_EPS) * lng_ref[...] + lnb_ref[...] for u in subs]
    rw = [((yn[u] + bonus[u]) * part(g_ref, u)).astype(BF16) for u in subs]
    o = [_dot(part(attn_ref, u), wo_ref[0:MLA_WIDTH, :]) + _dot(rw[u], wo_ref[MLA_WIDTH:D_MODEL, :])
         for u in subs]
    x1 = [part(x_ref, u) + g1_ref[...] * o[u] for u in subs]
    h2 = [x1[u] * lax.rsqrt(jnp.mean(x1[u] * x1[u], axis=-1, keepdims=True) + NORM_EPS) * ng_ref[...]
          * (1.0 + sc2_ref[...]) + sh2_ref[...] for u in subs]
    h_hl = [_split2(h2[u]) for u in subs]
    both = [_dot(h_hl[u][0], wr_ref[...]) for u in subs]
    lo_hi = [_dot(h_hl[u][1], wr_ref[:, 0:ROUTER_COLS]) for u in subs]
    routed = [_route(both[u][:, 0:ROUTER_COLS] + both[u][:, ROUTER_COLS:2 * ROUTER_COLS] + lo_hi[u] + br_ref[...])
              for u in subs]
    for u in subs:
        sl = slice(u * rows, (u + 1) * rows)
        x1_o[sl, :] = x1[u]
        h2_o[sl, :] = h2[u]
        idx_o[sl, :] = routed[u][0]
        gate_o[sl, :] = routed[u][1]
    rank_o[...] = _slot_rank(jnp.concatenate([routed[u][0] for u in subs], axis=0), run_scr)
    cnt_o[...] = run_scr[...]


def _mix_call(x, attn, yscan, r, k, v, lr, g, mod_tab, key_a, bonus_rk, lnx_g, lnx_b, norm_g,
              w_out_b, w_router, b_router, tpb):
    n_batch, t_len, _ = x.shape
    tpl = t_len // TM
    n = n_batch * t_len

    def lat(i):
        return (i // tpl) * tpb + 1 + i % tpl

    def tok(cols, col_blk=0):
        return pl.BlockSpec((TM, cols), lambda i: (lat(i), col_blk))

    def mod_spec(kk):
        return pl.BlockSpec((None, 1, D_MODEL), lambda i: ((i // tpl) * 6 + kk, 0, 0))

    def vec(cols):
        return _resident((1, cols), lambda i: (0, 0))

    tile_out = pl.BlockSpec((TM, ROUTER_COLS), lambda i: (i, 0))
    return pl.pallas_call(
        _mix_kernel,
        grid=(n // TM,),
        in_specs=[
            pl.BlockSpec((None, TM, D_MODEL), lambda i: (i // tpl, i % tpl, 0)),
            pl.BlockSpec((TM, MLA_WIDTH), lambda i: (i, 0)),
            tok(RWKV_WIDTH), tok(RWKV_WIDTH),
            tok(RWKV_WIDTH), tok(RWKV_WIDTH), tok(RWKV_WIDTH),
            tok(RWKV_WIDTH, 0), tok(RWKV_WIDTH, 1), tok(RWKV_WIDTH),
            mod_spec(2), mod_spec(3), mod_spec(4),
            vec(RWKV_WIDTH), vec(RWKV_WIDTH), vec(RWKV_WIDTH), vec(RWKV_WIDTH), vec(D_MODEL),
            _resident((D_MODEL, D_MODEL), lambda i: (0, 0)),
            _resident((D_MODEL, 2 * ROUTER_COLS), lambda i: (0, 0)),
            vec(ROUTER_COLS),
        ],
        out_specs=[
            pl.BlockSpec((TM, D_MODEL), lambda i: (i, 0)),
            pl.BlockSpec((TM, D_MODEL), lambda i: (i, 0)),
            tile_out, tile_out, tile_out,
            pl.BlockSpec((1, ROUTER_COLS), lambda i: (0, 0)),
        ],
        out_shape=[
            jax.ShapeDtypeStruct((n, D_MODEL), F32),
            jax.ShapeDtypeStruct((n, D_MODEL), F32),
            jax.ShapeDtypeStruct((n, ROUTER_COLS), jnp.int32),
            jax.ShapeDtypeStruct((n, ROUTER_COLS), F32),
            jax.ShapeDtypeStruct((n, ROUTER_COLS), jnp.int32),
            jax.ShapeDtypeStruct((1, ROUTER_COLS), F32),
        ],
        scratch_shapes=[pltpu.VMEM((1, ROUTER_COLS), F32)],
        compiler_params=_cparams(("arbitrary",)),
        name="mix",
    )(x, attn, yscan[0], yscan[1], r, k, v, lr, lr, g, mod_tab, mod_tab, mod_tab,
      key_a, bonus_rk, lnx_g, lnx_b, norm_g.reshape(1, D_MODEL), w_out_b, w_router, b_router)


def _scatter_kernel(dest_ref, h_ref, init_hbm, xs_hbm, sem):
    del init_hbm
    base = pl.program_id(0) * (TM * TOP_K)

    def row(t, slot):
        return pltpu.make_async_copy(h_ref.at[pl.ds(t, 1)], xs_hbm.at[pl.ds(slot, 1)], sem)

    def start(t, c):
        for kk in range(TOP_K):
            row(t, dest_ref[base + t * TOP_K + kk]).start()
        return c
    lax.fori_loop(0, TM, start, 0, unroll=DMA_UNROLL)

    def wait(t, c):
        for _ in range(TOP_K):
            row(t, 0).wait()
        return c
    lax.fori_loop(0, TM, wait, 0, unroll=DMA_UNROLL)


def _scatter_call(dest, h2, n_slots):
    n = h2.shape[0]
    return pl.pallas_call(
        _scatter_kernel,
        grid_spec=pltpu.PrefetchScalarGridSpec(
            num_scalar_prefetch=1,
            grid=(n // TM,),
            in_specs=[
                pl.BlockSpec((TM, D_MODEL), lambda i, dest: (i, 0)),
                pl.BlockSpec(memory_space=pl.ANY),
            ],
            out_specs=pl.BlockSpec(memory_space=pl.ANY),
            scratch_shapes=[pltpu.SemaphoreType.DMA],
        ),
        out_shape=jax.ShapeDtypeStruct((n_slots, D_MODEL), F32),
        input_output_aliases={2: 0},
        compiler_params=_cparams(("arbitrary",)),
        name="scatter",
    )(dest, h2, jnp.zeros((n_slots, D_MODEL), F32))


def _moe_kernel(be_ref, used_ref, x_ref, w1_ref, w3_ref, w2_ref, y_ref, w1b, w3b, w2b):
    i = pl.program_id(0)

    @pl.when(i < used_ref[0])
    def _():
        @pl.when((i == 0) | (be_ref[i] != be_ref[jnp.maximum(i - 1, 0)]))
        def _():
            w1b[...] = w1_ref[...].astype(BF16)
            w3b[...] = w3_ref[...].astype(BF16)
            w2b[...] = w2_ref[...].astype(BF16)

        x = x_ref[...].astype(BF16)
        a1 = _dot(x, w1b[...])
        a3 = _dot(x, w3b[...])
        hm = (a1 * jax.nn.sigmoid(a1) * a3).astype(BF16)
        y_ref[...] = _dot(hm, w2b[...])

    @pl.when(i >= used_ref[0])
    def _():
        y_ref[...] = jnp.zeros_like(y_ref)


def _moe_call(block_expert, n_used, xs, w1, w3, w2):
    n_blocks = block_expert.shape[0]

    def wspec(shape):
        return pl.BlockSpec((None,) + shape, lambda i, be, used: (be[i], 0, 0))

    return pl.pallas_call(
        _moe_kernel,
        grid_spec=pltpu.PrefetchScalarGridSpec(
            num_scalar_prefetch=2,
            grid=(n_blocks,),
            in_specs=[
                pl.BlockSpec((MOE_BLOCK, D_MODEL), lambda i, be, used: (jnp.minimum(i, used[0] - 1), 0)),
                wspec((D_MODEL, D_EXPERT)),
                wspec((D_MODEL, D_EXPERT)),
                wspec((D_EXPERT, D_MODEL)),
            ],
            out_specs=pl.BlockSpec((MOE_BLOCK, D_MODEL), lambda i, be, used: (i, 0)),
            scratch_shapes=[
                pltpu.VMEM((D_MODEL, D_EXPERT), BF16),
                pltpu.VMEM((D_MODEL, D_EXPERT), BF16),
                pltpu.VMEM((D_EXPERT, D_MODEL), BF16),
            ],
        ),
        out_shape=jax.ShapeDtypeStruct(xs.shape, F32),
        compiler_params=_cparams(("arbitrary",)),
        name="moe",
    )(block_expert, n_used, xs, w1, w3, w2)


def _final_kernel(dest_ref, x1_ref, gate_ref, g2_ref, ng_ref, ys_hbm, o_ref, ybuf, sems):
    i = pl.program_id(0)
    last = pl.num_programs(0) - 1
    cur = i % 2

    def row(buf, t, kk, slot):
        return pltpu.make_async_copy(ys_hbm.at[pl.ds(slot, 1)], ybuf.at[buf, kk, pl.ds(t, 1)], sems.at[buf])

    def fetch(buf, tile, t):
        for kk in range(TOP_K):
            row(buf, t, kk, dest_ref[(tile * TM + t) * TOP_K + kk]).start()

    def wait_all(buf):
        def wait(t, c):
            for kk in range(TOP_K):
                row(buf, t, kk, 0).wait()
            return c
        lax.fori_loop(0, TM, wait, 0, unroll=DMA_UNROLL)

    @pl.when(i == 0)
    def _():
        def start(t, c):
            fetch(0, 0, t)
            return c
        lax.fori_loop(0, TM, start, 0, unroll=DMA_UNROLL)

    wait_all(cur)
    nxt = jnp.minimum(i + 1, last)
    for t in range(TM):
        fetch(1 - cur, nxt, t)

    gate = gate_ref[...]
    y = ybuf[cur, 0] * gate[:, 0:1] + ybuf[cur, 1] * gate[:, 1:2]
    x2 = x1_ref[...] + g2_ref[...] * y
    o_ref[...] = x2 * lax.rsqrt(jnp.mean(x2 * x2, axis=-1, keepdims=True) + NORM_EPS) * ng_ref[...]

    @pl.when(i == last)
    def _():
        wait_all(1 - cur)


def _final_call(dest, x1, ys, gates, mod_tab, final_g, t_len):
    n = x1.shape[0]
    tpl = t_len // TM
    return pl.pallas_call(
        _final_kernel,
        grid_spec=pltpu.PrefetchScalarGridSpec(
            num_scalar_prefetch=1,
            grid=(n // TM,),
            in_specs=[
                pl.BlockSpec((TM, D_MODEL), lambda i, dest: (i, 0)),
                pl.BlockSpec((TM, ROUTER_COLS), lambda i, dest: (i, 0)),
                pl.BlockSpec((None, 1, D_MODEL), lambda i, dest: ((i // tpl) * 6 + 5, 0, 0)),
                _resident((1, D_MODEL), lambda i, dest: (0, 0)),
                pl.BlockSpec(memory_space=pl.ANY),
            ],
            out_specs=pl.BlockSpec((TM, D_MODEL), lambda i, dest: (i, 0)),
            scratch_shapes=[pltpu.VMEM((2, TOP_K, TM, D_MODEL), F32), pltpu.SemaphoreType.DMA((2,))],
        ),
        out_shape=jax.ShapeDtypeStruct((n, D_MODEL), F32),
        compiler_params=_cparams(("arbitrary",)),
        name="final",
    )(dest, x1, gates, mod_tab, final_g.reshape(1, D_MODEL), ys)


def _pad_cols(w, width):
    return jnp.pad(w, ((0, 0), (0, width - w.shape[1])))


_ROPE_SWAP = np.concatenate([np.arange(16, 32), np.arange(0, 16), np.arange(48, 64), np.arange(32, 48)])


def _rope_tables(t_len):
    pos = jnp.arange(t_len)
    inv_freq = ROPE_THETA ** (-jnp.arange(0, ROPE_AXIS_DIM, 2, dtype=F32) / ROPE_AXIS_DIM)
    ang_r = (pos // GRID_W)[:, None].astype(F32) * inv_freq
    ang_c = (pos % GRID_W)[:, None].astype(F32) * inv_freq
    cos = jnp.concatenate([jnp.cos(ang_r)] * 2 + [jnp.cos(ang_c)] * 2, axis=1)
    sin = jnp.concatenate([-jnp.sin(ang_r), jnp.sin(ang_r), -jnp.sin(ang_c), jnp.sin(ang_c)], axis=1)
    cos = jnp.concatenate([jnp.ones((CTX_LEN, QK_ROPE_DIM), F32), cos], axis=0)
    sin = jnp.concatenate([jnp.zeros((CTX_LEN, QK_ROPE_DIM), F32), sin], axis=0)
    rows = cos.shape[0]
    z64 = jnp.zeros((rows, QK_ROPE_DIM), F32)
    ck = jnp.concatenate([cos, z64], axis=1)
    sk = jnp.concatenate([sin, z64], axis=1)
    q_scale = MLA_SCALE * math.log2(math.e)
    cq = q_scale * jnp.concatenate([jnp.ones((rows, QK_NOPE_DIM), F32), cos, z64], axis=1)
    sq = q_scale * jnp.concatenate([jnp.zeros((rows, QK_NOPE_DIM), F32), sin, z64], axis=1)
    return ck, sk, cq, sq


def _slot_tables(idx2, rank2, counts, n_tokens):
    n_blocks = (n_tokens * TOP_K + N_EXPERTS * (MOE_BLOCK - 1) + MOE_BLOCK - 1) // MOE_BLOCK
    padded = (counts + MOE_BLOCK - 1) // MOE_BLOCK * MOE_BLOCK
    pad_end = jnp.cumsum(padded)
    pad_start = pad_end - padded
    experts = jnp.arange(N_EXPERTS, dtype=jnp.int32)
    first = jnp.sum(jnp.where(idx2[..., None] == experts, pad_start, 0), axis=-1)
    dest = (first + rank2).reshape(-1).astype(jnp.int32)
    block_start = jnp.arange(n_blocks, dtype=jnp.int32) * MOE_BLOCK
    block_expert = jnp.minimum(jnp.sum(block_start[:, None] >= pad_end[None, :], axis=1), N_EXPERTS - 1)
    n_used = (pad_end[-1] // MOE_BLOCK).reshape(1)
    return dest, block_expert.astype(jnp.int32), n_used.astype(jnp.int32), n_blocks * MOE_BLOCK


def kernel(x, c, ctx, c_ctx, w_mod, b_mod, norm_attn_g, norm_ffn_g, w_in, shift_mu, q_norm_g, w_uq, kv_norm_g, w_ukv, decay_w0, decay_up, iclr_a0, iclr_up, gate_up, key_k, key_a, bonus_r_k, lnx_g, lnx_b, w_out, w_grp, b_grp, w_exp, b_exp, w1, w3, w2, final_norm_g):
    n_batch, t_len, _ = x.shape
    assert ctx.shape[1] == CTX_LEN == TM and t_len % TM == 0 and w_mod.shape[0] == 1
    tpb = (CTX_LEN + t_len) // TM
    n = n_batch * t_len

    c_rows = jnp.zeros((8, D_MODEL), F32).at[:n_batch].set(c).at[n_batch].set(c_ctx)
    mod_tab = _mod_call(c_rows, w_mod[0], b_mod[0]).reshape(8 * 6, 1, D_MODEL)

    wi = w_in[0]
    w_kr = wi[:, 768:MLA_IN]
    o = MLA_IN
    w_in_p = jnp.concatenate([
        wi[:, 0:768], _pad_cols(w_kr, LANES), _pad_cols(w_kr[:, _ROPE_SWAP], LANES),
        wi[:, o:o + COLS_RKV],
        _pad_cols(wi[:, o + COLS_RKV:o + COLS_RKV + DECAY_LORA], LANES),
        _pad_cols(wi[:, o + COLS_RKV + DECAY_LORA:o + COLS_RKV + DECAY_LORA + ICLR_LORA], LANES),
        _pad_cols(wi[:, o + COLS_RKV + DECAY_LORA + ICLR_LORA:], 2 * LANES),
    ], axis=1).astype(BF16)
    mu = shift_mu[0]
    mu_rkv = mu[:, 0:COLS_RKV]
    mu_lora = jnp.concatenate([
        _pad_cols(mu[:, COLS_RKV:COLS_RKV + DECAY_LORA], LANES),
        _pad_cols(mu[:, COLS_RKV + DECAY_LORA:COLS_RKV + DECAY_LORA + ICLR_LORA], LANES),
        _pad_cols(mu[:, COLS_RKV + DECAY_LORA + ICLR_LORA:], 2 * LANES)], axis=1)

    def lora_up(w):
        both = jnp.concatenate([w[0], w[1]], axis=1)
        return jnp.pad(both, ((0, LANES - both.shape[0]), (0, 0))).astype(BF16)

    gup = jnp.pad(gate_up[0], ((0, 2 * LANES - GATE_LORA), (0, 0))).astype(BF16)
    p_mla, r, k, v, lw, lr, g = _project_call(
        x, ctx, mod_tab, norm_attn_g[0], w_in_p, mu_rkv, mu_lora, lora_up(decay_up[0]), lora_up(iclr_up[0]), gup,
        decay_w0[0].reshape(1, -1), iclr_a0[0].reshape(1, -1), tpb)
    key_k2 = key_k[0].reshape(1, -1)
    key_a2 = key_a[0].reshape(1, -1)
    scan_rows = SCAN_SUB * CHUNK
    assert CTX_LEN % scan_rows == 0 and t_len % scan_rows == 0
    y_dirs = [_scan_call(rev, r, k, v, lw, lr, key_k2, key_a2, n_batch,
                         (CTX_LEN + t_len) // scan_rows, CTX_LEN // scan_rows) for rev in (False, True)]

    hd = QK_NOPE_DIM + QK_ROPE_DIM
    wq = w_uq[0].reshape(Q_LORA_RANK, MLA_HEADS, hd)
    zq = jnp.zeros((Q_LORA_RANK, MLA_HEADS, QK_ROPE_DIM), F32)
    wa = jnp.concatenate([wq, zq], axis=2).reshape(Q_LORA_RANK, -1).astype(BF16)
    wb = jnp.concatenate([jnp.zeros((Q_LORA_RANK, MLA_HEADS, QK_NOPE_DIM), F32),
                          wq[:, :, QK_NOPE_DIM:][:, :, _ROPE_SWAP], zq], axis=2
                         ).reshape(Q_LORA_RANK, -1).astype(BF16)
    wkv3 = w_ukv[0].reshape(KV_LORA_RANK, MLA_HEADS, QK_NOPE_DIM + V_HEAD_DIM)
    wk = wkv3[:, :, :QK_NOPE_DIM].reshape(KV_LORA_RANK, -1).astype(BF16)
    wvt = wkv3[:, :, QK_NOPE_DIM:].reshape(KV_LORA_RANK, -1).T.astype(BF16)
    q, kmat, vmat = _mla_prep_call(p_mla, _rope_tables(t_len), q_norm_g[0], kv_norm_g[0], wa, wb, wk, wvt,
                                   n_batch, tpb)
    attn = _attn_call(q, kmat, vmat, n_batch, t_len, tpb)

    w_router = _pad_cols(jnp.concatenate([w_grp[0], w_exp[0]], axis=1), ROUTER_COLS)
    w_router_hi = w_router.astype(BF16)
    w_router2 = jnp.concatenate([w_router_hi, (w_router - w_router_hi.astype(F32)).astype(BF16)], axis=1)
    b_router = _pad_cols(jnp.concatenate([b_grp[0], b_exp[0]]).reshape(1, -1), ROUTER_COLS)
    x1, h2, idx, gates, rank, counts = _mix_call(
        x, attn, y_dirs, r, k, v, lr, g, mod_tab, key_a2, bonus_r_k[0].reshape(1, -1),
        lnx_g[0].reshape(1, -1), lnx_b[0].reshape(1, -1), norm_ffn_g[0],
        w_out[0].astype(BF16), w_router2, b_router, tpb)

    dest, block_expert, n_used, n_slots = _slot_tables(
        idx[:, :TOP_K], rank[:, :TOP_K], counts[0, :N_EXPERTS].astype(jnp.int32), n)
    xs = _scatter_call(dest, h2, n_slots)
    ys = _moe_call(block_expert, n_used, xs, w1[0], w3[0], w2[0])
    out = _final_call(dest, x1, ys, gates, mod_tab, final_norm_g, t_len)
    return out.reshape(n_batch, t_len, D_MODEL)
```

```python
import functools
import math

import jax
import jax.numpy as jnp
import numpy as np
from jax import lax
from jax.experimental import pallas as pl
from jax.experimental.pallas import tpu as pltpu

F32 = jnp.float32
BF16 = jnp.bfloat16
HIGHEST = lax.Precision.HIGHEST

D_MODEL = 2048
CTX_LEN = 256
GRID_W = 64
NORM_EPS = 1e-6

MLA_HEADS = 8
QK_NOPE_DIM = 128
QK_ROPE_DIM = 64
V_HEAD_DIM = 128
Q_LORA_RANK = 512
KV_LORA_RANK = 256
MLA_WIDTH = MLA_HEADS * V_HEAD_DIM
MLA_SCALE = (QK_NOPE_DIM + QK_ROPE_DIM) ** -0.5
ROPE_THETA = 10000.0
ROPE_AXIS_DIM = QK_ROPE_DIM // 2
QK_PAD_DIM = 256
VT_ROWS = V_HEAD_DIM + 16

RWKV_HEAD_DIM = 64
RWKV_WIDTH = D_MODEL - MLA_WIDTH
RWKV_HEADS = RWKV_WIDTH // RWKV_HEAD_DIM
DECAY_LORA = 64
ICLR_LORA = 64
GATE_LORA = 160
LNX_EPS = 64e-5

N_GROUPS = 4
EXPERTS_PER_GROUP = 8
N_EXPERTS = N_GROUPS * EXPERTS_PER_GROUP
TOP_K = 2
D_EXPERT = 512
MOE_BLOCK = 256

MLA_IN = Q_LORA_RANK + KV_LORA_RANK + QK_ROPE_DIM
LANES = 128
TM = 256
HALO = 8
CHUNK = 64
PAIR = 2 * RWKV_HEAD_DIM
N_PAIRS = RWKV_WIDTH // PAIR
HEAD_GROUP = 256
PAIRS_PER_STEP = 8
SCAN_SUB = 4
ATTN_TQ = 2048
ATTN_TK = 768
MIX_SUB = 2
DMA_UNROLL = 8
VMEM_LIMIT = 56 * 1024 * 1024

COLS_MLA = 1024
COLS_RKV = 3 * RWKV_WIDTH
COLS_LORA = 512
COLS_IN = COLS_MLA + COLS_RKV + COLS_LORA
ROUTER_COLS = 128


def _cparams(sem):
    return pltpu.CompilerParams(dimension_semantics=sem, vmem_limit_bytes=VMEM_LIMIT)


def _resident(shape, index_map):
    return pl.BlockSpec(shape, index_map, pipeline_mode=pl.Buffered(1))


def _dot(a, b):
    return jnp.dot(a, b, preferred_element_type=F32)


def _dot_nt(a, b):
    return lax.dot_general(a, b, (((1,), (1,)), ((), ())), preferred_element_type=F32)


def _dot_tn(a, b):
    return lax.dot_general(a, b, (((0,), (0,)), ((), ())), preferred_element_type=F32)


def _split2(x):
    hi = x.astype(BF16)
    lo = (x - hi.astype(F32)).astype(BF16)
    return hi, lo


def _head_ones():
    row = lax.broadcasted_iota(jnp.int32, (HEAD_GROUP, HEAD_GROUP), 0)
    col = lax.broadcasted_iota(jnp.int32, (HEAD_GROUP, HEAD_GROUP), 1)
    return jnp.where(row // RWKV_HEAD_DIM == col // RWKV_HEAD_DIM, 1.0, 0.0).astype(BF16)


def _head_sum(x, ones_bd):
    rows = x.shape[0]
    n = x.shape[1] // HEAD_GROUP
    parts = [half[:, c * HEAD_GROUP:(c + 1) * HEAD_GROUP] for half in _split2(x) for c in range(n)]
    res = _dot(jnp.concatenate(parts, axis=0), ones_bd)
    return jnp.concatenate([res[c * rows:(c + 1) * rows] + res[(n + c) * rows:(n + c + 1) * rows]
                            for c in range(n)], axis=1)


def _split3(x):
    hi = x.astype(BF16)
    r1 = x - hi.astype(F32)
    mid = r1.astype(BF16)
    lo = (r1 - mid.astype(F32)).astype(BF16)
    return hi, mid, lo


def _mod_kernel(c_ref, w_ref, b_ref, o_ref):
    c = c_ref[...]
    s = c * jax.nn.sigmoid(c)
    o_ref[...] = jnp.dot(s, w_ref[...], preferred_element_type=F32, precision=HIGHEST) + b_ref[...]


def _mod_call(c_rows, w_mod, b_mod):
    n = w_mod.shape[1]
    tn = 1024
    return pl.pallas_call(
        _mod_kernel,
        grid=(n // tn,),
        in_specs=[
            pl.BlockSpec((8, D_MODEL), lambda i: (0, 0)),
            pl.BlockSpec((D_MODEL, tn), lambda i: (0, i)),
            pl.BlockSpec((1, tn), lambda i: (0, i)),
        ],
        out_specs=pl.BlockSpec((8, tn), lambda i: (0, i)),
        out_shape=jax.ShapeDtypeStruct((8, n), F32),
        compiler_params=_cparams(("arbitrary",)),
        name="mod",
    )(c_rows, w_mod, b_mod.reshape(1, n))


def _project_kernel(tpb, x_ref, xp_ref, xn_ref, ctx_ref, sh_ref, sc_ref, g_ref, w_ref, mu_ref, mul_ref,
                    wup_ref, aup_ref, gup_ref, w0_ref, a0_ref,
                    o_mla, r_o, k_o, v_o, lw_o, a_o, g_o):
    j = pl.program_id(0) % tpb
    is_ctx = j == 0
    zero_prev_row = jnp.where(j <= 1, 0, -1)
    zero_next_row = jnp.where(is_ctx | (j == tpb - 1), TM - 1, -1)

    xin = jnp.where(is_ctx, ctx_ref[...], x_ref[...])
    xe = jnp.concatenate([xp_ref[...], xin, xn_ref[...]], axis=0)
    ms = jnp.mean(xe * xe, axis=-1, keepdims=True)
    h = xe * lax.rsqrt(ms + NORM_EPS) * g_ref[...]
    h = h * (1.0 + sc_ref[...]) + sh_ref[...]
    o_mla[...] = _dot(h[HALO:HALO + TM].astype(BF16), w_ref[:, 0:COLS_MLA])
    hb = h.astype(BF16)
    ext = TM + 2 * HALO

    def shifted(pe, mu):
        main = pe[HALO:HALO + TM]
        row = lax.broadcasted_iota(jnp.int32, main.shape, 0)
        prev = jnp.where(row == zero_prev_row, 0.0, pltpu.roll(pe, 1, 0)[HALO:HALO + TM])
        nxt = jnp.where(row == zero_next_row, 0.0, pltpu.roll(pe, ext - 1, 0)[HALO:HALO + TM])
        return main + mu[0:1, :] * (prev - main) + mu[1:2, :] * (nxt - main)

    bounds = [COLS_MLA + c * RWKV_WIDTH for c in range(4)] + [COLS_IN]
    mus = [mu_ref[:, c * RWKV_WIDTH:(c + 1) * RWKV_WIDTH] for c in range(3)] + [mul_ref[...]]
    outs = (r_o, k_o, v_o)
    pe_next = _dot(hb, w_ref[:, bounds[0]:bounds[1]])
    for c in range(4):
        pe = pe_next
        if c + 1 < 4:
            pe_next = _dot(hb, w_ref[:, bounds[c + 1]:bounds[c + 2]])
        if c < 3:
            outs[c][...] = shifted(pe, mus[c])
        else:
            lo = shifted(pe, mus[c])
    wl = jnp.tanh(lo[:, 0:LANES]).astype(BF16)
    al = lo[:, LANES:2 * LANES].astype(BF16)
    gl = jax.nn.sigmoid(lo[:, 2 * LANES:4 * LANES]).astype(BF16)
    w_raw = w0_ref[...] + _dot(wl, wup_ref[...])
    lw_o[...] = -math.exp(-0.5) * jax.nn.sigmoid(w_raw)
    a_o[...] = jax.nn.sigmoid(a0_ref[...] + _dot(al, aup_ref[...]))
    g_o[...] = _dot(gl, gup_ref[...])


def _mod_row(i, tpb, n_batch):
    return jnp.where(i % tpb == 0, n_batch, i // tpb)


def _project_call(x, ctx, mod_tab, norm_g, w_in_p, mu_rkv, mu_lora, wup, aup, gup, w0, a0, tpb):
    n_batch, t_len, _ = x.shape
    nt = n_batch * tpb * TM
    sub = TM // HALO
    last_halo = t_len // HALO - 1
    w2 = 2 * RWKV_WIDTH

    def mod_spec(k):
        return pl.BlockSpec((None, 1, D_MODEL), lambda i: (_mod_row(i, tpb, n_batch) * 6 + k, 0, 0))

    def lat(i):
        return jnp.maximum(i % tpb - 1, 0)

    def tile_out(cols):
        return pl.BlockSpec((TM, cols), lambda i: (i, 0))

    return pl.pallas_call(
        functools.partial(_project_kernel, tpb),
        grid=(n_batch * tpb,),
        in_specs=[
            pl.BlockSpec((None, TM, D_MODEL), lambda i: (i // tpb, lat(i), 0)),
            pl.BlockSpec((None, HALO, D_MODEL), lambda i: (i // tpb, jnp.maximum(lat(i) * sub - 1, 0), 0)),
            pl.BlockSpec((None, HALO, D_MODEL), lambda i: (i // tpb, jnp.minimum((lat(i) + 1) * sub, last_halo), 0)),
            pl.BlockSpec((None, TM, D_MODEL), lambda i: (i // tpb, 0, 0)),
            mod_spec(0),
            mod_spec(1),
            _resident((1, D_MODEL), lambda i: (0, 0)),
            _resident((D_MODEL, COLS_IN), lambda i: (0, 0)),
            _resident((2, COLS_RKV), lambda i: (0, 0)),
            _resident((2, COLS_LORA), lambda i: (0, 0)),
            _resident((LANES, w2), lambda i: (0, 0)),
            _resident((LANES, w2), lambda i: (0, 0)),
            _resident((2 * LANES, RWKV_WIDTH), lambda i: (0, 0)),
            _resident((1, w2), lambda i: (0, 0)),
            _resident((1, w2), lambda i: (0, 0)),
        ],
        out_specs=[tile_out(COLS_MLA), tile_out(RWKV_WIDTH), tile_out(RWKV_WIDTH), tile_out(RWKV_WIDTH),
                   tile_out(w2), tile_out(w2), tile_out(RWKV_WIDTH)],
        out_shape=[jax.ShapeDtypeStruct((nt, cols), F32)
                   for cols in (COLS_MLA, RWKV_WIDTH, RWKV_WIDTH, RWKV_WIDTH, w2, w2, RWKV_WIDTH)],
        compiler_params=_cparams(("arbitrary",)),
        name="project",
    )(x, x, x, ctx, mod_tab, mod_tab, norm_g.reshape(1, D_MODEL), w_in_p,
      mu_rkv, mu_lora, wup, aup, gup, w0, a0)


def _stack_heads(x):
    lane = lax.broadcasted_iota(jnp.int32, x.shape, 1)
    zero = jnp.zeros_like(x)
    return jnp.concatenate([jnp.where(lane < RWKV_HEAD_DIM, x, zero),
                            jnp.where(lane >= RWKV_HEAD_DIM, x, zero)], axis=0)


def _unstack_heads(z):
    half = z.shape[0] // 2
    return z[:half] + z[half:]


def _scan_kernel(rev, r_ref, k_ref, v_ref, lw_ref, a_ref, kkey_ref, akey_ref, y_ref, s_scr):
    @pl.when(pl.program_id(2) == 0)
    def _():
        s_scr[...] = jnp.zeros_like(s_scr)

    c2 = 2 * CHUNK
    rows = SCAN_SUB * CHUNK
    tb = lax.broadcasted_iota(jnp.int32, (rows, rows), 0)
    ib = lax.broadcasted_iota(jnp.int32, (rows, rows), 1)
    upto = (ib >= tb) if rev else (ib <= tb)
    cum_mat = jnp.where((tb // CHUNK == ib // CHUNK) & upto, 1.0, 0.0).astype(BF16)

    row = lax.broadcasted_iota(jnp.int32, (c2, LANES), 0)
    col = lax.broadcasted_iota(jnp.int32, (c2, LANES), 1)
    t_idx = row % CHUNK
    i_idx = col % CHUNK
    before = (i_idx > t_idx) if rev else (i_idx < t_idx)
    keep = before | ((i_idx == t_idx) & (row >= CHUNK))
    same_head = (row // RWKV_HEAD_DIM) == (col // RWKV_HEAD_DIM)
    eye_f = jnp.where(row == col, 1.0, 0.0)

    zero = jnp.zeros((c2, LANES), F32)
    pairs = range(PAIRS_PER_STEP)
    units = [(c, p) for c in range(SCAN_SUB) for p in pairs]

    def unit(x, u):
        c, p = u
        return x[c * CHUNK:(c + 1) * CHUNK, p * PAIR:(p + 1) * PAIR]

    r = r_ref[...]
    k = k_ref[...]
    lw = lw_ref[...]
    lr = a_ref[...]
    vb = v_ref[...].astype(BF16)
    kraw = k * kkey_ref[...]
    w_hi, w_mid, w_lo = _split3(lw)
    lp = _dot(cum_mat, w_hi) + _dot(cum_mat, w_mid) + _dot(cum_mat, w_lo)
    kk = kraw * lax.rsqrt(_head_sum(kraw * kraw, _head_ones()) + 1e-12)
    b = kk * lr
    kd = k * (1.0 + (lr - 1.0) * akey_ref[...])
    last = 0 if rev else CHUNK - 1
    ltot_rows = [lp[c * CHUNK + last:c * CHUNK + last + 1, :] for c in range(SCAN_SUB)]
    ltot = jnp.concatenate([jnp.broadcast_to(t, (CHUNK, t.shape[1])) for t in ltot_rows], axis=0)
    e_neg = jnp.exp(-lp)
    e_rest = jnp.exp(ltot - lp)
    e_tot = [jnp.exp(t) for t in ltot_rows]
    at = -kk * jnp.exp(lp - lw)
    rt = r * jnp.exp(lp)
    at_b = at.astype(BF16)
    rt_b = rt.astype(BF16)
    bt_b = (b * e_neg).astype(BF16)
    kt_b = (kd * e_neg).astype(BF16)
    bh = (b * e_rest).astype(BF16)
    kh = (kd * e_rest).astype(BF16)

    sv = {u: _stack_heads(unit(vb, u)) for u in units}
    ar = {u: jnp.concatenate([unit(at_b, u), unit(rt_b, u)], axis=0) for u in units}
    ab = {u: jnp.where(keep, _dot_nt(ar[u], _stack_heads(unit(bt_b, u))), zero) for u in units}
    ak = {u: jnp.where(keep, _dot_nt(ar[u], _stack_heads(unit(kt_b, u))), zero) for u in units}
    a_rb = {u: ab[u][CHUNK:].astype(BF16) for u in units}
    akv = {u: _dot(ak[u].astype(BF16), sv[u]) for u in units}

    pw = {u: _stack_heads(ab[u][:CHUNK]) for u in units}
    tm = {u: eye_f + pw[u] for u in units}
    pw = {u: _dot(pw[u].astype(BF16), pw[u].astype(BF16)) for u in units}
    for _ in range(int(math.log2(CHUNK)) - 2):
        both = {u: _dot(jnp.concatenate([tm[u], pw[u]], axis=0).astype(BF16), pw[u].astype(BF16)) for u in units}
        tm = {u: tm[u] + both[u][:c2] for u in units}
        pw = {u: both[u][c2:] for u in units}
    tm = {u: tm[u] + _dot(tm[u].astype(BF16), pw[u].astype(BF16)) for u in units}
    t_p = {u: _unstack_heads(tm[u]).astype(BF16) for u in units}

    wg = {u: _dot(t_p[u], jnp.concatenate([_stack_heads(akv[u][:CHUNK].astype(BF16)),
                                           _stack_heads(unit(at_b, u))], axis=1)) for u in units}
    w_b = {u: wg[u][:, :LANES].astype(BF16) for u in units}
    g_b = {u: wg[u][:, LANES:].astype(BF16) for u in units}
    qz = {u: _dot(a_rb[u], jnp.concatenate([_stack_heads(g_b[u]), _stack_heads(w_b[u])], axis=1)) for u in units}
    gz = jnp.zeros((CHUNK, LANES), BF16)
    mn = {u: _dot_tn(jnp.concatenate([jnp.concatenate([w_b[u], g_b[u]], axis=1),
                                      jnp.concatenate([unit(vb, u), gz], axis=1)], axis=0),
                     jnp.concatenate([unit(bh, u), unit(kh, u)], axis=0)) for u in units}
    n_st = {u: jnp.where(same_head, mn[u][:c2], zero) for u in units}
    m_bd = {u: jnp.where(same_head, mn[u][c2:], zero).astype(BF16) for u in units}
    q_b = {u: (unit(rt, u) + qz[u][:, :LANES]).astype(BF16) for u in units}
    z = {u: qz[u][:, LANES:] + akv[u][CHUNK:] for u in units}

    state = [s_scr[p] for p in pairs]
    for c in (reversed(range(SCAN_SUB)) if rev else range(SCAN_SUB)):
        s_b = [state[p].astype(BF16) for p in pairs]
        for p in pairs:
            y_ref[c * CHUNK:(c + 1) * CHUNK, p * PAIR:(p + 1) * PAIR] = (
                _dot_nt(q_b[(c, p)], _stack_heads(s_b[p])) + z[(c, p)])
        state = [state[p] * e_tot[c][:, p * PAIR:(p + 1) * PAIR] + _dot(s_b[p], m_bd[(c, p)])
                 + _unstack_heads(n_st[(c, p)]) for p in pairs]
    for p in pairs:
        s_scr[p] = state[p]


def _scan_call(rev, r, k, v, lw, lr, key_k, key_a, n_batch, bpb, ctx_blocks):
    nt = r.shape[0]
    groups = N_PAIRS // PAIRS_PER_STEP
    gw = PAIRS_PER_STEP * PAIR
    rows = SCAN_SUB * CHUNK
    d = 1 if rev else 0

    def block_row(b, j):
        if rev:
            j = jnp.where(j < ctx_blocks, ctx_blocks - 1 - j, bpb + ctx_blocks - 1 - j)
        return b * bpb + j

    shared = pl.BlockSpec((rows, gw), lambda b, g, j: (block_row(b, j), g))
    per_dir = pl.BlockSpec((rows, gw), lambda b, g, j: (block_row(b, j), d * groups + g))
    keys = pl.BlockSpec((1, gw), lambda b, g, j: (0, g))
    return pl.pallas_call(
        functools.partial(_scan_kernel, rev),
        grid=(n_batch, groups, bpb),
        in_specs=[shared, shared, shared, per_dir, per_dir, keys, keys],
        out_specs=shared,
        out_shape=jax.ShapeDtypeStruct((nt, RWKV_WIDTH), F32),
        scratch_shapes=[pltpu.VMEM((PAIRS_PER_STEP, RWKV_HEAD_DIM, PAIR), F32)],
        compiler_params=_cparams(("arbitrary", "arbitrary", "arbitrary")),
        name="scan_bwd" if rev else "scan_fwd",
    )(r, k, v, lw, lr, key_k, key_a)


def _mla_prep_kernel(p_ref, ck_ref, sk_ref, cq_ref, sq_ref, qg_ref, kvg_ref, wa_ref, wb_ref, wkv_ref, wvt_ref,
                     q_o, k_o, v_o):
    cq = p_ref[:, 0:Q_LORA_RANK]
    cqn = (cq * lax.rsqrt(jnp.mean(cq * cq, axis=-1, keepdims=True) + NORM_EPS) * qg_ref[...]).astype(BF16)
    ckv = p_ref[:, Q_LORA_RANK:Q_LORA_RANK + KV_LORA_RANK]
    ckvn = (ckv * lax.rsqrt(jnp.mean(ckv * ckv, axis=-1, keepdims=True) + NORM_EPS) * kvg_ref[...]).astype(BF16)
    kr_a = p_ref[:, 768:896]
    kr_b = p_ref[:, 896:1024]
    k_rot = (kr_a * ck_ref[...] + kr_b * sk_ref[...]).astype(BF16)
    cos_q = cq_ref[...]
    sin_q = sq_ref[...]
    for h in range(MLA_HEADS):
        hs = slice(h * QK_PAD_DIM, (h + 1) * QK_PAD_DIM)
        q_o[:, hs] = (_dot(cqn, wa_ref[:, hs]) * cos_q + _dot(cqn, wb_ref[:, hs]) * sin_q).astype(BF16)
        k_o[:, h * QK_PAD_DIM:h * QK_PAD_DIM + QK_NOPE_DIM] = _dot(
            ckvn, wkv_ref[:, h * QK_NOPE_DIM:(h + 1) * QK_NOPE_DIM]).astype(BF16)
        k_o[:, h * QK_PAD_DIM + QK_NOPE_DIM:(h + 1) * QK_PAD_DIM] = k_rot
    v_t = _dot_nt(wvt_ref[...], ckvn)
    for h in range(MLA_HEADS):
        v_o[h, 0:V_HEAD_DIM, :] = v_t[h * V_HEAD_DIM:(h + 1) * V_HEAD_DIM, :].astype(BF16)
        v_o[h, V_HEAD_DIM:VT_ROWS, :] = jnp.ones((VT_ROWS - V_HEAD_DIM, TM), BF16)


def _mla_prep_call(p_mla, tabs, q_norm_g, kv_norm_g, wa, wb, wk, wvt, n_batch, tpb):
    nt = p_mla.shape[0]
    ck, sk, cq, sq = tabs
    qw = MLA_HEADS * QK_PAD_DIM
    return pl.pallas_call(
        _mla_prep_kernel,
        grid=(nt // TM,),
        in_specs=[
            pl.BlockSpec((TM, COLS_MLA), lambda i: (i, 0)),
            pl.BlockSpec((TM, LANES), lambda i: (i % tpb, 0)),
            pl.BlockSpec((TM, LANES), lambda i: (i % tpb, 0)),
            pl.BlockSpec((TM, QK_PAD_DIM), lambda i: (i % tpb, 0)),
            pl.BlockSpec((TM, QK_PAD_DIM), lambda i: (i % tpb, 0)),
            _resident((1, Q_LORA_RANK), lambda i: (0, 0)),
            _resident((1, KV_LORA_RANK), lambda i: (0, 0)),
            _resident((Q_LORA_RANK, qw), lambda i: (0, 0)),
            _resident((Q_LORA_RANK, qw), lambda i: (0, 0)),
            _resident((KV_LORA_RANK, MLA_WIDTH), lambda i: (0, 0)),
            _resident((MLA_WIDTH, KV_LORA_RANK), lambda i: (0, 0)),
        ],
        out_specs=[
            pl.BlockSpec((TM, qw), lambda i: (i, 0)),
            pl.BlockSpec((TM, qw), lambda i: (i, 0)),
            pl.BlockSpec((None, MLA_HEADS, VT_ROWS, TM), lambda i: (i // tpb, 0, 0, i % tpb)),
        ],
        out_shape=[
            jax.ShapeDtypeStruct((nt, qw), BF16),
            jax.ShapeDtypeStruct((nt, qw), BF16),
            jax.ShapeDtypeStruct((n_batch, MLA_HEADS, VT_ROWS, tpb * TM), BF16),
        ],
        compiler_params=_cparams(("arbitrary",)),
        name="mla_prep",
    )(p_mla, ck, sk, cq, sq, q_norm_g.reshape(1, -1), kv_norm_g.reshape(1, -1), wa, wb, wk, wvt)


def _attn_kernel(n_kv, *refs):
    q_refs, (k_ref, v_ref, o_ref) = refs[:-3], refs[-3:]
    qs = [q_ref[...] for q_ref in q_refs]
    chains = range(len(qs))

    m = [jnp.full((1, TM), -jnp.inf, F32) for _ in chains]
    acc = [jnp.zeros((VT_ROWS, TM), F32) for _ in chains]
    def scores(j):
        kj = k_ref[j * ATTN_TK:(j + 1) * ATTN_TK, :]
        return [_dot_nt(kj, qs[c]) for c in chains]

    s_next = scores(0)
    for j in range(n_kv):
        vj = v_ref[:, j * ATTN_TK:(j + 1) * ATTN_TK]
        s = s_next
        if j + 1 < n_kv:
            s_next = scores(j + 1)
        for c in chains:
            m_new = jnp.maximum(m[c], jnp.max(s[c], axis=0, keepdims=True))
            alpha = jnp.exp2(m[c] - m_new)
            p = jnp.exp2((s[c] - m_new).astype(BF16))
            acc[c] = alpha * acc[c] + _dot(vj, p)
            m[c] = m_new
    for c in chains:
        out = acc[c][0:V_HEAD_DIM, :] / acc[c][V_HEAD_DIM:V_HEAD_DIM + 1, :]
        o_ref[c * TM:(c + 1) * TM, :] = jnp.transpose(out).astype(o_ref.dtype)


def _attn_call(q, k, v, n_batch, t_len, tpb):
    rows_b = tpb * TM
    assert rows_b % ATTN_TK == 0 and t_len % ATTN_TQ == 0 and ATTN_TQ % TM == 0
    n_q = t_len // ATTN_TQ
    sub = ATTN_TQ // TM
    k3 = k.reshape(n_batch, rows_b, MLA_HEADS * QK_PAD_DIM)

    def q_spec(u):
        return pl.BlockSpec((TM, QK_PAD_DIM), lambda b, h, i: (b * tpb + 1 + i * sub + u, h))

    return pl.pallas_call(
        functools.partial(_attn_kernel, rows_b // ATTN_TK),
        grid=(n_batch, MLA_HEADS, n_q),
        in_specs=[q_spec(u) for u in range(sub)] + [
            pl.BlockSpec((None, rows_b, QK_PAD_DIM), lambda b, h, i: (b, 0, h)),
            pl.BlockSpec((None, None, VT_ROWS, rows_b), lambda b, h, i: (b, h, 0, 0)),
        ],
        out_specs=pl.BlockSpec((ATTN_TQ, V_HEAD_DIM), lambda b, h, i: (b * n_q + i, h)),
        out_shape=jax.ShapeDtypeStruct((n_batch * t_len, MLA_WIDTH), BF16),
        compiler_params=_cparams(("arbitrary", "arbitrary", "arbitrary")),
        name="attention",
    )(*([q] * sub), k3, v)


def _slot_rank(idx, run_ref):
    lane = lax.broadcasted_iota(jnp.int32, idx.shape, 1)
    oh0 = lane == idx[:, 0:1]
    oh1 = lane == idx[:, 1:2]
    both = jnp.where(oh0 | oh1, 1.0, 0.0)
    t_row = lax.broadcasted_iota(jnp.int32, (TM, TM), 0)
    t_col = lax.broadcasted_iota(jnp.int32, (TM, TM), 1)
    earlier = jnp.where(t_col < t_row, 1.0, 0.0).astype(BF16)
    seen = _dot(earlier, both.astype(BF16)) + run_ref[...]
    r0 = jnp.sum(jnp.where(oh0, seen, 0.0), axis=-1, keepdims=True)
    r1 = jnp.sum(jnp.where(oh1, seen, 0.0), axis=-1, keepdims=True)
    run_ref[...] = run_ref[...] + jnp.sum(both, axis=0, keepdims=True)
    return jnp.where(lane == 0, r0, jnp.where(lane == 1, r1, 0.0)).astype(jnp.int32)


def _route(logits):
    lane = lax.broadcasted_iota(jnp.int32, logits.shape, 1)
    neg = jnp.full_like(logits, -jnp.inf)
    big = jnp.full_like(lane, 2 ** 30)
    is_grp = lane < N_GROUPS
    gl = jnp.where(is_grp, logits, neg)
    ge = jnp.exp(gl - jnp.max(gl, axis=-1, keepdims=True))
    gp = ge / jnp.sum(ge, axis=-1, keepdims=True)
    g_val = jnp.max(gp, axis=-1, keepdims=True)
    g_idx = jnp.min(jnp.where(is_grp & (gp == g_val), lane, big), axis=-1, keepdims=True)
    e_lane = lane - N_GROUPS
    in_grp = (e_lane >= g_idx * EXPERTS_PER_GROUP) & (e_lane < (g_idx + 1) * EXPERTS_PER_GROUP)
    el = jnp.where(in_grp, logits, neg)
    ee = jnp.exp(el - jnp.max(el, axis=-1, keepdims=True))
    ep = ee / jnp.sum(ee, axis=-1, keepdims=True)
    v1 = jnp.max(ep, axis=-1, keepdims=True)
    i1 = jnp.min(jnp.where(in_grp & (ep == v1), lane, big), axis=-1, keepdims=True)
    rest = in_grp & (lane != i1)
    v2 = jnp.max(jnp.where(rest, ep, neg), axis=-1, keepdims=True)
    i2 = jnp.min(jnp.where(rest & (ep == v2), lane, big), axis=-1, keepdims=True)
    denom = v1 + v2
    idx = jnp.where(lane == 0, i1 - N_GROUPS, jnp.where(lane == 1, i2 - N_GROUPS, 0))
    gate = jnp.where(lane == 0, g_val * v1 / denom, jnp.where(lane == 1, g_val * v2 / denom, 0.0))
    return idx, gate


def _mix_kernel(x_ref, attn_ref, yf_ref, yb_ref, r_ref, k_ref, v_ref, af_ref, ab_ref, g_ref,
                g1_ref, sh2_ref, sc2_ref, akey_ref, rk_ref, lng_ref, lnb_ref, ng_ref,
                wo_ref, wr_ref, br_ref,
                x1_o, h2_o, idx_o, gate_o, rank_o, cnt_o, run_scr):
    @pl.when(pl.program_id(0) == 0)
    def _():
        run_scr[...] = jnp.zeros_like(run_scr)

    subs = range(MIX_SUB)
    rows = TM // MIX_SUB

    def part(ref, u):
        return ref[u * rows:(u + 1) * rows, :]

    ones_bd = _head_ones()
    inv = 1.0 / RWKV_HEAD_DIM
    y = [part(yf_ref, u) + part(yb_ref, u) for u in subs]
    mu = [_head_sum(y[u], ones_bd) * inv for u in subs]
    dy = [y[u] - mu[u] for u in subs]
    var = [_head_sum(dy[u] * dy[u], ones_bd) * inv for u in subs]
    k_sum = [part(k_ref, u) * (2.0 + (part(af_ref, u) + part(ab_ref, u) - 2.0) * akey_ref[...]) for u in subs]
    bonus = [_head_sum(part(r_ref, u) * k_sum[u] * rk_ref[...], ones_bd) * part(v_ref, u) for u in subs]
    yn = [dy[u] * lax.rsqrt(var[u] + LNX_EPS) * lng_ref[...] + lnb_ref[...] for u in subs]
    rw = [((yn[u] + bonus[u]) * part(g_ref, u)).astype(BF16) for u in subs]
    o = [_dot(part(attn_ref, u), wo_ref[0:MLA_WIDTH, :]) + _dot(rw[u], wo_ref[MLA_WIDTH:D_MODEL, :])
         for u in subs]
    x1 = [part(x_ref, u) + g1_ref[...] * o[u] for u in subs]
    h2 = [x1[u] * lax.rsqrt(jnp.mean(x1[u] * x1[u], axis=-1, keepdims=True) + NORM_EPS) * ng_ref[...]
          * (1.0 + sc2_ref[...]) + sh2_ref[...] for u in subs]
    h_hl = [_split2(h2[u]) for u in subs]
    both = [_dot(h_hl[u][0], wr_ref[...]) for u in subs]
    lo_hi = [_dot(h_hl[u][1], wr_ref[:, 0:ROUTER_COLS]) for u in subs]
    routed = [_route(both[u][:, 0:ROUTER_COLS] + both[u][:, ROUTER_COLS:2 * ROUTER_COLS] + lo_hi[u] + br_ref[...])
              for u in subs]
    for u in subs:
        sl = slice(u * rows, (u + 1) * rows)
        x1_o[sl, :] = x1[u]
        h2_o[sl, :] = h2[u]
        idx_o[sl, :] = routed[u][0]
        gate_o[sl, :] = routed[u][1]
    rank_o[...] = _slot_rank(jnp.concatenate([routed[u][0] for u in subs], axis=0), run_scr)
    cnt_o[...] = run_scr[...]


def _mix_call(x, attn, yscan, r, k, v, lr, g, mod_tab, key_a, bonus_rk, lnx_g, lnx_b, norm_g,
              w_out_b, w_router, b_router, tpb):
    n_batch, t_len, _ = x.shape
    tpl = t_len // TM
    n = n_batch * t_len

    def lat(i):
        return (i // tpl) * tpb + 1 + i % tpl

    def tok(cols, col_blk=0):
        return pl.BlockSpec((TM, cols), lambda i: (lat(i), col_blk))

    def mod_spec(kk):
        return pl.BlockSpec((None, 1, D_MODEL), lambda i: ((i // tpl) * 6 + kk, 0, 0))

    def vec(cols):
        return _resident((1, cols), lambda i: (0, 0))

    tile_out = pl.BlockSpec((TM, ROUTER_COLS), lambda i: (i, 0))
    return pl.pallas_call(
        _mix_kernel,
        grid=(n // TM,),
        in_specs=[
            pl.BlockSpec((None, TM, D_MODEL), lambda i: (i // tpl, i % tpl, 0)),
            pl.BlockSpec((TM, MLA_WIDTH), lambda i: (i, 0)),
            tok(RWKV_WIDTH), tok(RWKV_WIDTH),
            tok(RWKV_WIDTH), tok(RWKV_WIDTH), tok(RWKV_WIDTH),
            tok(RWKV_WIDTH, 0), tok(RWKV_WIDTH, 1), tok(RWKV_WIDTH),
            mod_spec(2), mod_spec(3), mod_spec(4),
            vec(RWKV_WIDTH), vec(RWKV_WIDTH), vec(RWKV_WIDTH), vec(RWKV_WIDTH), vec(D_MODEL),
            _resident((D_MODEL, D_MODEL), lambda i: (0, 0)),
            _resident((D_MODEL, 2 * ROUTER_COLS), lambda i: (0, 0)),
            vec(ROUTER_COLS),
        ],
        out_specs=[
            pl.BlockSpec((TM, D_MODEL), lambda i: (i, 0)),
            pl.BlockSpec((TM, D_MODEL), lambda i: (i, 0)),
            tile_out, tile_out, tile_out,
            pl.BlockSpec((1, ROUTER_COLS), lambda i: (0, 0)),
        ],
        out_shape=[
            jax.ShapeDtypeStruct((n, D_MODEL), F32),
            jax.ShapeDtypeStruct((n, D_MODEL), F32),
            jax.ShapeDtypeStruct((n, ROUTER_COLS), jnp.int32),
            jax.ShapeDtypeStruct((n, ROUTER_COLS), F32),
            jax.ShapeDtypeStruct((n, ROUTER_COLS), jnp.int32),
            jax.ShapeDtypeStruct((1, ROUTER_COLS), F32),
        ],
        scratch_shapes=[pltpu.VMEM((1, ROUTER_COLS), F32)],
        compiler_params=_cparams(("arbitrary",)),
        name="mix",
    )(x, attn, yscan[0], yscan[1], r, k, v, lr, lr, g, mod_tab, mod_tab, mod_tab,
      key_a, bonus_rk, lnx_g, lnx_b, norm_g.reshape(1, D_MODEL), w_out_b, w_router, b_router)


def _scatter_kernel(dest_ref, pad_lo_ref, pad_hi_ref, used_ref, h_ref, xs_hbm, zbuf, sem, zsem):
    base = pl.program_id(0) * (TM * TOP_K)
    n_blocks = xs_hbm.shape[0] // MOE_BLOCK

    def zero_row(slot):
        return pltpu.make_async_copy(zbuf.at[pl.ds(0, 1)], xs_hbm.at[pl.ds(slot, 1)], zsem)

    def zero_block(b):
        return pltpu.make_async_copy(zbuf, xs_hbm.at[pl.ds(pl.multiple_of(b * MOE_BLOCK, MOE_BLOCK), MOE_BLOCK)], zsem)

    @pl.when(pl.program_id(0) == 0)
    def _():
        zbuf[...] = jnp.zeros_like(zbuf)

        def each_pad_row(fn):
            def per_expert(e, c):
                return lax.fori_loop(pad_lo_ref[e], pad_hi_ref[e], fn, c)
            lax.fori_loop(0, N_EXPERTS, per_expert, 0)

        def start_row(r, c):
            zero_row(r).start()
            return c

        def wait_row(r, c):
            zero_row(0).wait()
            return c

        def start_block(b, c):
            zero_block(b).start()
            return c

        def wait_block(b, c):
            zero_block(0).wait()
            return c

        each_pad_row(start_row)
        lax.fori_loop(used_ref[0], n_blocks, start_block, 0)
        each_pad_row(wait_row)
        lax.fori_loop(used_ref[0], n_blocks, wait_block, 0)

    def row(t, slot):
        return pltpu.make_async_copy(h_ref.at[pl.ds(t, 1)], xs_hbm.at[pl.ds(slot, 1)], sem)

    def start(t, c):
        for kk in range(TOP_K):
            row(t, dest_ref[base + t * TOP_K + kk]).start()
        return c
    lax.fori_loop(0, TM, start, 0, unroll=DMA_UNROLL)

    def wait(t, c):
        for _ in range(TOP_K):
            row(t, 0).wait()
        return c
    lax.fori_loop(0, TM, wait, 0, unroll=DMA_UNROLL)


def _scatter_call(dest, pad_lo, pad_hi, n_used, h2, n_slots):
    n = h2.shape[0]
    return pl.pallas_call(
        _scatter_kernel,
        grid_spec=pltpu.PrefetchScalarGridSpec(
            num_scalar_prefetch=4,
            grid=(n // TM,),
            in_specs=[pl.BlockSpec((TM, D_MODEL), lambda i, *_: (i, 0))],
            out_specs=pl.BlockSpec(memory_space=pl.ANY),
            scratch_shapes=[pltpu.VMEM((MOE_BLOCK, D_MODEL), F32),
                            pltpu.SemaphoreType.DMA, pltpu.SemaphoreType.DMA],
        ),
        out_shape=jax.ShapeDtypeStruct((n_slots, D_MODEL), F32),
        compiler_params=_cparams(("arbitrary",)),
        name="scatter",
    )(dest, pad_lo, pad_hi, n_used, h2)


def _moe_kernel(be_ref, used_ref, x_ref, w1_ref, w3_ref, w2_ref, y_ref, w1b, w3b, w2b):
    i = pl.program_id(0)

    @pl.when(i < used_ref[0])
    def _():
        @pl.when((i == 0) | (be_ref[i] != be_ref[jnp.maximum(i - 1, 0)]))
        def _():
            w1b[...] = w1_ref[...].astype(BF16)
            w3b[...] = w3_ref[...].astype(BF16)
            w2b[...] = w2_ref[...].astype(BF16)

        x = x_ref[...].astype(BF16)
        a1 = _dot(x, w1b[...])
        a3 = _dot(x, w3b[...])
        hm = (a1 * jax.nn.sigmoid(a1) * a3).astype(BF16)
        y_ref[...] = _dot(hm, w2b[...])

    @pl.when(i >= used_ref[0])
    def _():
        y_ref[...] = jnp.zeros_like(y_ref)


def _moe_call(block_expert, n_used, xs, w1, w3, w2):
    n_blocks = block_expert.shape[0]

    def wspec(shape):
        return pl.BlockSpec((None,) + shape, lambda i, be, used: (be[i], 0, 0))

    return pl.pallas_call(
        _moe_kernel,
        grid_spec=pltpu.PrefetchScalarGridSpec(
            num_scalar_prefetch=2,
            grid=(n_blocks,),
            in_specs=[
                pl.BlockSpec((MOE_BLOCK, D_MODEL), lambda i, be, used: (jnp.minimum(i, used[0] - 1), 0)),
                wspec((D_MODEL, D_EXPERT)),
                wspec((D_MODEL, D_EXPERT)),
                wspec((D_EXPERT, D_MODEL)),
            ],
            out_specs=pl.BlockSpec((MOE_BLOCK, D_MODEL), lambda i, be, used: (i, 0)),
            scratch_shapes=[
                pltpu.VMEM((D_MODEL, D_EXPERT), BF16),
                pltpu.VMEM((D_MODEL, D_EXPERT), BF16),
                pltpu.VMEM((D_EXPERT, D_MODEL), BF16),
            ],
        ),
        out_shape=jax.ShapeDtypeStruct(xs.shape, F32),
        compiler_params=_cparams(("arbitrary",)),
        name="moe",
    )(block_expert, n_used, xs, w1, w3, w2)


def _final_kernel(dest_ref, x1_ref, gate_ref, g2_ref, ng_ref, ys_hbm, o_ref, ybuf, sems):
    i = pl.program_id(0)
    last = pl.num_programs(0) - 1
    cur = i % 2

    def row(buf, t, kk, slot):
        return pltpu.make_async_copy(ys_hbm.at[pl.ds(slot, 1)], ybuf.at[buf, kk, pl.ds(t, 1)], sems.at[buf])

    def fetch(buf, tile, t):
        for kk in range(TOP_K):
            row(buf, t, kk, dest_ref[(tile * TM + t) * TOP_K + kk]).start()

    def wait_all(buf):
        def wait(t, c):
            for kk in range(TOP_K):
                row(buf, t, kk, 0).wait()
            return c
        lax.fori_loop(0, TM, wait, 0, unroll=DMA_UNROLL)

    @pl.when(i == 0)
    def _():
        def start(t, c):
            fetch(0, 0, t)
            return c
        lax.fori_loop(0, TM, start, 0, unroll=DMA_UNROLL)

    wait_all(cur)
    nxt = jnp.minimum(i + 1, last)
    for t in range(TM):
        fetch(1 - cur, nxt, t)

    gate = gate_ref[...]
    y = ybuf[cur, 0] * gate[:, 0:1] + ybuf[cur, 1] * gate[:, 1:2]
    x2 = x1_ref[...] + g2_ref[...] * y
    o_ref[...] = x2 * lax.rsqrt(jnp.mean(x2 * x2, axis=-1, keepdims=True) + NORM_EPS) * ng_ref[...]

    @pl.when(i == last)
    def _():
        wait_all(1 - cur)


def _final_call(dest, x1, ys, gates, mod_tab, final_g, t_len):
    n = x1.shape[0]
    tpl = t_len // TM
    return pl.pallas_call(
        _final_kernel,
        grid_spec=pltpu.PrefetchScalarGridSpec(
            num_scalar_prefetch=1,
            grid=(n // TM,),
            in_specs=[
                pl.BlockSpec((TM, D_MODEL), lambda i, dest: (i, 0)),
                pl.BlockSpec((TM, ROUTER_COLS), lambda i, dest: (i, 0)),
                pl.BlockSpec((None, 1, D_MODEL), lambda i, dest: ((i // tpl) * 6 + 5, 0, 0)),
                _resident((1, D_MODEL), lambda i, dest: (0, 0)),
                pl.BlockSpec(memory_space=pl.ANY),
            ],
            out_specs=pl.BlockSpec((TM, D_MODEL), lambda i, dest: (i, 0)),
            scratch_shapes=[pltpu.VMEM((2, TOP_K, TM, D_MODEL), F32), pltpu.SemaphoreType.DMA((2,))],
        ),
        out_shape=jax.ShapeDtypeStruct((n, D_MODEL), F32),
        compiler_params=_cparams(("arbitrary",)),
        name="final",
    )(dest, x1, gates, mod_tab, final_g.reshape(1, D_MODEL), ys)


def _pad_cols(w, width):
    return jnp.pad(w, ((0, 0), (0, width - w.shape[1])))


_ROPE_SWAP = np.concatenate([np.arange(16, 32), np.arange(0, 16), np.arange(48, 64), np.arange(32, 48)])


def _rope_tables(t_len):
    pos = jnp.arange(t_len)
    inv_freq = ROPE_THETA ** (-jnp.arange(0, ROPE_AXIS_DIM, 2, dtype=F32) / ROPE_AXIS_DIM)
    ang_r = (pos // GRID_W)[:, None].astype(F32) * inv_freq
    ang_c = (pos % GRID_W)[:, None].astype(F32) * inv_freq
    cos = jnp.concatenate([jnp.cos(ang_r)] * 2 + [jnp.cos(ang_c)] * 2, axis=1)
    sin = jnp.concatenate([-jnp.sin(ang_r), jnp.sin(ang_r), -jnp.sin(ang_c), jnp.sin(ang_c)], axis=1)
    cos = jnp.concatenate([jnp.ones((CTX_LEN, QK_ROPE_DIM), F32), cos], axis=0)
    sin = jnp.concatenate([jnp.zeros((CTX_LEN, QK_ROPE_DIM), F32), sin], axis=0)
    rows = cos.shape[0]
    z64 = jnp.zeros((rows, QK_ROPE_DIM), F32)
    ck = jnp.concatenate([cos, z64], axis=1)
    sk = jnp.concatenate([sin, z64], axis=1)
    q_scale = MLA_SCALE * math.log2(math.e)
    cq = q_scale * jnp.concatenate([jnp.ones((rows, QK_NOPE_DIM), F32), cos, z64], axis=1)
    sq = q_scale * jnp.concatenate([jnp.zeros((rows, QK_NOPE_DIM), F32), sin, z64], axis=1)
    return ck, sk, cq, sq


def _slot_tables(idx2, rank2, counts, n_tokens):
    n_blocks = (n_tokens * TOP_K + N_EXPERTS * (MOE_BLOCK - 1) + MOE_BLOCK - 1) // MOE_BLOCK
    padded = (counts + MOE_BLOCK - 1) // MOE_BLOCK * MOE_BLOCK
    pad_end = jnp.cumsum(padded)
    pad_start = pad_end - padded
    experts = jnp.arange(N_EXPERTS, dtype=jnp.int32)
    first = jnp.sum(jnp.where(idx2[..., None] == experts, pad_start, 0), axis=-1)
    dest = (first + rank2).reshape(-1).astype(jnp.int32)
    block_start = jnp.arange(n_blocks, dtype=jnp.int32) * MOE_BLOCK
    block_expert = jnp.minimum(jnp.sum(block_start[:, None] >= pad_end[None, :], axis=1), N_EXPERTS - 1)
    n_used = (pad_end[-1] // MOE_BLOCK).reshape(1)
    pad_rows = ((pad_start + counts).astype(jnp.int32), pad_end.astype(jnp.int32))
    return dest, pad_rows, block_expert.astype(jnp.int32), n_used.astype(jnp.int32), n_blocks * MOE_BLOCK


def kernel(x, c, ctx, c_ctx, w_mod, b_mod, norm_attn_g, norm_ffn_g, w_in, shift_mu, q_norm_g, w_uq, kv_norm_g, w_ukv, decay_w0, decay_up, iclr_a0, iclr_up, gate_up, key_k, key_a, bonus_r_k, lnx_g, lnx_b, w_out, w_grp, b_grp, w_exp, b_exp, w1, w3, w2, final_norm_g):
    n_batch, t_len, _ = x.shape
    assert ctx.shape[1] == CTX_LEN == TM and t_len % TM == 0 and w_mod.shape[0] == 1
    tpb = (CTX_LEN + t_len) // TM
    n = n_batch * t_len

    c_rows = jnp.zeros((8, D_MODEL), F32).at[:n_batch].set(c).at[n_batch].set(c_ctx)
    mod_tab = _mod_call(c_rows, w_mod[0], b_mod[0]).reshape(8 * 6, 1, D_MODEL)

    wi = w_in[0]
    w_kr = wi[:, 768:MLA_IN]
    o = MLA_IN
    w_in_p = jnp.concatenate([
        wi[:, 0:768], _pad_cols(w_kr, LANES), _pad_cols(w_kr[:, _ROPE_SWAP], LANES),
        wi[:, o:o + COLS_RKV],
        _pad_cols(wi[:, o + COLS_RKV:o + COLS_RKV + DECAY_LORA], LANES),
        _pad_cols(wi[:, o + COLS_RKV + DECAY_LORA:o + COLS_RKV + DECAY_LORA + ICLR_LORA], LANES),
        _pad_cols(wi[:, o + COLS_RKV + DECAY_LORA + ICLR_LORA:], 2 * LANES),
    ], axis=1).astype(BF16)
    mu = shift_mu[0]
    mu_rkv = mu[:, 0:COLS_RKV]
    mu_lora = jnp.concatenate([
        _pad_cols(mu[:, COLS_RKV:COLS_RKV + DECAY_LORA], LANES),
        _pad_cols(mu[:, COLS_RKV + DECAY_LORA:COLS_RKV + DECAY_LORA + ICLR_LORA], LANES),
        _pad_cols(mu[:, COLS_RKV + DECAY_LORA + ICLR_LORA:], 2 * LANES)], axis=1)

    def lora_up(w):
        both = jnp.concatenate([w[0], w[1]], axis=1)
        return jnp.pad(both, ((0, LANES - both.shape[0]), (0, 0))).astype(BF16)

    gup = jnp.pad(gate_up[0], ((0, 2 * LANES - GATE_LORA), (0, 0))).astype(BF16)
    p_mla, r, k, v, lw, lr, g = _project_call(
        x, ctx, mod_tab, norm_attn_g[0], w_in_p, mu_rkv, mu_lora, lora_up(decay_up[0]), lora_up(iclr_up[0]), gup,
        decay_w0[0].reshape(1, -1), iclr_a0[0].reshape(1, -1), tpb)
    key_k2 = key_k[0].reshape(1, -1)
    key_a2 = key_a[0].reshape(1, -1)
    scan_rows = SCAN_SUB * CHUNK
    assert CTX_LEN % scan_rows == 0 and t_len % scan_rows == 0
    y_dirs = [_scan_call(rev, r, k, v, lw, lr, key_k2, key_a2, n_batch,
                         (CTX_LEN + t_len) // scan_rows, CTX_LEN // scan_rows) for rev in (False, True)]

    hd = QK_NOPE_DIM + QK_ROPE_DIM
    wq = w_uq[0].reshape(Q_LORA_RANK, MLA_HEADS, hd)
    zq = jnp.zeros((Q_LORA_RANK, MLA_HEADS, QK_ROPE_DIM), F32)
    wa = jnp.concatenate([wq, zq], axis=2).reshape(Q_LORA_RANK, -1).astype(BF16)
    wb = jnp.concatenate([jnp.zeros((Q_LORA_RANK, MLA_HEADS, QK_NOPE_DIM), F32),
                          wq[:, :, QK_NOPE_DIM:][:, :, _ROPE_SWAP], zq], axis=2
                         ).reshape(Q_LORA_RANK, -1).astype(BF16)
    wkv3 = w_ukv[0].reshape(KV_LORA_RANK, MLA_HEADS, QK_NOPE_DIM + V_HEAD_DIM)
    wk = wkv3[:, :, :QK_NOPE_DIM].reshape(KV_LORA_RANK, -1).astype(BF16)
    wvt = wkv3[:, :, QK_NOPE_DIM:].reshape(KV_LORA_RANK, -1).T.astype(BF16)
    q, kmat, vmat = _mla_prep_call(p_mla, _rope_tables(t_len), q_norm_g[0], kv_norm_g[0], wa, wb, wk, wvt,
                                   n_batch, tpb)
    attn = _attn_call(q, kmat, vmat, n_batch, t_len, tpb)

    w_router = _pad_cols(jnp.concatenate([w_grp[0], w_exp[0]], axis=1), ROUTER_COLS)
    w_router_hi = w_router.astype(BF16)
    w_router2 = jnp.concatenate([w_router_hi, (w_router - w_router_hi.astype(F32)).astype(BF16)], axis=1)
    b_router = _pad_cols(jnp.concatenate([b_grp[0], b_exp[0]]).reshape(1, -1), ROUTER_COLS)
    x1, h2, idx, gates, rank, counts = _mix_call(
        x, attn, y_dirs, r, k, v, lr, g, mod_tab, key_a2, bonus_r_k[0].reshape(1, -1),
        lnx_g[0].reshape(1, -1), lnx_b[0].reshape(1, -1), norm_ffn_g[0],
        w_out[0].astype(BF16), w_router2, b_router, tpb)

    dest, pad_rows, block_expert, n_used, n_slots = _slot_tables(
        idx[:, :TOP_K], rank[:, :TOP_K], counts[0, :N_EXPERTS].astype(jnp.int32), n)
    xs = _scatter_call(dest, pad_rows[0], pad_rows[1], n_used, h2, n_slots)
    ys = _moe_call(block_expert, n_used, xs, w1[0], w3[0], w2[0])
    out = _final_call(dest, x1, ys, gates, mod_tab, final_norm_g, t_len)
    return out.reshape(n_batch, t_len, D_MODEL)
```

```python
import functools
import math

import jax
import jax.numpy as jnp
import numpy as np
from jax import lax
from jax.experimental import pallas as pl
from jax.experimental.pallas import tpu as pltpu

F32 = jnp.float32
BF16 = jnp.bfloat16
HIGHEST = lax.Precision.HIGHEST

D_MODEL = 2048
CTX_LEN = 256
GRID_W = 64
NORM_EPS = 1e-6

MLA_HEADS = 8
QK_NOPE_DIM = 128
QK_ROPE_DIM = 64
V_HEAD_DIM = 128
Q_LORA_RANK = 512
KV_LORA_RANK = 256
MLA_WIDTH = MLA_HEADS * V_HEAD_DIM
MLA_SCALE = (QK_NOPE_DIM + QK_ROPE_DIM) ** -0.5
ROPE_THETA = 10000.0
ROPE_AXIS_DIM = QK_ROPE_DIM // 2
QK_PAD_DIM = 256
VT_ROWS = V_HEAD_DIM + 16

RWKV_HEAD_DIM = 64
RWKV_WIDTH = D_MODEL - MLA_WIDTH
RWKV_HEADS = RWKV_WIDTH // RWKV_HEAD_DIM
DECAY_LORA = 64
ICLR_LORA = 64
GATE_LORA = 160
LNX_EPS = 64e-5

N_GROUPS = 4
EXPERTS_PER_GROUP = 8
N_EXPERTS = N_GROUPS * EXPERTS_PER_GROUP
TOP_K = 2
D_EXPERT = 512
MOE_BLOCK = 256

MLA_IN = Q_LORA_RANK + KV_LORA_RANK + QK_ROPE_DIM
LANES = 128
TM = 256
HALO = 8
CHUNK = 64
PAIR = 2 * RWKV_HEAD_DIM
N_PAIRS = RWKV_WIDTH // PAIR
HEAD_GROUP = 256
PAIRS_PER_STEP = 8
SCAN_SUB = 4
ATTN_TQ = 2048
ATTN_TK = 768
MIX_SUB = 2
DMA_UNROLL = 8
VMEM_LIMIT = 56 * 1024 * 1024

COLS_MLA = 1024
COLS_RKV = 3 * RWKV_WIDTH
COLS_LORA = 512
COLS_IN = COLS_MLA + COLS_RKV + COLS_LORA
ROUTER_COLS = 128


def _cparams(sem):
    return pltpu.CompilerParams(dimension_semantics=sem, vmem_limit_bytes=VMEM_LIMIT)


def _resident(shape, index_map):
    return pl.BlockSpec(shape, index_map, pipeline_mode=pl.Buffered(1))


def _dot(a, b):
    return jnp.dot(a, b, preferred_element_type=F32)


def _dot_nt(a, b):
    return lax.dot_general(a, b, (((1,), (1,)), ((), ())), preferred_element_type=F32)


def _dot_tn(a, b):
    return lax.dot_general(a, b, (((0,), (0,)), ((), ())), preferred_element_type=F32)


def _split2(x):
    hi = x.astype(BF16)
    lo = (x - hi.astype(F32)).astype(BF16)
    return hi, lo


def _head_ones():
    row = lax.broadcasted_iota(jnp.int32, (HEAD_GROUP, HEAD_GROUP), 0)
    col = lax.broadcasted_iota(jnp.int32, (HEAD_GROUP, HEAD_GROUP), 1)
    return jnp.where(row // RWKV_HEAD_DIM == col // RWKV_HEAD_DIM, 1.0, 0.0).astype(BF16)


def _head_sum(x, ones_bd):
    rows = x.shape[0]
    n = x.shape[1] // HEAD_GROUP
    parts = [half[:, c * HEAD_GROUP:(c + 1) * HEAD_GROUP] for half in _split2(x) for c in range(n)]
    res = _dot(jnp.concatenate(parts, axis=0), ones_bd)
    return jnp.concatenate([res[c * rows:(c + 1) * rows] + res[(n + c) * rows:(n + c + 1) * rows]
                            for c in range(n)], axis=1)


def _split3(x):
    hi = x.astype(BF16)
    r1 = x - hi.astype(F32)
    mid = r1.astype(BF16)
    lo = (r1 - mid.astype(F32)).astype(BF16)
    return hi, mid, lo


def _mod_kernel(c_ref, w_ref, b_ref, o_ref):
    c = c_ref[...]
    s = c * jax.nn.sigmoid(c)
    o_ref[...] = jnp.dot(s, w_ref[...], preferred_element_type=F32, precision=HIGHEST) + b_ref[...]


def _mod_call(c_rows, w_mod, b_mod):
    n = w_mod.shape[1]
    tn = 1024
    return pl.pallas_call(
        _mod_kernel,
        grid=(n // tn,),
        in_specs=[
            pl.BlockSpec((8, D_MODEL), lambda i: (0, 0)),
            pl.BlockSpec((D_MODEL, tn), lambda i: (0, i)),
            pl.BlockSpec((1, tn), lambda i: (0, i)),
        ],
        out_specs=pl.BlockSpec((8, tn), lambda i: (0, i)),
        out_shape=jax.ShapeDtypeStruct((8, n), F32),
        compiler_params=_cparams(("arbitrary",)),
        name="mod",
    )(c_rows, w_mod, b_mod.reshape(1, n))


def _project_kernel(tpb, x_ref, xp_ref, xn_ref, ctx_ref, sh_ref, sc_ref, g_ref, w_ref, mu_ref, mul_ref,
                    wup_ref, aup_ref, gup_ref, w0_ref, a0_ref,
                    o_mla, r_o, k_o, v_o, lw_o, a_o, g_o):
    j = pl.program_id(0) % tpb
    is_ctx = j == 0
    zero_prev_row = jnp.where(j <= 1, 0, -1)
    zero_next_row = jnp.where(is_ctx | (j == tpb - 1), TM - 1, -1)

    xin = jnp.where(is_ctx, ctx_ref[...], x_ref[...])
    xe = jnp.concatenate([xp_ref[...], xin, xn_ref[...]], axis=0)
    ms = jnp.mean(xe * xe, axis=-1, keepdims=True)
    h = xe * lax.rsqrt(ms + NORM_EPS) * g_ref[...]
    h = h * (1.0 + sc_ref[...]) + sh_ref[...]
    o_mla[...] = _dot(h[HALO:HALO + TM].astype(BF16), w_ref[:, 0:COLS_MLA])
    hb = h.astype(BF16)
    ext = TM + 2 * HALO

    def shifted(pe, mu):
        main = pe[HALO:HALO + TM]
        row = lax.broadcasted_iota(jnp.int32, main.shape, 0)
        prev = jnp.where(row == zero_prev_row, 0.0, pltpu.roll(pe, 1, 0)[HALO:HALO + TM])
        nxt = jnp.where(row == zero_next_row, 0.0, pltpu.roll(pe, ext - 1, 0)[HALO:HALO + TM])
        return main + mu[0:1, :] * (prev - main) + mu[1:2, :] * (nxt - main)

    bounds = [COLS_MLA + c * RWKV_WIDTH for c in range(4)] + [COLS_IN]
    mus = [mu_ref[:, c * RWKV_WIDTH:(c + 1) * RWKV_WIDTH] for c in range(3)] + [mul_ref[...]]
    outs = (r_o, k_o, v_o)
    pe_next = _dot(hb, w_ref[:, bounds[0]:bounds[1]])
    for c in range(4):
        pe = pe_next
        if c + 1 < 4:
            pe_next = _dot(hb, w_ref[:, bounds[c + 1]:bounds[c + 2]])
        if c < 3:
            outs[c][...] = shifted(pe, mus[c])
        else:
            lo = shifted(pe, mus[c])
    wl = jnp.tanh(lo[:, 0:LANES]).astype(BF16)
    al = lo[:, LANES:2 * LANES].astype(BF16)
    gl = jax.nn.sigmoid(lo[:, 2 * LANES:4 * LANES]).astype(BF16)
    w_raw = w0_ref[...] + _dot(wl, wup_ref[...])
    lw_o[...] = -math.exp(-0.5) * jax.nn.sigmoid(w_raw)
    a_o[...] = jax.nn.sigmoid(a0_ref[...] + _dot(al, aup_ref[...]))
    g_o[...] = _dot(gl, gup_ref[...])


def _mod_row(i, tpb, n_batch):
    return jnp.where(i % tpb == 0, n_batch, i // tpb)


def _project_call(x, ctx, mod_tab, norm_g, w_in_p, mu_rkv, mu_lora, wup, aup, gup, w0, a0, tpb):
    n_batch, t_len, _ = x.shape
    nt = n_batch * tpb * TM
    sub = TM // HALO
    last_halo = t_len // HALO - 1
    w2 = 2 * RWKV_WIDTH

    def mod_spec(k):
        return pl.BlockSpec((None, 1, D_MODEL), lambda i: (_mod_row(i, tpb, n_batch) * 6 + k, 0, 0))

    def lat(i):
        return jnp.maximum(i % tpb - 1, 0)

    def tile_out(cols):
        return pl.BlockSpec((TM, cols), lambda i: (i, 0))

    return pl.pallas_call(
        functools.partial(_project_kernel, tpb),
        grid=(n_batch * tpb,),
        in_specs=[
            pl.BlockSpec((None, TM, D_MODEL), lambda i: (i // tpb, lat(i), 0)),
            pl.BlockSpec((None, HALO, D_MODEL), lambda i: (i // tpb, jnp.maximum(lat(i) * sub - 1, 0), 0)),
            pl.BlockSpec((None, HALO, D_MODEL), lambda i: (i // tpb, jnp.minimum((lat(i) + 1) * sub, last_halo), 0)),
            pl.BlockSpec((None, TM, D_MODEL), lambda i: (i // tpb, 0, 0)),
            mod_spec(0),
            mod_spec(1),
            _resident((1, D_MODEL), lambda i: (0, 0)),
            _resident((D_MODEL, COLS_IN), lambda i: (0, 0)),
            _resident((2, COLS_RKV), lambda i: (0, 0)),
            _resident((2, COLS_LORA), lambda i: (0, 0)),
            _resident((LANES, w2), lambda i: (0, 0)),
            _resident((LANES, w2), lambda i: (0, 0)),
            _resident((2 * LANES, RWKV_WIDTH), lambda i: (0, 0)),
            _resident((1, w2), lambda i: (0, 0)),
            _resident((1, w2), lambda i: (0, 0)),
        ],
        out_specs=[tile_out(COLS_MLA), tile_out(RWKV_WIDTH), tile_out(RWKV_WIDTH), tile_out(RWKV_WIDTH),
                   tile_out(w2), tile_out(w2), tile_out(RWKV_WIDTH)],
        out_shape=[jax.ShapeDtypeStruct((nt, cols), F32)
                   for cols in (COLS_MLA, RWKV_WIDTH, RWKV_WIDTH, RWKV_WIDTH, w2, w2, RWKV_WIDTH)],
        compiler_params=_cparams(("arbitrary",)),
        name="project",
    )(x, x, x, ctx, mod_tab, mod_tab, norm_g.reshape(1, D_MODEL), w_in_p,
      mu_rkv, mu_lora, wup, aup, gup, w0, a0)


def _stack_heads(x):
    lane = lax.broadcasted_iota(jnp.int32, x.shape, 1)
    zero = jnp.zeros_like(x)
    return jnp.concatenate([jnp.where(lane < RWKV_HEAD_DIM, x, zero),
                            jnp.where(lane >= RWKV_HEAD_DIM, x, zero)], axis=0)


def _unstack_heads(z):
    half = z.shape[0] // 2
    return z[:half] + z[half:]


def _scan_kernel(rev, r_ref, k_ref, v_ref, lw_ref, a_ref, kkey_ref, akey_ref, y_ref, s_scr):
    @pl.when(pl.program_id(2) == 0)
    def _():
        s_scr[...] = jnp.zeros_like(s_scr)

    c2 = 2 * CHUNK
    rows = SCAN_SUB * CHUNK
    tb = lax.broadcasted_iota(jnp.int32, (rows, rows), 0)
    ib = lax.broadcasted_iota(jnp.int32, (rows, rows), 1)
    upto = (ib >= tb) if rev else (ib <= tb)
    cum_mat = jnp.where((tb // CHUNK == ib // CHUNK) & upto, 1.0, 0.0).astype(BF16)

    row = lax.broadcasted_iota(jnp.int32, (c2, LANES), 0)
    col = lax.broadcasted_iota(jnp.int32, (c2, LANES), 1)
    t_idx = row % CHUNK
    i_idx = col % CHUNK
    before = (i_idx > t_idx) if rev else (i_idx < t_idx)
    keep = before | ((i_idx == t_idx) & (row >= CHUNK))
    same_head = (row // RWKV_HEAD_DIM) == (col // RWKV_HEAD_DIM)
    eye_f = jnp.where(row == col, 1.0, 0.0)

    zero = jnp.zeros((c2, LANES), F32)
    pairs = range(PAIRS_PER_STEP)
    units = [(c, p) for c in range(SCAN_SUB) for p in pairs]

    def unit(x, u):
        c, p = u
        return x[c * CHUNK:(c + 1) * CHUNK, p * PAIR:(p + 1) * PAIR]

    r = r_ref[...]
    k = k_ref[...]
    lw = lw_ref[...]
    lr = a_ref[...]
    vb = v_ref[...].astype(BF16)
    kraw = k * kkey_ref[...]
    w_hi, w_mid, w_lo = _split3(lw)
    lp = _dot(cum_mat, w_hi) + _dot(cum_mat, w_mid) + _dot(cum_mat, w_lo)
    kk = kraw * lax.rsqrt(_head_sum(kraw * kraw, _head_ones()) + 1e-12)
    b = kk * lr
    kd = k * (1.0 + (lr - 1.0) * akey_ref[...])
    last = 0 if rev else CHUNK - 1
    ltot_rows = [lp[c * CHUNK + last:c * CHUNK + last + 1, :] for c in range(SCAN_SUB)]
    ltot = jnp.concatenate([jnp.broadcast_to(t, (CHUNK, t.shape[1])) for t in ltot_rows], axis=0)
    e_neg = jnp.exp(-lp)
    e_rest = jnp.exp(ltot - lp)
    e_tot = [jnp.exp(t) for t in ltot_rows]
    at = -kk * jnp.exp(lp - lw)
    rt = r * jnp.exp(lp)
    at_b = at.astype(BF16)
    rt_b = rt.astype(BF16)
    bt_b = (b * e_neg).astype(BF16)
    kt_b = (kd * e_neg).astype(BF16)
    bh = (b * e_rest).astype(BF16)
    kh = (kd * e_rest).astype(BF16)

    sv = {u: _stack_heads(unit(vb, u)) for u in units}
    ar = {u: jnp.concatenate([unit(at_b, u), unit(rt_b, u)], axis=0) for u in units}
    ab = {u: jnp.where(keep, _dot_nt(ar[u], _stack_heads(unit(bt_b, u))), zero) for u in units}
    ak = {u: jnp.where(keep, _dot_nt(ar[u], _stack_heads(unit(kt_b, u))), zero) for u in units}
    a_rb = {u: ab[u][CHUNK:].astype(BF16) for u in units}
    akv = {u: _dot(ak[u].astype(BF16), sv[u]) for u in units}

    pw = {u: _stack_heads(ab[u][:CHUNK]) for u in units}
    tm = {u: eye_f + pw[u] for u in units}
    pw = {u: _dot(pw[u].astype(BF16), pw[u].astype(BF16)) for u in units}
    for _ in range(int(math.log2(CHUNK)) - 2):
        both = {u: _dot(jnp.concatenate([tm[u], pw[u]], axis=0).astype(BF16), pw[u].astype(BF16)) for u in units}
        tm = {u: tm[u] + both[u][:c2] for u in units}
        pw = {u: both[u][c2:] for u in units}
    tm = {u: tm[u] + _dot(tm[u].astype(BF16), pw[u].astype(BF16)) for u in units}
    t_p = {u: _unstack_heads(tm[u]).astype(BF16) for u in units}

    wg = {u: _dot(t_p[u], jnp.concatenate([_stack_heads(akv[u][:CHUNK].astype(BF16)),
                                           _stack_heads(unit(at_b, u))], axis=1)) for u in units}
    w_b = {u: wg[u][:, :LANES].astype(BF16) for u in units}
    g_b = {u: wg[u][:, LANES:].astype(BF16) for u in units}
    qz = {u: _dot(a_rb[u], jnp.concatenate([_stack_heads(g_b[u]), _stack_heads(w_b[u])], axis=1)) for u in units}
    gz = jnp.zeros((CHUNK, LANES), BF16)
    mn = {u: _dot_tn(jnp.concatenate([jnp.concatenate([w_b[u], g_b[u]], axis=1),
                                      jnp.concatenate([unit(vb, u), gz], axis=1)], axis=0),
                     jnp.concatenate([unit(bh, u), unit(kh, u)], axis=0)) for u in units}
    n_st = {u: jnp.where(same_head, mn[u][:c2], zero) for u in units}
    m_bd = {u: jnp.where(same_head, mn[u][c2:], zero).astype(BF16) for u in units}
    q_b = {u: (unit(rt, u) + qz[u][:, :LANES]).astype(BF16) for u in units}
    z = {u: qz[u][:, LANES:] + akv[u][CHUNK:] for u in units}

    state = [s_scr[p] for p in pairs]
    for c in (reversed(range(SCAN_SUB)) if rev else range(SCAN_SUB)):
        s_b = [state[p].astype(BF16) for p in pairs]
        for p in pairs:
            y_ref[c * CHUNK:(c + 1) * CHUNK, p * PAIR:(p + 1) * PAIR] = (
                _dot_nt(q_b[(c, p)], _stack_heads(s_b[p])) + z[(c, p)])
        state = [state[p] * e_tot[c][:, p * PAIR:(p + 1) * PAIR] + _dot(s_b[p], m_bd[(c, p)])
                 + _unstack_heads(n_st[(c, p)]) for p in pairs]
    for p in pairs:
        s_scr[p] = state[p]


def _scan_call(rev, r, k, v, lw, lr, key_k, key_a, n_batch, bpb, ctx_blocks):
    nt = r.shape[0]
    groups = N_PAIRS // PAIRS_PER_STEP
    gw = PAIRS_PER_STEP * PAIR
    rows = SCAN_SUB * CHUNK
    d = 1 if rev else 0

    def block_row(b, j):
        if rev:
            j = jnp.where(j < ctx_blocks, ctx_blocks - 1 - j, bpb + ctx_blocks - 1 - j)
        return b * bpb + j

    shared = pl.BlockSpec((rows, gw), lambda b, g, j: (block_row(b, j), g))
    per_dir = pl.BlockSpec((rows, gw), lambda b, g, j: (block_row(b, j), d * groups + g))
    keys = pl.BlockSpec((1, gw), lambda b, g, j: (0, g))
    return pl.pallas_call(
        functools.partial(_scan_kernel, rev),
        grid=(n_batch, groups, bpb),
        in_specs=[shared, shared, shared, per_dir, per_dir, keys, keys],
        out_specs=shared,
        out_shape=jax.ShapeDtypeStruct((nt, RWKV_WIDTH), F32),
        scratch_shapes=[pltpu.VMEM((PAIRS_PER_STEP, RWKV_HEAD_DIM, PAIR), F32)],
        compiler_params=_cparams(("arbitrary", "arbitrary", "arbitrary")),
        name="scan_bwd" if rev else "scan_fwd",
    )(r, k, v, lw, lr, key_k, key_a)


def _mla_prep_kernel(p_ref, ck_ref, sk_ref, cq_ref, sq_ref, qg_ref, kvg_ref, wa_ref, wb_ref, wkv_ref, wvt_ref,
                     q_o, k_o, v_o):
    cq = p_ref[:, 0:Q_LORA_RANK]
    cqn = (cq * lax.rsqrt(jnp.mean(cq * cq, axis=-1, keepdims=True) + NORM_EPS) * qg_ref[...]).astype(BF16)
    ckv = p_ref[:, Q_LORA_RANK:Q_LORA_RANK + KV_LORA_RANK]
    ckvn = (ckv * lax.rsqrt(jnp.mean(ckv * ckv, axis=-1, keepdims=True) + NORM_EPS) * kvg_ref[...]).astype(BF16)
    kr_a = p_ref[:, 768:896]
    kr_b = p_ref[:, 896:1024]
    k_rot = (kr_a * ck_ref[...] + kr_b * sk_ref[...]).astype(BF16)
    cos_q = cq_ref[...]
    sin_q = sq_ref[...]
    for h in range(MLA_HEADS):
        hs = slice(h * QK_PAD_DIM, (h + 1) * QK_PAD_DIM)
        q_o[:, hs] = (_dot(cqn, wa_ref[:, hs]) * cos_q + _dot(cqn, wb_ref[:, hs]) * sin_q).astype(BF16)
        k_o[:, h * QK_PAD_DIM:h * QK_PAD_DIM + QK_NOPE_DIM] = _dot(
            ckvn, wkv_ref[:, h * QK_NOPE_DIM:(h + 1) * QK_NOPE_DIM]).astype(BF16)
        k_o[:, h * QK_PAD_DIM + QK_NOPE_DIM:(h + 1) * QK_PAD_DIM] = k_rot
    v_t = _dot_nt(wvt_ref[...], ckvn)
    for h in range(MLA_HEADS):
        v_o[h, 0:V_HEAD_DIM, :] = v_t[h * V_HEAD_DIM:(h + 1) * V_HEAD_DIM, :].astype(BF16)
        v_o[h, V_HEAD_DIM:VT_ROWS, :] = jnp.ones((VT_ROWS - V_HEAD_DIM, TM), BF16)


def _mla_prep_call(p_mla, tabs, q_norm_g, kv_norm_g, wa, wb, wk, wvt, n_batch, tpb):
    nt = p_mla.shape[0]
    ck, sk, cq, sq = tabs
    qw = MLA_HEADS * QK_PAD_DIM
    return pl.pallas_call(
        _mla_prep_kernel,
        grid=(nt // TM,),
        in_specs=[
            pl.BlockSpec((TM, COLS_MLA), lambda i: (i, 0)),
            pl.BlockSpec((TM, LANES), lambda i: (i % tpb, 0)),
            pl.BlockSpec((TM, LANES), lambda i: (i % tpb, 0)),
            pl.BlockSpec((TM, QK_PAD_DIM), lambda i: (i % tpb, 0)),
            pl.BlockSpec((TM, QK_PAD_DIM), lambda i: (i % tpb, 0)),
            _resident((1, Q_LORA_RANK), lambda i: (0, 0)),
            _resident((1, KV_LORA_RANK), lambda i: (0, 0)),
            _resident((Q_LORA_RANK, qw), lambda i: (0, 0)),
            _resident((Q_LORA_RANK, qw), lambda i: (0, 0)),
            _resident((KV_LORA_RANK, MLA_WIDTH), lambda i: (0, 0)),
            _resident((MLA_WIDTH, KV_LORA_RANK), lambda i: (0, 0)),
        ],
        out_specs=[
            pl.BlockSpec((TM, qw), lambda i: (i, 0)),
            pl.BlockSpec((TM, qw), lambda i: (i, 0)),
            pl.BlockSpec((None, MLA_HEADS, VT_ROWS, TM), lambda i: (i // tpb, 0, 0, i % tpb)),
        ],
        out_shape=[
            jax.ShapeDtypeStruct((nt, qw), BF16),
            jax.ShapeDtypeStruct((nt, qw), BF16),
            jax.ShapeDtypeStruct((n_batch, MLA_HEADS, VT_ROWS, tpb * TM), BF16),
        ],
        compiler_params=_cparams(("arbitrary",)),
        name="mla_prep",
    )(p_mla, ck, sk, cq, sq, q_norm_g.reshape(1, -1), kv_norm_g.reshape(1, -1), wa, wb, wk, wvt)


def _attn_kernel(n_kv, *refs):
    q_refs, (k_ref, v_ref, o_ref) = refs[:-3], refs[-3:]
    qs = [q_ref[...] for q_ref in q_refs]
    chains = range(len(qs))

    m = [jnp.full((1, TM), -jnp.inf, F32) for _ in chains]
    acc = [jnp.zeros((VT_ROWS, TM), F32) for _ in chains]
    def scores(j):
        kj = k_ref[j * ATTN_TK:(j + 1) * ATTN_TK, :]
        return [_dot_nt(kj, qs[c]) for c in chains]

    s_next = scores(0)
    for j in range(n_kv):
        vj = v_ref[:, j * ATTN_TK:(j + 1) * ATTN_TK]
        s = s_next
        if j + 1 < n_kv:
            s_next = scores(j + 1)
        for c in chains:
            m_new = jnp.maximum(m[c], jnp.max(s[c], axis=0, keepdims=True))
            alpha = jnp.exp2(m[c] - m_new)
            p = jnp.exp2((s[c] - m_new).astype(BF16))
            acc[c] = alpha * acc[c] + _dot(vj, p)
            m[c] = m_new
    for c in chains:
        out = acc[c][0:V_HEAD_DIM, :] / acc[c][V_HEAD_DIM:V_HEAD_DIM + 1, :]
        o_ref[c * TM:(c + 1) * TM, :] = jnp.transpose(out).astype(o_ref.dtype)


def _attn_call(q, k, v, n_batch, t_len, tpb):
    rows_b = tpb * TM
    assert rows_b % ATTN_TK == 0 and t_len % ATTN_TQ == 0 and ATTN_TQ % TM == 0
    n_q = t_len // ATTN_TQ
    sub = ATTN_TQ // TM
    k3 = k.reshape(n_batch, rows_b, MLA_HEADS * QK_PAD_DIM)

    def q_spec(u):
        return pl.BlockSpec((TM, QK_PAD_DIM), lambda b, h, i: (b * tpb + 1 + i * sub + u, h))

    return pl.pallas_call(
        functools.partial(_attn_kernel, rows_b // ATTN_TK),
        grid=(n_batch, MLA_HEADS, n_q),
        in_specs=[q_spec(u) for u in range(sub)] + [
            pl.BlockSpec((None, rows_b, QK_PAD_DIM), lambda b, h, i: (b, 0, h)),
            pl.BlockSpec((None, None, VT_ROWS, rows_b), lambda b, h, i: (b, h, 0, 0)),
        ],
        out_specs=pl.BlockSpec((ATTN_TQ, V_HEAD_DIM), lambda b, h, i: (b * n_q + i, h)),
        out_shape=jax.ShapeDtypeStruct((n_batch * t_len, MLA_WIDTH), BF16),
        compiler_params=_cparams(("arbitrary", "arbitrary", "arbitrary")),
        name="attention",
    )(*([q] * sub), k3, v)


def _slot_rank(idx, run_ref):
    lane = lax.broadcasted_iota(jnp.int32, idx.shape, 1)
    oh0 = lane == idx[:, 0:1]
    oh1 = lane == idx[:, 1:2]
    both = jnp.where(oh0 | oh1, 1.0, 0.0)
    t_row = lax.broadcasted_iota(jnp.int32, (TM, TM), 0)
    t_col = lax.broadcasted_iota(jnp.int32, (TM, TM), 1)
    earlier = jnp.where(t_col < t_row, 1.0, 0.0).astype(BF16)
    seen = _dot(earlier, both.astype(BF16)) + run_ref[...]
    r0 = jnp.sum(jnp.where(oh0, seen, 0.0), axis=-1, keepdims=True)
    r1 = jnp.sum(jnp.where(oh1, seen, 0.0), axis=-1, keepdims=True)
    run_ref[...] = run_ref[...] + jnp.sum(both, axis=0, keepdims=True)
    return jnp.where(lane == 0, r0, jnp.where(lane == 1, r1, 0.0)).astype(jnp.int32)


def _route(logits):
    lane = lax.broadcasted_iota(jnp.int32, logits.shape, 1)
    neg = jnp.full_like(logits, -jnp.inf)
    big = jnp.full_like(lane, 2 ** 30)
    is_grp = lane < N_GROUPS
    gl = jnp.where(is_grp, logits, neg)
    ge = jnp.exp(gl - jnp.max(gl, axis=-1, keepdims=True))
    gp = ge / jnp.sum(ge, axis=-1, keepdims=True)
    g_val = jnp.max(gp, axis=-1, keepdims=True)
    g_idx = jnp.min(jnp.where(is_grp & (gp == g_val), lane, big), axis=-1, keepdims=True)
    e_lane = lane - N_GROUPS
    in_grp = (e_lane >= g_idx * EXPERTS_PER_GROUP) & (e_lane < (g_idx + 1) * EXPERTS_PER_GROUP)
    el = jnp.where(in_grp, logits, neg)
    ee = jnp.exp(el - jnp.max(el, axis=-1, keepdims=True))
    ep = ee / jnp.sum(ee, axis=-1, keepdims=True)
    v1 = jnp.max(ep, axis=-1, keepdims=True)
    i1 = jnp.min(jnp.where(in_grp & (ep == v1), lane, big), axis=-1, keepdims=True)
    rest = in_grp & (lane != i1)
    v2 = jnp.max(jnp.where(rest, ep, neg), axis=-1, keepdims=True)
    i2 = jnp.min(jnp.where(rest & (ep == v2), lane, big), axis=-1, keepdims=True)
    denom = v1 + v2
    idx = jnp.where(lane == 0, i1 - N_GROUPS, jnp.where(lane == 1, i2 - N_GROUPS, 0))
    gate = jnp.where(lane == 0, g_val * v1 / denom, jnp.where(lane == 1, g_val * v2 / denom, 0.0))
    return idx, gate


def _mix_kernel(x_ref, attn_ref, yf_ref, yb_ref, r_ref, k_ref, v_ref, af_ref, ab_ref, g_ref,
                g1_ref, sh2_ref, sc2_ref, akey_ref, rk_ref, lng_ref, lnb_ref, ng_ref,
                wo_ref, wr_ref, br_ref,
                x1_o, h2_o, idx_o, gate_o, rank_o, cnt_o, run_scr):
    @pl.when(pl.program_id(0) == 0)
    def _():
        run_scr[...] = jnp.zeros_like(run_scr)

    subs = range(MIX_SUB)
    rows = TM // MIX_SUB

    def part(ref, u):
        return ref[u * rows:(u + 1) * rows, :]

    ones_bd = _head_ones()
    inv = 1.0 / RWKV_HEAD_DIM
    y = [part(yf_ref, u) + part(yb_ref, u) for u in subs]
    mu = [_head_sum(y[u], ones_bd) * inv for u in subs]
    dy = [y[u] - mu[u] for u in subs]
    var = [_head_sum(dy[u] * dy[u], ones_bd) * inv for u in subs]
    k_sum = [part(k_ref, u) * (2.0 + (part(af_ref, u) + part(ab_ref, u) - 2.0) * akey_ref[...]) for u in subs]
    bonus = [_head_sum(part(r_ref, u) * k_sum[u] * rk_ref[...], ones_bd) * part(v_ref, u) for u in subs]
    yn = [dy[u] * lax.rsqrt(var[u] + LNX_EPS) * lng_ref[...] + lnb_ref[...] for u in subs]
    rw = [((yn[u] + bonus[u]) * part(g_ref, u)).astype(BF16) for u in subs]
    o = [_dot(part(attn_ref, u), wo_ref[0:MLA_WIDTH, :]) + _dot(rw[u], wo_ref[MLA_WIDTH:D_MODEL, :])
         for u in subs]
    x1 = [part(x_ref, u) + g1_ref[...] * o[u] for u in subs]
    h2 = [x1[u] * lax.rsqrt(jnp.mean(x1[u] * x1[u], axis=-1, keepdims=True) + NORM_EPS) * ng_ref[...]
          * (1.0 + sc2_ref[...]) + sh2_ref[...] for u in subs]
    h_hl = [_split2(h2[u]) for u in subs]
    both = [_dot(h_hl[u][0], wr_ref[...]) for u in subs]
    lo_hi = [_dot(h_hl[u][1], wr_ref[:, 0:ROUTER_COLS]) for u in subs]
    routed = [_route(both[u][:, 0:ROUTER_COLS] + both[u][:, ROUTER_COLS:2 * ROUTER_COLS] + lo_hi[u] + br_ref[...])
              for u in subs]
    for u in subs:
        sl = slice(u * rows, (u + 1) * rows)
        x1_o[sl, :] = x1[u]
        h2_o[sl, :] = h2[u]
        idx_o[sl, :] = routed[u][0]
        gate_o[sl, :] = routed[u][1]
    rank_o[...] = _slot_rank(jnp.concatenate([routed[u][0] for u in subs], axis=0), run_scr)
    cnt_o[...] = run_scr[...]


def _mix_call(x, attn, yscan, r, k, v, lr, g, mod_tab, key_a, bonus_rk, lnx_g, lnx_b, norm_g,
              w_out_b, w_router, b_router, tpb):
    n_batch, t_len, _ = x.shape
    tpl = t_len // TM
    n = n_batch * t_len

    def lat(i):
        return (i // tpl) * tpb + 1 + i % tpl

    def tok(cols, col_blk=0):
        return pl.BlockSpec((TM, cols), lambda i: (lat(i), col_blk))

    def mod_spec(kk):
        return pl.BlockSpec((None, 1, D_MODEL), lambda i: ((i // tpl) * 6 + kk, 0, 0))

    def vec(cols):
        return _resident((1, cols), lambda i: (0, 0))

    tile_out = pl.BlockSpec((TM, ROUTER_COLS), lambda i: (i, 0))
    return pl.pallas_call(
        _mix_kernel,
        grid=(n // TM,),
        in_specs=[
            pl.BlockSpec((None, TM, D_MODEL), lambda i: (i // tpl, i % tpl, 0)),
            pl.BlockSpec((TM, MLA_WIDTH), lambda i: (i, 0)),
            tok(RWKV_WIDTH), tok(RWKV_WIDTH),
            tok(RWKV_WIDTH), tok(RWKV_WIDTH), tok(RWKV_WIDTH),
            tok(RWKV_WIDTH, 0), tok(RWKV_WIDTH, 1), tok(RWKV_WIDTH),
            mod_spec(2), mod_spec(3), mod_spec(4),
            vec(RWKV_WIDTH), vec(RWKV_WIDTH), vec(RWKV_WIDTH), vec(RWKV_WIDTH), vec(D_MODEL),
            _resident((D_MODEL, D_MODEL), lambda i: (0, 0)),
            _resident((D_MODEL, 2 * ROUTER_COLS), lambda i: (0, 0)),
            vec(ROUTER_COLS),
        ],
        out_specs=[
            pl.BlockSpec((TM, D_MODEL), lambda i: (i, 0)),
            pl.BlockSpec((TM, D_MODEL), lambda i: (i, 0)),
            tile_out, tile_out, tile_out,
            pl.BlockSpec((1, ROUTER_COLS), lambda i: (0, 0)),
        ],
        out_shape=[
            jax.ShapeDtypeStruct((n, D_MODEL), F32),
            jax.ShapeDtypeStruct((n, D_MODEL), F32),
            jax.ShapeDtypeStruct((n, ROUTER_COLS), jnp.int32),
            jax.ShapeDtypeStruct((n, ROUTER_COLS), F32),
            jax.ShapeDtypeStruct((n, ROUTER_COLS), jnp.int32),
            jax.ShapeDtypeStruct((1, ROUTER_COLS), F32),
        ],
        scratch_shapes=[pltpu.VMEM((1, ROUTER_COLS), F32)],
        compiler_params=_cparams(("arbitrary",)),
        name="mix",
    )(x, attn, yscan[0], yscan[1], r, k, v, lr, lr, g, mod_tab, mod_tab, mod_tab,
      key_a, bonus_rk, lnx_g, lnx_b, norm_g.reshape(1, D_MODEL), w_out_b, w_router, b_router)


def _scatter_kernel(dest_ref, pad_lo_ref, pad_hi_ref, used_ref, h_ref, xs_hbm, zbuf, sem, zsem):
    base = pl.program_id(0) * (TM * TOP_K)
    n_blocks = xs_hbm.shape[0] // MOE_BLOCK

    def zero_row(slot):
        return pltpu.make_async_copy(zbuf.at[pl.ds(0, 1)], xs_hbm.at[pl.ds(slot, 1)], zsem)

    def zero_block(b):
        return pltpu.make_async_copy(zbuf, xs_hbm.at[pl.ds(pl.multiple_of(b * MOE_BLOCK, MOE_BLOCK), MOE_BLOCK)], zsem)

    def each_pad_row(fn):
        def per_expert(e, c):
            return lax.fori_loop(pad_lo_ref[e], pad_hi_ref[e], fn, c)
        lax.fori_loop(0, N_EXPERTS, per_expert, 0)

    @pl.when(pl.program_id(0) == 0)
    def _():
        zbuf[...] = jnp.zeros_like(zbuf)

        def start_row(r, c):
            zero_row(r).start()
            return c

        def start_block(b, c):
            zero_block(b).start()
            return c

        each_pad_row(start_row)
        lax.fori_loop(used_ref[0], n_blocks, start_block, 0)

    @pl.when(pl.program_id(0) == pl.num_programs(0) - 1)
    def _():
        def wait_row(r, c):
            zero_row(0).wait()
            return c

        def wait_block(b, c):
            zero_block(0).wait()
            return c

        each_pad_row(wait_row)
        lax.fori_loop(used_ref[0], n_blocks, wait_block, 0)

    def row(t, slot):
        return pltpu.make_async_copy(h_ref.at[pl.ds(t, 1)], xs_hbm.at[pl.ds(slot, 1)], sem)

    def start(t, c):
        for kk in range(TOP_K):
            row(t, dest_ref[base + t * TOP_K + kk]).start()
        return c
    lax.fori_loop(0, TM, start, 0, unroll=DMA_UNROLL)

    def wait(t, c):
        for _ in range(TOP_K):
            row(t, 0).wait()
        return c
    lax.fori_loop(0, TM, wait, 0, unroll=DMA_UNROLL)


def _scatter_call(dest, pad_lo, pad_hi, n_used, h2, n_slots):
    n = h2.shape[0]
    return pl.pallas_call(
        _scatter_kernel,
        grid_spec=pltpu.PrefetchScalarGridSpec(
            num_scalar_prefetch=4,
            grid=(n // TM,),
            in_specs=[pl.BlockSpec((TM, D_MODEL), lambda i, *_: (i, 0))],
            out_specs=pl.BlockSpec(memory_space=pl.ANY),
            scratch_shapes=[pltpu.VMEM((MOE_BLOCK, D_MODEL), F32),
                            pltpu.SemaphoreType.DMA, pltpu.SemaphoreType.DMA],
        ),
        out_shape=jax.ShapeDtypeStruct((n_slots, D_MODEL), F32),
        compiler_params=_cparams(("arbitrary",)),
        name="scatter",
    )(dest, pad_lo, pad_hi, n_used, h2)


def _moe_kernel(be_ref, used_ref, x_ref, w1_ref, w3_ref, w2_ref, y_ref, w1b, w3b, w2b):
    i = pl.program_id(0)

    @pl.when(i < used_ref[0])
    def _():
        @pl.when((i == 0) | (be_ref[i] != be_ref[jnp.maximum(i - 1, 0)]))
        def _():
            w1b[...] = w1_ref[...].astype(BF16)
            w3b[...] = w3_ref[...].astype(BF16)
            w2b[...] = w2_ref[...].astype(BF16)

        x = x_ref[...].astype(BF16)
        a1 = _dot(x, w1b[...])
        a3 = _dot(x, w3b[...])
        hm = (a1 * jax.nn.sigmoid(a1) * a3).astype(BF16)
        y_ref[...] = _dot(hm, w2b[...])

    @pl.when(i >= used_ref[0])
    def _():
        y_ref[...] = jnp.zeros_like(y_ref)


def _moe_call(block_expert, n_used, xs, w1, w3, w2):
    n_blocks = block_expert.shape[0]

    def wspec(shape):
        return pl.BlockSpec((None,) + shape, lambda i, be, used: (be[i], 0, 0))

    return pl.pallas_call(
        _moe_kernel,
        grid_spec=pltpu.PrefetchScalarGridSpec(
            num_scalar_prefetch=2,
            grid=(n_blocks,),
            in_specs=[
                pl.BlockSpec((MOE_BLOCK, D_MODEL), lambda i, be, used: (jnp.minimum(i, used[0] - 1), 0)),
                wspec((D_MODEL, D_EXPERT)),
                wspec((D_MODEL, D_EXPERT)),
                wspec((D_EXPERT, D_MODEL)),
            ],
            out_specs=pl.BlockSpec((MOE_BLOCK, D_MODEL), lambda i, be, used: (i, 0)),
            scratch_shapes=[
                pltpu.VMEM((D_MODEL, D_EXPERT), BF16),
                pltpu.VMEM((D_MODEL, D_EXPERT), BF16),
                pltpu.VMEM((D_EXPERT, D_MODEL), BF16),
            ],
        ),
        out_shape=jax.ShapeDtypeStruct(xs.shape, F32),
        compiler_params=_cparams(("arbitrary",)),
        name="moe",
    )(block_expert, n_used, xs, w1, w3, w2)


def _final_kernel(dest_ref, x1_ref, gate_ref, g2_ref, ng_ref, ys_hbm, o_ref, ybuf, sems):
    i = pl.program_id(0)
    last = pl.num_programs(0) - 1
    cur = i % 2

    def row(buf, t, kk, slot):
        return pltpu.make_async_copy(ys_hbm.at[pl.ds(slot, 1)], ybuf.at[buf, kk, pl.ds(t, 1)], sems.at[buf])

    def fetch(buf, tile, t):
        for kk in range(TOP_K):
            row(buf, t, kk, dest_ref[(tile * TM + t) * TOP_K + kk]).start()

    def wait_all(buf):
        def wait(t, c):
            for kk in range(TOP_K):
                row(buf, t, kk, 0).wait()
            return c
        lax.fori_loop(0, TM, wait, 0, unroll=DMA_UNROLL)

    @pl.when(i == 0)
    def _():
        def start(t, c):
            fetch(0, 0, t)
            return c
        lax.fori_loop(0, TM, start, 0, unroll=DMA_UNROLL)

    wait_all(cur)
    nxt = jnp.minimum(i + 1, last)
    for t in range(TM):
        fetch(1 - cur, nxt, t)

    gate = gate_ref[...]
    y = ybuf[cur, 0] * gate[:, 0:1] + ybuf[cur, 1] * gate[:, 1:2]
    x2 = x1_ref[...] + g2_ref[...] * y
    o_ref[...] = x2 * lax.rsqrt(jnp.mean(x2 * x2, axis=-1, keepdims=True) + NORM_EPS) * ng_ref[...]

    @pl.when(i == last)
    def _():
        wait_all(1 - cur)


def _final_call(dest, x1, ys, gates, mod_tab, final_g, t_len):
    n = x1.shape[0]
    tpl = t_len // TM
    return pl.pallas_call(
        _final_kernel,
        grid_spec=pltpu.PrefetchScalarGridSpec(
            num_scalar_prefetch=1,
            grid=(n // TM,),
            in_specs=[
                pl.BlockSpec((TM, D_MODEL), lambda i, dest: (i, 0)),
                pl.BlockSpec((TM, ROUTER_COLS), lambda i, dest: (i, 0)),
                pl.BlockSpec((None, 1, D_MODEL), lambda i, dest: ((i // tpl) * 6 + 5, 0, 0)),
                _resident((1, D_MODEL), lambda i, dest: (0, 0)),
                pl.BlockSpec(memory_space=pl.ANY),
            ],
            out_specs=pl.BlockSpec((TM, D_MODEL), lambda i, dest: (i, 0)),
            scratch_shapes=[pltpu.VMEM((2, TOP_K, TM, D_MODEL), F32), pltpu.SemaphoreType.DMA((2,))],
        ),
        out_shape=jax.ShapeDtypeStruct((n, D_MODEL), F32),
        compiler_params=_cparams(("arbitrary",)),
        name="final",
    )(dest, x1, gates, mod_tab, final_g.reshape(1, D_MODEL), ys)


def _pad_cols(w, width):
    return jnp.pad(w, ((0, 0), (0, width - w.shape[1])))


_ROPE_SWAP = np.concatenate([np.arange(16, 32), np.arange(0, 16), np.arange(48, 64), np.arange(32, 48)])


def _rope_tables(t_len):
    pos = jnp.arange(t_len)
    inv_freq = ROPE_THETA ** (-jnp.arange(0, ROPE_AXIS_DIM, 2, dtype=F32) / ROPE_AXIS_DIM)
    ang_r = (pos // GRID_W)[:, None].astype(F32) * inv_freq
    ang_c = (pos % GRID_W)[:, None].astype(F32) * inv_freq
    cos = jnp.concatenate([jnp.cos(ang_r)] * 2 + [jnp.cos(ang_c)] * 2, axis=1)
    sin = jnp.concatenate([-jnp.sin(ang_r), jnp.sin(ang_r), -jnp.sin(ang_c), jnp.sin(ang_c)], axis=1)
    cos = jnp.concatenate([jnp.ones((CTX_LEN, QK_ROPE_DIM), F32), cos], axis=0)
    sin = jnp.concatenate([jnp.zeros((CTX_LEN, QK_ROPE_DIM), F32), sin], axis=0)
    rows = cos.shape[0]
    z64 = jnp.zeros((rows, QK_ROPE_DIM), F32)
    ck = jnp.concatenate([cos, z64], axis=1)
    sk = jnp.concatenate([sin, z64], axis=1)
    q_scale = MLA_SCALE * math.log2(math.e)
    cq = q_scale * jnp.concatenate([jnp.ones((rows, QK_NOPE_DIM), F32), cos, z64], axis=1)
    sq = q_scale * jnp.concatenate([jnp.zeros((rows, QK_NOPE_DIM), F32), sin, z64], axis=1)
    return ck, sk, cq, sq


def _slot_tables(idx2, rank2, counts, n_tokens):
    n_blocks = (n_tokens * TOP_K + N_EXPERTS * (MOE_BLOCK - 1) + MOE_BLOCK - 1) // MOE_BLOCK
    padded = (counts + MOE_BLOCK - 1) // MOE_BLOCK * MOE_BLOCK
    pad_end = jnp.cumsum(padded)
    pad_start = pad_end - padded
    experts = jnp.arange(N_EXPERTS, dtype=jnp.int32)
    first = jnp.sum(jnp.where(idx2[..., None] == experts, pad_start, 0), axis=-1)
    dest = (first + rank2).reshape(-1).astype(jnp.int32)
    block_start = jnp.arange(n_blocks, dtype=jnp.int32) * MOE_BLOCK
    block_expert = jnp.minimum(jnp.sum(block_start[:, None] >= pad_end[None, :], axis=1), N_EXPERTS - 1)
    n_used = (pad_end[-1] // MOE_BLOCK).reshape(1)
    pad_rows = ((pad_start + counts).astype(jnp.int32), pad_end.astype(jnp.int32))
    return dest, pad_rows, block_expert.astype(jnp.int32), n_used.astype(jnp.int32), n_blocks * MOE_BLOCK


def kernel(x, c, ctx, c_ctx, w_mod, b_mod, norm_attn_g, norm_ffn_g, w_in, shift_mu, q_norm_g, w_uq, kv_norm_g, w_ukv, decay_w0, decay_up, iclr_a0, iclr_up, gate_up, key_k, key_a, bonus_r_k, lnx_g, lnx_b, w_out, w_grp, b_grp, w_exp, b_exp, w1, w3, w2, final_norm_g):
    n_batch, t_len, _ = x.shape
    assert ctx.shape[1] == CTX_LEN == TM and t_len % TM == 0 and w_mod.shape[0] == 1
    tpb = (CTX_LEN + t_len) // TM
    n = n_batch * t_len

    c_rows = jnp.zeros((8, D_MODEL), F32).at[:n_batch].set(c).at[n_batch].set(c_ctx)
    mod_tab = _mod_call(c_rows, w_mod[0], b_mod[0]).reshape(8 * 6, 1, D_MODEL)

    wi = w_in[0]
    w_kr = wi[:, 768:MLA_IN]
    o = MLA_IN
    w_in_p = jnp.concatenate([
        wi[:, 0:768], _pad_cols(w_kr, LANES), _pad_cols(w_kr[:, _ROPE_SWAP], LANES),
        wi[:, o:o + COLS_RKV],
        _pad_cols(wi[:, o + COLS_RKV:o + COLS_RKV + DECAY_LORA], LANES),
        _pad_cols(wi[:, o + COLS_RKV + DECAY_LORA:o + COLS_RKV + DECAY_LORA + ICLR_LORA], LANES),
        _pad_cols(wi[:, o + COLS_RKV + DECAY_LORA + ICLR_LORA:], 2 * LANES),
    ], axis=1).astype(BF16)
    mu = shift_mu[0]
    mu_rkv = mu[:, 0:COLS_RKV]
    mu_lora = jnp.concatenate([
        _pad_cols(mu[:, COLS_RKV:COLS_RKV + DECAY_LORA], LANES),
        _pad_cols(mu[:, COLS_RKV + DECAY_LORA:COLS_RKV + DECAY_LORA + ICLR_LORA], LANES),
        _pad_cols(mu[:, COLS_RKV + DECAY_LORA + ICLR_LORA:], 2 * LANES)], axis=1)

    def lora_up(w):
        both = jnp.concatenate([w[0], w[1]], axis=1)
        return jnp.pad(both, ((0, LANES - both.shape[0]), (0, 0))).astype(BF16)

    gup = jnp.pad(gate_up[0], ((0, 2 * LANES - GATE_LORA), (0, 0))).astype(BF16)
    p_mla, r, k, v, lw, lr, g = _project_call(
        x, ctx, mod_tab, norm_attn_g[0], w_in_p, mu_rkv, mu_lora, lora_up(decay_up[0]), lora_up(iclr_up[0]), gup,
        decay_w0[0].reshape(1, -1), iclr_a0[0].reshape(1, -1), tpb)
    key_k2 = key_k[0].reshape(1, -1)
    key_a2 = key_a[0].reshape(1, -1)
    scan_rows = SCAN_SUB * CHUNK
    assert CTX_LEN % scan_rows == 0 and t_len % scan_rows == 0
    y_dirs = [_scan_call(rev, r, k, v, lw, lr, key_k2, key_a2, n_batch,
                         (CTX_LEN + t_len) // scan_rows, CTX_LEN // scan_rows) for rev in (False, True)]

    hd = QK_NOPE_DIM + QK_ROPE_DIM
    wq = w_uq[0].reshape(Q_LORA_RANK, MLA_HEADS, hd)
    zq = jnp.zeros((Q_LORA_RANK, MLA_HEADS, QK_ROPE_DIM), F32)
    wa = jnp.concatenate([wq, zq], axis=2).reshape(Q_LORA_RANK, -1).astype(BF16)
    wb = jnp.concatenate([jnp.zeros((Q_LORA_RANK, MLA_HEADS, QK_NOPE_DIM), F32),
                          wq[:, :, QK_NOPE_DIM:][:, :, _ROPE_SWAP], zq], axis=2
                         ).reshape(Q_LORA_RANK, -1).astype(BF16)
    wkv3 = w_ukv[0].reshape(KV_LORA_RANK, MLA_HEADS, QK_NOPE_DIM + V_HEAD_DIM)
    wk = wkv3[:, :, :QK_NOPE_DIM].reshape(KV_LORA_RANK, -1).astype(BF16)
    wvt = wkv3[:, :, QK_NOPE_DIM:].reshape(KV_LORA_RANK, -1).T.astype(BF16)
    q, kmat, vmat = _mla_prep_call(p_mla, _rope_tables(t_len), q_norm_g[0], kv_norm_g[0], wa, wb, wk, wvt,
                                   n_batch, tpb)
    attn = _attn_call(q, kmat, vmat, n_batch, t_len, tpb)

    w_router = _pad_cols(jnp.concatenate([w_grp[0], w_exp[0]], axis=1), ROUTER_COLS)
    w_router_hi = w_router.astype(BF16)
    w_router2 = jnp.concatenate([w_router_hi, (w_router - w_router_hi.astype(F32)).astype(BF16)], axis=1)
    b_router = _pad_cols(jnp.concatenate([b_grp[0], b_exp[0]]).reshape(1, -1), ROUTER_COLS)
    x1, h2, idx, gates, rank, counts = _mix_call(
        x, attn, y_dirs, r, k, v, lr, g, mod_tab, key_a2, bonus_r_k[0].reshape(1, -1),
        lnx_g[0].reshape(1, -1), lnx_b[0].reshape(1, -1), norm_ffn_g[0],
        w_out[0].astype(BF16), w_router2, b_router, tpb)

    dest, pad_rows, block_expert, n_used, n_slots = _slot_tables(
        idx[:, :TOP_K], rank[:, :TOP_K], counts[0, :N_EXPERTS].astype(jnp.int32), n)
    xs = _scatter_call(dest, pad_rows[0], pad_rows[1], n_used, h2, n_slots)
    ys = _moe_call(block_expert, n_used, xs, w1[0], w3[0], w2[0])
    out = _final_call(dest, x1, ys, gates, mod_tab, final_norm_g, t_len)
    return out.reshape(n_batch, t_len, D_MODEL)
```
